```python
import jax, jax.numpy as jnp
from jax import lax
import numpy as np

D_MODEL = 1024
BATCH = 8
SEQ = 8192
DEPTH = 2

GRID_W = 64
QBLK = 128
ROPE_THETA = 10000.0
HEAD_DIM = 64
A_HEADS = 8
A_KV = 2
B_HEADS = 8
B_KV = 2
WINDOW = 128
AB_IN = (A_HEADS + 2 * A_KV + B_HEADS + 2 * B_KV) * HEAD_DIM
AB_MIX = (A_HEADS + B_HEADS) * HEAD_DIM
MLA_HEADS = 16
Q_LORA = 256
KV_LORA = 128
QK_NOPE = 64
QK_ROPE = 32
V_DIM = 64
MLA_DOWN = Q_LORA + KV_LORA + QK_ROPE
N_EXPERTS = 16
EC_FACTOR = 2
D_EXPERT = 1024
N_EVEN = (DEPTH + 1) // 2
N_ODD = DEPTH // 2
ALPHA = (2.0 * DEPTH) ** 0.25
BETA = (8.0 * DEPTH) ** -0.25
NEG_INF = -1e30

kernel_name = "hybrid_axialgqa_swa_mla_ecmoe_deepnorm"


def rms_norm(x, g, eps=1e-6):
    xf = x.astype(jnp.float32)
    y = xf * lax.rsqrt(jnp.mean(xf * xf, axis=-1, keepdims=True) + eps)
    return (y * g.astype(jnp.float32)).astype(x.dtype)


def layer_norm(x, g, b, eps=1e-5):
    xf = x.astype(jnp.float32)
    mu = jnp.mean(xf, axis=-1, keepdims=True)
    var = jnp.mean(jnp.square(xf - mu), axis=-1, keepdims=True)
    y = (xf - mu) * lax.rsqrt(var + eps)
    return (y * g.astype(jnp.float32) + b.astype(jnp.float32)).astype(x.dtype)


def rope_angles(pos, dim):
    freqs = ROPE_THETA ** (-(jnp.arange(0, dim, 2, dtype=jnp.float32) / dim))
    return pos[:, None] * freqs[None, :]


def apply_rope(x, ang):
    cos = jnp.cos(ang)[None, :, None, :]
    sin = jnp.sin(ang)[None, :, None, :]
    xf = x.astype(jnp.float32)
    x1, x2 = jnp.split(xf, 2, axis=-1)
    return jnp.concatenate([x1 * cos - x2 * sin, x2 * cos + x1 * sin], axis=-1).astype(x.dtype)


def apply_axial_rope(x, ang_row, ang_col):
    half = x.shape[-1] // 2
    return jnp.concatenate([apply_rope(x[..., :half], ang_row),
                            apply_rope(x[..., half:], ang_col)], axis=-1)


def dense_attention(q, k, v, scale):
    B, S, H, dq = q.shape
    KV = k.shape[2]
    G = H // KV
    dv = v.shape[-1]
    nb = S // QBLK
    qb = q.reshape(B, nb, QBLK, KV, G, dq).transpose(1, 0, 2, 3, 4, 5)

    def one_block(qblk):
        s = jnp.einsum('bqkgd,bskd->bkgqs', qblk, k, preferred_element_type=jnp.float32) * scale
        p = jax.nn.softmax(s, axis=-1).astype(v.dtype)
        return jnp.einsum('bkgqs,bskd->bqkgd', p, v)

    o = lax.map(one_block, qb)
    return o.transpose(1, 0, 2, 3, 4, 5).reshape(B, S, H * dv)


def window_attention(q, k, v, sink, scale):
    B, S, H, d = q.shape
    KV = k.shape[2]
    G = H // KV
    dv = v.shape[-1]
    nb = S // QBLK
    span = QBLK + 2 * WINDOW
    pad = ((0, 0), (WINDOW, WINDOW), (0, 0), (0, 0))
    kp = jnp.pad(k, pad)
    vp = jnp.pad(v, pad)
    qb = q.reshape(B, nb, QBLK, KV, G, d).transpose(1, 0, 2, 3, 4, 5)
    offs = jnp.arange(span) - WINDOW
    qa = jnp.arange(QBLK)
    rel_ok = jnp.abs(offs[None, :] - qa[:, None]) <= WINDOW
    sink_l = sink.astype(jnp.float32).reshape(KV, G)[None, :, :, None, None]

    def one_block(args):
        qblk, i = args
        start = i * QBLK
        kw = lax.dynamic_slice_in_dim(kp, start, span, axis=1)
        vw = lax.dynamic_slice_in_dim(vp, start, span, axis=1)
        kpos = start + offs
        valid = rel_ok & ((kpos >= 0) & (kpos < S))[None, :]
        s = jnp.einsum('bqkgd,bskd->bkgqs', qblk, kw, preferred_element_type=jnp.float32) * scale
        s = jnp.where(valid[None, None, None], s, NEG_INF)
        s = jnp.concatenate([s, jnp.broadcast_to(sink_l, s.shape[:-1] + (1,))], axis=-1)
        p = jax.nn.softmax(s, axis=-1)[..., :-1].astype(vw.dtype)
        return jnp.einsum('bkgqs,bskd->bqkgd', p, vw)

    o = lax.map(one_block, (qb, jnp.arange(nb)))
    return o.transpose(1, 0, 2, 3, 4, 5).reshape(B, S, H * dv)


def mix_ab(x, w_in, q_norm, k_norm, sink, w_out):
    B, S, _ = x.shape
    proj = x @ w_in
    sizes = [A_HEADS * HEAD_DIM, A_KV * HEAD_DIM, A_KV * HEAD_DIM,
             B_HEADS * HEAD_DIM, B_KV * HEAD_DIM, B_KV * HEAD_DIM]
    cuts = list(np.cumsum(sizes)[:-1])
    qa, ka, va, qb, kb, vb = jnp.split(proj, cuts, axis=-1)
    rows = S // GRID_W
    row_pos = jnp.repeat(jnp.arange(rows, dtype=jnp.float32), GRID_W)
    col_pos = jnp.tile(jnp.arange(GRID_W, dtype=jnp.float32), rows)
    seq_pos = jnp.arange(S, dtype=jnp.float32)
    ang_row = rope_angles(row_pos, HEAD_DIM // 2)
    ang_col = rope_angles(col_pos, HEAD_DIM // 2)
    ang_seq = rope_angles(seq_pos, HEAD_DIM)
    qa = apply_axial_rope(rms_norm(qa.reshape(B, S, A_HEADS, HEAD_DIM), q_norm), ang_row, ang_col)
    ka = apply_axial_rope(rms_norm(ka.reshape(B, S, A_KV, HEAD_DIM), k_norm), ang_row, ang_col)
    va = va.reshape(B, S, A_KV, HEAD_DIM)
    oa = dense_attention(qa, ka, va, HEAD_DIM ** -0.5)
    qb = apply_rope(qb.reshape(B, S, B_HEADS, HEAD_DIM), ang_seq)
    kb = apply_rope(kb.reshape(B, S, B_KV, HEAD_DIM), ang_seq)
    vb = vb.reshape(B, S, B_KV, HEAD_DIM)
    ob = window_attention(qb, kb, vb, sink, HEAD_DIM ** -0.5)
    return jnp.concatenate([oa, ob], axis=-1) @ w_out


def mix_mla(x, w_down, q_norm, kv_norm, w_uq, w_ukv, w_out):
    B, S, _ = x.shape
    proj = x @ w_down
    c_q, c_kv, k_r = jnp.split(proj, [Q_LORA, Q_LORA + KV_LORA], axis=-1)
    q = (rms_norm(c_q, q_norm) @ w_uq).reshape(B, S, MLA_HEADS, QK_NOPE + QK_ROPE)
    kv = (rms_norm(c_kv, kv_norm) @ w_ukv).reshape(B, S, MLA_HEADS, QK_NOPE + V_DIM)
    q_nope, q_rope = jnp.split(q, [QK_NOPE], axis=-1)
    k_nope, v = jnp.split(kv, [QK_NOPE], axis=-1)
    ang = rope_angles(jnp.arange(S, dtype=jnp.float32), QK_ROPE)
    q_rope = apply_rope(q_rope, ang)
    k_r = apply_rope(k_r[:, :, None, :], ang)
    q = jnp.concatenate([q_nope, q_rope], axis=-1)
    k = jnp.concatenate([k_nope, jnp.broadcast_to(k_r, (B, S, MLA_HEADS, QK_ROPE))], axis=-1)
    o = dense_attention(q, k, v, (QK_NOPE + QK_ROPE) ** -0.5)
    return o @ w_out


def ec_moe(x, w_router, w_gate, w_up, w_down):
    B, S, D = x.shape
    cap = EC_FACTOR * S // N_EXPERTS
    logits = jnp.einsum('bsd,de->bse', x, w_router, preferred_element_type=jnp.float32)
    aff = jax.nn.softmax(logits, axis=-1)
    g, idx = lax.top_k(aff.transpose(0, 2, 1), cap)
    xg = jax.vmap(lambda xb, ib: xb[ib])(x, idx)
    h = jax.nn.silu(jnp.einsum('becd,edf->becf', xg, w_gate)) * jnp.einsum('becd,edf->becf', xg, w_up)
    y = jnp.einsum('becf,efd->becd', h, w_down) * g[..., None].astype(x.dtype)
    return jax.vmap(lambda yb, ib: jnp.zeros((S, D), x.dtype).at[ib.reshape(-1)].add(yb.reshape(-1, D)))(y, idx)


def setup_inputs(seed: int = 0) -> dict:
    key = jax.random.key(seed)
    ks = iter(jax.random.split(key, 32))

    def nrm(shape, scale):
        return jax.random.normal(next(ks), shape, jnp.float32) * scale

    D = D_MODEL
    return {
        "x": nrm((BATCH, SEQ, D), 1.0),
        "ab_w_in": nrm((N_EVEN, D, AB_IN), D ** -0.5),
        "ab_q_norm": 1.0 + nrm((N_EVEN, HEAD_DIM), 0.02),
        "ab_k_norm": 1.0 + nrm((N_EVEN, HEAD_DIM), 0.02),
        "ab_sink": nrm((N_EVEN, B_HEADS), 1.0),
        "ab_w_out": nrm((N_EVEN, AB_MIX, D), BETA * AB_MIX ** -0.5),
        "mla_w_down": nrm((N_ODD, D, MLA_DOWN), D ** -0.5),
        "mla_q_norm": 1.0 + nrm((N_ODD, Q_LORA), 0.02),
        "mla_kv_norm": 1.0 + nrm((N_ODD, KV_LORA), 0.02),
        "mla_w_uq": nrm((N_ODD, Q_LORA, MLA_HEADS * (QK_NOPE + QK_ROPE)), Q_LORA ** -0.5),
        "mla_w_ukv": nrm((N_ODD, KV_LORA, MLA_HEADS * (QK_NOPE + V_DIM)), KV_LORA ** -0.5),
        "mla_w_out": nrm((N_ODD, MLA_HEADS * V_DIM, D), BETA * (MLA_HEADS * V_DIM) ** -0.5),
        "ln_mix_g": 1.0 + nrm((DEPTH, D), 0.02),
        "ln_mix_b": nrm((DEPTH, D), 0.02),
        "moe_router": nrm((DEPTH, D, N_EXPERTS), D ** -0.5),
        "moe_w_gate": nrm((DEPTH, N_EXPERTS, D, D_EXPERT), D ** -0.5),
        "moe_w_up": nrm((DEPTH, N_EXPERTS, D, D_EXPERT), D ** -0.5),
        "moe_w_down": nrm((DEPTH, N_EXPERTS, D_EXPERT, D), BETA * D_EXPERT ** -0.5),
        "ln_ffn_g": 1.0 + nrm((DEPTH, D), 0.02),
        "ln_ffn_b": nrm((DEPTH, D), 0.02),
    }


def reference(x, ab_w_in, ab_q_norm, ab_k_norm, ab_sink, ab_w_out,
              mla_w_down, mla_q_norm, mla_kv_norm, mla_w_uq, mla_w_ukv, mla_w_out,
              ln_mix_g, ln_mix_b, moe_router, moe_w_gate, moe_w_up, moe_w_down,
              ln_ffn_g, ln_ffn_b):
    for l in range(DEPTH):
        i = l // 2
        if l % 2 == 0:
            h = mix_ab(x, ab_w_in[i], ab_q_norm[i], ab_k_norm[i], ab_sink[i], ab_w_out[i])
        else:
            h = mix_mla(x, mla_w_down[i], mla_q_norm[i], mla_kv_norm[i],
                        mla_w_uq[i], mla_w_ukv[i], mla_w_out[i])
        x = layer_norm(ALPHA * x + h, ln_mix_g[l], ln_mix_b[l])
        f = ec_moe(x, moe_router[l], moe_w_gate[l], moe_w_up[l], moe_w_down[l])
        x = layer_norm(ALPHA * x + f, ln_ffn_g[l], ln_ffn_b[l])
    return x
```

```python
import functools

import jax
import jax.numpy as jnp
from jax import lax
from jax.experimental import pallas as pl
from jax.experimental.pallas import tpu as pltpu

F32 = jnp.float32
BF16 = jnp.bfloat16

GRID_W = 64
ROPE_THETA = 10000.0
HEAD_DIM = 64
A_HEADS, A_KV = 8, 2
B_HEADS, B_KV = 8, 2
WINDOW = 128
MLA_HEADS = 16
Q_LORA, KV_LORA = 256, 128
QK_NOPE, QK_ROPE, V_DIM = 64, 32, 64
N_EXPERTS = 16
EC_FACTOR = 2
DEPTH = 2
ALPHA = (2.0 * DEPTH) ** 0.25
NEG_INF = -1e30
RMS_EPS = 1e-6
LN_EPS = 1e-5

LANES = 128
VMEM_LIMIT = 56 * 1024 * 1024

PROJ_ROWS = 512
ATTN_ROWS = 1024
ATTN_KEYS = 512
WIN_Q = 256
MOE_TOKENS = 2048
MOE_SLOTS = 128
CUMSUM_CHUNK = 256


def _cparams(sem):
    return pltpu.CompilerParams(dimension_semantics=sem, vmem_limit_bytes=VMEM_LIMIT)


def _rope(x, c, s_lo, s_hi, shift):
    return x * c + pltpu.roll(x, LANES - shift, 1) * s_lo + pltpu.roll(x, shift, 1) * s_hi


def _rms(x, g, n):
    ms = jnp.sum(x * x, axis=1, keepdims=True) * (1.0 / n)
    return x * lax.rsqrt(ms + RMS_EPS) * g


def _layer_norm(y, g, b):
    mu = jnp.mean(y, axis=1, keepdims=True)
    d = y - mu
    var = jnp.mean(d * d, axis=1, keepdims=True)
    return d * lax.rsqrt(var + LN_EPS) * g + b


def _proj_ab_kernel(x_ref, w_ref, gq_ref, gk_ref, ca_ref, sla_ref, sha_ref, cb_ref, slb_ref, shb_ref,
                    qa_ref, kta_ref, va_ref, qb_ref, ktb_ref, vb_ref):
    xb = x_ref[0].astype(BF16)
    scale = HEAD_DIM ** -0.5
    ca, sla, sha = ca_ref[...], sla_ref[...], sha_ref[...]
    cb, slb, shb = cb_ref[...], slb_ref[...], shb_ref[...]
    n_groups = A_HEADS + 2 * A_KV + B_HEADS + 2 * B_KV
    for c in range(n_groups // 2):
        pr = jnp.dot(xb, w_ref[:, c * 2 * LANES:(c + 1) * 2 * LANES], preferred_element_type=F32)
        for half in range(2):
            g = 2 * c + half
            ph = pr[:, half * LANES:(half + 1) * LANES]
            if g < A_HEADS:
                q = _rope(_rms(ph, gq_ref[...], HEAD_DIM), ca, sla, sha, HEAD_DIM // 4) * scale
                qa_ref[0, g] = q.astype(BF16)
            elif g < A_HEADS + A_KV:
                k = _rope(_rms(ph, gk_ref[...], HEAD_DIM), ca, sla, sha, HEAD_DIM // 4)
                kta_ref[0, g - A_HEADS] = k.T.astype(BF16)
            elif g < A_HEADS + 2 * A_KV:
                va_ref[0, g - A_HEADS - A_KV] = ph.astype(BF16)
            elif g < A_HEADS + 2 * A_KV + B_HEADS:
                q = _rope(ph, cb, slb, shb, HEAD_DIM // 2) * scale
                qb_ref[0, g - A_HEADS - 2 * A_KV] = q.astype(BF16)
            elif g < A_HEADS + 2 * A_KV + B_HEADS + B_KV:
                k = _rope(ph, cb, slb, shb, HEAD_DIM // 2)
                ktb_ref[0, g - A_HEADS - 2 * A_KV - B_HEADS] = k.T.astype(BF16)
            else:
                vb_ref[0, g - A_HEADS - 2 * A_KV - B_HEADS - B_KV] = ph.astype(BF16)


def _proj_ab(x, w_p, gq, gk, tabs_a, tabs_b):
    B, S, D = x.shape
    tm = min(PROJ_ROWS, S)
    ns = S // tm
    row = lambda b, i: (b, i, 0)
    tab = pl.BlockSpec((tm, LANES), lambda b, i: (i, 0))
    vec = pl.BlockSpec((1, LANES), lambda b, i: (0, 0))
    hm = lambda h: pl.BlockSpec((1, h, tm, LANES), lambda b, i: (b, 0, i, 0))
    tr = lambda h: pl.BlockSpec((1, h, LANES, tm), lambda b, i: (b, 0, 0, i))
    sd = lambda h: jax.ShapeDtypeStruct((B, h, S, LANES), BF16)
    sdt = lambda h: jax.ShapeDtypeStruct((B, h, LANES, S), BF16)
    return pl.pallas_call(
        _proj_ab_kernel,
        grid=(B, ns),
        in_specs=[pl.BlockSpec((1, tm, D), row),
                  pl.BlockSpec(w_p.shape, lambda b, i: (0, 0)),
                  vec, vec, tab, tab, tab, tab, tab, tab],
        out_specs=[hm(A_HEADS), tr(A_KV), hm(A_KV), hm(B_HEADS), tr(B_KV), hm(B_KV)],
        out_shape=[sd(A_HEADS), sdt(A_KV), sd(A_KV), sd(B_HEADS), sdt(B_KV), sd(B_KV)],
        compiler_params=_cparams(("parallel", "parallel")),
        name="proj_ab",
    )(x, w_p, gq, gk, *tabs_a, *tabs_b)


def _flash_kernel(q_ref, kt_ref, v_ref, o_ref, m_sc, l_sc, acc_sc, *, G, tq, tk, nk):
    M = G * tq
    q = q_ref[0].reshape(M, LANES)
    m_sc[...] = jnp.full((M, 1), -jnp.inf, F32)
    l_sc[...] = jnp.zeros((M, 1), F32)
    acc_sc[...] = jnp.zeros((M, LANES), F32)

    def body(kb, carry):
        off = pl.multiple_of(kb * tk, tk)
        s = jnp.dot(q, kt_ref[0, 0, :, pl.ds(off, tk)], preferred_element_type=F32)
        m_prev = m_sc[...]
        m_new = jnp.maximum(m_prev, jnp.max(s, axis=1, keepdims=True))
        alpha = jnp.exp(m_prev - m_new)
        p = jnp.exp(s - m_new)
        l_sc[...] = alpha * l_sc[...] + jnp.sum(p, axis=1, keepdims=True)
        pv = jnp.dot(p.astype(BF16), v_ref[0, 0, pl.ds(off, tk), :], preferred_element_type=F32)
        acc_sc[...] = acc_sc[...] * alpha + pv
        m_sc[...] = m_new
        return carry

    lax.fori_loop(0, nk, body, 0)
    o = acc_sc[...] / l_sc[...]
    for g in range(G):
        o_ref[0, :, g * LANES:(g + 1) * LANES] = o[g * tq:(g + 1) * tq].astype(BF16)


def _flash(q, kt, v):
    B, H, S, _ = q.shape
    HK = kt.shape[1]
    G = H // HK
    tq = min(ATTN_ROWS // G, S)
    tk = min(ATTN_KEYS, S)
    M = G * tq
    return pl.pallas_call(
        functools.partial(_flash_kernel, G=G, tq=tq, tk=tk, nk=S // tk),
        grid=(B, HK, S // tq),
        in_specs=[pl.BlockSpec((1, G, tq, LANES), lambda b, h, i: (b, h, i, 0)),
                  pl.BlockSpec((1, 1, LANES, S), lambda b, h, i: (b, h, 0, 0)),
                  pl.BlockSpec((1, 1, S, LANES), lambda b, h, i: (b, h, 0, 0))],
        out_specs=pl.BlockSpec((1, tq, G * LANES), lambda b, h, i: (b, i, h)),
        out_shape=jax.ShapeDtypeStruct((B, S, H * LANES), BF16),
        scratch_shapes=[pltpu.VMEM((M, 1), F32), pltpu.VMEM((M, 1), F32), pltpu.VMEM((M, LANES), F32)],
        compiler_params=_cparams(("parallel", "parallel", "parallel")),
        name="flash_attn",
    )(q, kt, v)


def _window_kernel(sink_ref, q_ref, kt_ref, v_ref, o_ref, *, G, tq, span, S):
    M = G * tq
    kvh = pl.program_id(1)
    start = pl.program_id(2) * tq
    kstart = pl.multiple_of(jnp.clip(start - WINDOW, 0, S - span), LANES)
    q = q_ref[0].reshape(M, LANES)
    s = jnp.dot(q, kt_ref[0, 0, :, pl.ds(kstart, span)], preferred_element_type=F32)
    row = lax.broadcasted_iota(jnp.int32, (M, 1), 0)
    qpos = start + row % tq
    kpos = kstart + lax.broadcasted_iota(jnp.int32, (1, span), 1)
    s = jnp.where(jnp.abs(qpos - kpos) <= WINDOW, s, NEG_INF)
    sink = jnp.zeros((M, 1), F32)
    for g in range(G):
        sink = jnp.where(row // tq == g, sink_ref[kvh * G + g], sink)
    m = jnp.maximum(jnp.max(s, axis=1, keepdims=True), sink)
    p = jnp.exp(s - m)
    denom = jnp.sum(p, axis=1, keepdims=True) + jnp.exp(sink - m)
    pv = jnp.dot(p.astype(BF16), v_ref[0, 0, pl.ds(kstart, span), :], preferred_element_type=F32)
    o = pv / denom
    for g in range(G):
        o_ref[0, :, g * LANES:(g + 1) * LANES] = o[g * tq:(g + 1) * tq].astype(BF16)


def _window(q, kt, v, sink):
    B, H, S, _ = q.shape
    HK = kt.shape[1]
    G = H // HK
    tq = min(WIN_Q, S)
    span = min(tq + 2 * WINDOW, S)
    return pl.pallas_call(
        functools.partial(_window_kernel, G=G, tq=tq, span=span, S=S),
        grid_spec=pltpu.PrefetchScalarGridSpec(
            num_scalar_prefetch=1,
            grid=(B, HK, S // tq),
            in_specs=[pl.BlockSpec((1, G, tq, LANES), lambda b, h, i, sk: (b, h, i, 0)),
                      pl.BlockSpec((1, 1, LANES, S), lambda b, h, i, sk: (b, h, 0, 0)),
                      pl.BlockSpec((1, 1, S, LANES), lambda b, h, i, sk: (b, h, 0, 0))],
            out_specs=pl.BlockSpec((1, tq, G * LANES), lambda b, h, i, sk: (b, i, h))),
        out_shape=jax.ShapeDtypeStruct((B, S, H * LANES), BF16),
        compiler_params=_cparams(("parallel", "parallel", "parallel")),
        name="window_attn",
    )(sink, q, kt, v)


def _out_ln_router_kernel(*refs, n_in):
    o_refs = refs[:n_in]
    w_refs = refs[n_in:2 * n_in]
    x_ref, g_ref, b_ref, wrh_ref, wrl_ref, x1_ref, x1b_ref, aff_ref = refs[2 * n_in:]
    h = jnp.dot(o_refs[0][0], w_refs[0][...], preferred_element_type=F32)
    for i in range(1, n_in):
        h = h + jnp.dot(o_refs[i][0], w_refs[i][...], preferred_element_type=F32)
    x1 = _layer_norm(ALPHA * x_ref[0] + h, g_ref[...], b_ref[...])
    x1_ref[0] = x1
    x_hi = x1.astype(BF16)
    x1b_ref[0] = x_hi
    x_lo = (x1 - x_hi.astype(F32)).astype(BF16)
    nt = (((1,), (1,)), ((), ()))
    logits = (lax.dot_general(wrh_ref[...], x_hi, nt, preferred_element_type=F32)
              + lax.dot_general(wrh_ref[...], x_lo, nt, preferred_element_type=F32)
              + lax.dot_general(wrl_ref[...], x_hi, nt, preferred_element_type=F32))
    z = jnp.exp(logits - jnp.max(logits, axis=0, keepdims=True))
    aff_ref[0] = z / jnp.sum(z, axis=0, keepdims=True)


def _out_ln_router(os_, ws_, x, g, b, wr_hi, wr_lo):
    B, S, D = x.shape
    E = wr_hi.shape[0]
    tm = min(PROJ_ROWS, S)
    n_in = len(os_)
    row = lambda bb, i: (bb, i, 0)
    const = lambda bb, i: (0, 0)
    in_specs = ([pl.BlockSpec((1, tm, o.shape[2]), row) for o in os_]
                + [pl.BlockSpec(w.shape, const) for w in ws_]
                + [pl.BlockSpec((1, tm, D), row), pl.BlockSpec((1, D), const), pl.BlockSpec((1, D), const),
                   pl.BlockSpec((E, D), const), pl.BlockSpec((E, D), const)])
    return pl.pallas_call(
        functools.partial(_out_ln_router_kernel, n_in=n_in),
        grid=(B, S // tm),
        in_specs=in_specs,
        out_specs=[pl.BlockSpec((1, tm, D), row), pl.BlockSpec((1, tm, D), row),
                   pl.BlockSpec((1, E, tm), lambda bb, i: (bb, 0, i))],
        out_shape=[jax.ShapeDtypeStruct((B, S, D), F32), jax.ShapeDtypeStruct((B, S, D), BF16),
                   jax.ShapeDtypeStruct((B, E, S), F32)],
        compiler_params=_cparams(("parallel", "parallel")),
        name="out_ln_router",
    )(*os_, *ws_, x, g, b, wr_hi, wr_lo)


def _topk_kernel(aff_ref, tri_ref, rel_ref, cnt_ref, *, S, E, cap, T, CH):
    aff = aff_ref[0]
    bits = pltpu.bitcast(aff, jnp.int32)
    capf = jnp.float32(cap)

    def count(mask):
        return jnp.sum(jnp.where(mask, 1.0, 0.0), axis=1, keepdims=True)

    def thr_body(i, t):
        cand = t | jnp.left_shift(jnp.int32(1), 30 - i)
        return jnp.where(count(bits >= cand) >= capf, cand, t)

    thr = lax.fori_loop(0, 31, thr_body, jnp.zeros((E, 1), jnp.int32))
    gt = bits > thr
    ties = bits == thr
    need = capf - count(gt)
    idx = lax.broadcasted_iota(jnp.int32, (E, S), 1)
    nbits = max(1, (S - 1).bit_length())

    def cut_body(i, c):
        cand = c | jnp.left_shift(jnp.int32(1), nbits - 1 - i)
        return jnp.where(count(ties & (idx < cand)) < need, cand, c)

    cut = lax.fori_loop(0, nbits, cut_body, jnp.zeros((E, 1), jnp.int32))
    sel = gt | (ties & (idx <= cut))

    tri = tri_ref[...]
    nsc = S // T
    lane_sc = lax.broadcasted_iota(jnp.int32, (E, nsc), 1)
    cnt = jnp.zeros((E, nsc), F32)
    for sc in range(nsc):
        run = jnp.zeros((E, 1), F32)
        for ch in range(T // CH):
            lo = sc * T + ch * CH
            selc = jnp.where(sel[:, lo:lo + CH], 1.0, 0.0)
            incl = jnp.dot(selc.astype(BF16), tri, preferred_element_type=F32)
            rel = jnp.where(selc > 0.0, incl - 1.0 + run, -1.0)
            rel_ref[0, :, lo:lo + CH] = rel
            run = run + incl[:, CH - 1:CH]
        cnt = jnp.where(lane_sc == sc, run, cnt)
    cnt_ref[0] = cnt


def _topk(aff, T):
    B, E, S = aff.shape
    cap = EC_FACTOR * S // N_EXPERTS
    CH = min(CUMSUM_CHUNK, T)
    r = lax.broadcasted_iota(jnp.int32, (CH, CH), 0)
    c = lax.broadcasted_iota(jnp.int32, (CH, CH), 1)
    tri = jnp.where(r <= c, 1.0, 0.0).astype(BF16)
    nsc = S // T
    return pl.pallas_call(
        functools.partial(_topk_kernel, S=S, E=E, cap=cap, T=T, CH=CH),
        grid=(B,),
        in_specs=[pl.BlockSpec((1, E, S), lambda b: (b, 0, 0)),
                  pl.BlockSpec((CH, CH), lambda b: (0, 0))],
        out_specs=[pl.BlockSpec((1, E, S), lambda b: (b, 0, 0)),
                   pl.BlockSpec((1, E, nsc), lambda b: (b, 0, 0))],
        out_shape=[jax.ShapeDtypeStruct((B, E, S), F32), jax.ShapeDtypeStruct((B, E, nsc), F32)],
        compiler_params=_cparams(("parallel",)),
        name="topk_select",
    )(aff, tri)


def _moe_kernel(nblk_ref, xb_ref, relt_ref, relc_ref, affc_ref, wg_ref, wu_ref, wd_ref, f_ref, y_sc,
                *, T, RB, E, nsc, TS):
    b, sc, e = pl.program_id(0), pl.program_id(1), pl.program_id(2)

    @pl.when(e == 0)
    def _():
        f_ref[...] = jnp.zeros(f_ref.shape, F32)

    nb = nblk_ref[(b * nsc + sc) * E + e]
    relt = relt_ref[0, 0]

    def ffn_block(j, carry):
        slot = (j * RB + lax.broadcasted_iota(jnp.int32, (RB, 1), 0)).astype(F32)
        onehot = jnp.where(relt == slot, 1.0, 0.0).astype(BF16)
        xg = jnp.dot(onehot, xb_ref[0], preferred_element_type=F32).astype(BF16)
        hg = jnp.dot(xg, wg_ref[0], preferred_element_type=F32)
        hu = jnp.dot(xg, wu_ref[0], preferred_element_type=F32)
        h = (hg / (1.0 + jnp.exp(-hg)) * hu).astype(BF16)
        y = jnp.dot(h, wd_ref[0], preferred_element_type=F32)
        y_sc[pl.ds(pl.multiple_of(j * RB, RB), RB), :] = y.astype(BF16)
        return carry

    lax.fori_loop(0, nb, ffn_block, 0)

    @pl.when(nb % 2 == 1)
    def _():
        y_sc[pl.ds(pl.multiple_of(nb * RB, RB), RB), :] = jnp.zeros((RB, y_sc.shape[1]), BF16)

    lane_e = lax.broadcasted_iota(jnp.int32, (1, E), 1) == e

    def scatter_block(kg, carry):
        koff = pl.multiple_of(kg * 2 * RB, 2 * RB)
        slot = (koff + lax.broadcasted_iota(jnp.int32, (1, 2 * RB), 1)).astype(F32)
        yk = y_sc[pl.ds(koff, 2 * RB), :]
        for ts in range(T // TS):
            rows = slice(ts * TS, (ts + 1) * TS)
            relc = jnp.sum(jnp.where(lane_e, relc_ref[0, rows, :], 0.0), axis=1, keepdims=True)
            gate = jnp.sum(jnp.where(lane_e, affc_ref[0, rows, :], 0.0), axis=1, keepdims=True)
            onehot_t = jnp.where(relc == slot, 1.0, 0.0).astype(BF16)
            f_ref[0, rows, :] += gate * jnp.dot(onehot_t, yk, preferred_element_type=F32)
        return carry

    lax.fori_loop(0, (nb + 1) // 2, scatter_block, 0)


def _moe(xb, relt, relc, affc, nblk, wg, wu, wd, T):
    B, S, D = xb.shape
    E, _, F = wg.shape
    nsc = S // T
    RB = MOE_SLOTS
    cap = EC_FACTOR * S // N_EXPERTS
    max_blocks = -(-min(cap, T) // RB)
    y_rows = (max_blocks + 1) * RB
    TS = min(512, T)
    return pl.pallas_call(
        functools.partial(_moe_kernel, T=T, RB=RB, E=E, nsc=nsc, TS=TS),
        grid_spec=pltpu.PrefetchScalarGridSpec(
            num_scalar_prefetch=1,
            grid=(B, nsc, E),
            in_specs=[pl.BlockSpec((1, T, D), lambda b, s, e, n: (b, s, 0)),
                      pl.BlockSpec((1, 1, 1, T), lambda b, s, e, n: (b, e, 0, s)),
                      pl.BlockSpec((1, T, E), lambda b, s, e, n: (b, s, 0)),
                      pl.BlockSpec((1, T, E), lambda b, s, e, n: (b, s, 0)),
                      pl.BlockSpec((1, D, F), lambda b, s, e, n: (e, 0, 0)),
                      pl.BlockSpec((1, D, F), lambda b, s, e, n: (e, 0, 0)),
                      pl.BlockSpec((1, F, D), lambda b, s, e, n: (e, 0, 0))],
            out_specs=pl.BlockSpec((1, T, D), lambda b, s, e, n: (b, s, 0)),
            scratch_shapes=[pltpu.VMEM((y_rows, D), BF16)]),
        out_shape=jax.ShapeDtypeStruct((B, S, D), F32),
        compiler_params=_cparams(("parallel", "parallel", "arbitrary")),
        name="moe_ffn",
    )(nblk, xb, relt, relc, affc, wg, wu, wd)


def _add_ln_kernel(x_ref, f_ref, g_ref, b_ref, o_ref):
    o_ref[0] = _layer_norm(ALPHA * x_ref[0] + f_ref[0], g_ref[...], b_ref[...])


def _add_ln(x, f, g, b):
    B, S, D = x.shape
    tm = min(PROJ_ROWS, S)
    row = lambda bb, i: (bb, i, 0)
    const = lambda bb, i: (0, 0)
    return pl.pallas_call(
        _add_ln_kernel,
        grid=(B, S // tm),
        in_specs=[pl.BlockSpec((1, tm, D), row), pl.BlockSpec((1, tm, D), row),
                  pl.BlockSpec((1, D), const), pl.BlockSpec((1, D), const)],
        out_specs=pl.BlockSpec((1, tm, D), row),
        out_shape=jax.ShapeDtypeStruct((B, S, D), F32),
        compiler_params=_cparams(("parallel", "parallel")),
        name="add_ln",
    )(x, f, g, b)


def _proj_mla_kernel(x_ref, wd_ref, gq_ref, gkv_ref, wq_ref, wk_ref, wv_ref,
                     cq_ref, slq_ref, shq_ref, ck_ref, slk_ref, shk_ref, q_ref, kt_ref, v_ref):
    xb = x_ref[0].astype(BF16)
    d = jnp.dot(xb, wd_ref[...], preferred_element_type=F32)
    cq = _rms(d[:, :Q_LORA], gq_ref[...], Q_LORA).astype(BF16)
    ckv = _rms(d[:, Q_LORA:Q_LORA + KV_LORA], gkv_ref[...], KV_LORA).astype(BF16)
    k_rope = _rope(d[:, Q_LORA + KV_LORA:], ck_ref[...], slk_ref[...], shk_ref[...], QK_ROPE // 2)
    cqt, slq, shq = cq_ref[...], slq_ref[...], shq_ref[...]
    for c in range(MLA_HEADS // 2):
        cols = slice(c * 2 * LANES, (c + 1) * 2 * LANES)
        q2 = jnp.dot(cq, wq_ref[:, cols], preferred_element_type=F32)
        k2 = jnp.dot(ckv, wk_ref[:, cols], preferred_element_type=F32)
        v2 = jnp.dot(ckv, wv_ref[:, cols], preferred_element_type=F32)
        for half in range(2):
            h = 2 * c + half
            lanes = slice(half * LANES, (half + 1) * LANES)
            q_ref[0, h] = _rope(q2[:, lanes], cqt, slq, shq, QK_ROPE // 2).astype(BF16)
            kt_ref[0, h] = (k2[:, lanes] + k_rope).T.astype(BF16)
            v_ref[0, h] = v2[:, lanes].astype(BF16)


def _proj_mla(x, wd_p, gq, gkv, wq_p, wk_p, wv_p, tabs_q, tabs_k):
    B, S, D = x.shape
    tm = min(PROJ_ROWS, S)
    H = MLA_HEADS
    const = lambda b, i: (0, 0)
    tab = pl.BlockSpec((tm, LANES), lambda b, i: (i, 0))
    full = lambda a: pl.BlockSpec(a.shape, const)
    return pl.pallas_call(
        _proj_mla_kernel,
        grid=(B, S // tm),
        in_specs=[pl.BlockSpec((1, tm, D), lambda b, i: (b, i, 0)),
                  full(wd_p), full(gq), full(gkv), full(wq_p), full(wk_p), full(wv_p),
                  tab, tab, tab, tab, tab, tab],
        out_specs=[pl.BlockSpec((1, H, tm, LANES), lambda b, i: (b, 0, i, 0)),
                   pl.BlockSpec((1, H, LANES, tm), lambda b, i: (b, 0, 0, i)),
                   pl.BlockSpec((1, H, tm, LANES), lambda b, i: (b, 0, i, 0))],
        out_shape=[jax.ShapeDtypeStruct((B, H, S, LANES), BF16),
                   jax.ShapeDtypeStruct((B, H, LANES, S), BF16),
                   jax.ShapeDtypeStruct((B, H, S, LANES), BF16)],
        compiler_params=_cparams(("parallel", "parallel")),
        name="proj_mla",
    )(x, wd_p, gq, gkv, wq_p, wk_p, wv_p, *tabs_q, *tabs_k)


def _pad_lanes(a, width=LANES):
    return jnp.pad(a, [(0, 0)] * (a.ndim - 1) + [(0, width - a.shape[-1])])


def _head_cols(w, n_heads, dim):
    return _pad_lanes(w.reshape(w.shape[0], n_heads, dim)).reshape(w.shape[0], n_heads * LANES)


def _head_rows(w, n_heads, dim):
    n = w.shape[1]
    return jnp.pad(w.reshape(n_heads, dim, n), ((0, 0), (0, LANES - dim), (0, 0))).reshape(n_heads * LANES, n)


def _angles(pos, dim):
    freqs = ROPE_THETA ** (-(jnp.arange(0, dim, 2, dtype=F32) / dim))
    return pos[:, None] * freqs[None, :]


def _place(S, pieces):
    out = jnp.zeros((S, LANES), F32)
    for off, val in pieces:
        out = out.at[:, off:off + val.shape[1]].set(val)
    return out


def _tables(S):
    t = jnp.arange(S)
    ar = _angles((t // GRID_W).astype(F32), HEAD_DIM // 2)
    ac = _angles((t % GRID_W).astype(F32), HEAD_DIM // 2)
    q4 = HEAD_DIM // 4
    tabs_a = (_place(S, [(0, jnp.cos(ar)), (q4, jnp.cos(ar)), (2 * q4, jnp.cos(ac)), (3 * q4, jnp.cos(ac))]),
              _place(S, [(0, -jnp.sin(ar)), (2 * q4, -jnp.sin(ac))]),
              _place(S, [(q4, jnp.sin(ar)), (3 * q4, jnp.sin(ac))]))
    asq = _angles(t.astype(F32), HEAD_DIM)
    h2 = HEAD_DIM // 2
    tabs_b = (_place(S, [(0, jnp.cos(asq)), (h2, jnp.cos(asq))]),
              _place(S, [(0, -jnp.sin(asq))]),
              _place(S, [(h2, jnp.sin(asq))]))
    am = _angles(t.astype(F32), QK_ROPE)
    r2 = QK_ROPE // 2
    ones = jnp.ones((S, QK_NOPE), F32)
    tabs_mk = (_place(S, [(0, ones), (QK_NOPE, jnp.cos(am)), (QK_NOPE + r2, jnp.cos(am))]),
               _place(S, [(QK_NOPE, -jnp.sin(am))]),
               _place(S, [(QK_NOPE + r2, jnp.sin(am))]))
    scale = (QK_NOPE + QK_ROPE) ** -0.5
    tabs_mq = tuple(tb * scale for tb in tabs_mk)
    return tabs_a, tabs_b, tabs_mq, tabs_mk


def _split_bf16(w):
    hi = w.astype(BF16)
    return hi, (w - hi.astype(F32)).astype(BF16)


def _moe_layer(x1, x1b, aff, w_gate, w_up, w_down, g, b):
    B, S, D = x1.shape
    T = min(MOE_TOKENS, S)
    rel, cnt = _topk(aff, T)
    nblk = ((cnt.astype(jnp.int32) + MOE_SLOTS - 1) // MOE_SLOTS).transpose(0, 2, 1).reshape(-1)
    relt = rel.reshape(B, N_EXPERTS, 1, S)
    relc = rel.transpose(0, 2, 1)
    affc = aff.transpose(0, 2, 1)
    f = _moe(x1b, relt, relc, affc, nblk, w_gate.astype(BF16), w_up.astype(BF16), w_down.astype(BF16), T)
    return _add_ln(x1, f, g, b)


def kernel(x, ab_w_in, ab_q_norm, ab_k_norm, ab_sink, ab_w_out, mla_w_down, mla_q_norm, mla_kv_norm,
           mla_w_uq, mla_w_ukv, mla_w_out, ln_mix_g, ln_mix_b, moe_router, moe_w_gate, moe_w_up, moe_w_down,
           ln_ffn_g, ln_ffn_b):
    B, S, D = x.shape
    tabs_a, tabs_b, tabs_mq, tabs_mk = _tables(S)
    row = lambda v: v.reshape(1, -1)

    n_ab = A_HEADS + 2 * A_KV + B_HEADS + 2 * B_KV
    w_in_p = _head_cols(ab_w_in[0], n_ab, HEAD_DIM).astype(BF16)
    qa, kta, va, qb, ktb, vb = _proj_ab(x, w_in_p, _pad_lanes(row(ab_q_norm[0])), _pad_lanes(row(ab_k_norm[0])),
                                        tabs_a, tabs_b)
    oa = _flash(qa, kta, va)
    ob = _window(qb, ktb, vb, ab_sink[0])
    n_a = A_HEADS * HEAD_DIM
    w_oa = _head_rows(ab_w_out[0][:n_a], A_HEADS, HEAD_DIM).astype(BF16)
    w_ob = _head_rows(ab_w_out[0][n_a:], B_HEADS, HEAD_DIM).astype(BF16)
    wr_hi, wr_lo = _split_bf16(moe_router[0].T)
    x1, x1b, aff = _out_ln_router([oa, ob], [w_oa, w_ob], x, row(ln_mix_g[0]), row(ln_mix_b[0]), wr_hi, wr_lo)
    x = _moe_layer(x1, x1b, aff, moe_w_gate[0], moe_w_up[0], moe_w_down[0], row(ln_ffn_g[0]), row(ln_ffn_b[0]))

    wd = mla_w_down[0]
    wd_p = jnp.concatenate(
        [wd[:, :Q_LORA + KV_LORA],
         jnp.zeros((D, QK_NOPE), F32), wd[:, Q_LORA + KV_LORA:], jnp.zeros((D, LANES - QK_NOPE - QK_ROPE), F32)],
        axis=1).astype(BF16)
    wq_p = _head_cols(mla_w_uq[0], MLA_HEADS, QK_NOPE + QK_ROPE).astype(BF16)
    wkv = mla_w_ukv[0].reshape(KV_LORA, MLA_HEADS, QK_NOPE + V_DIM)
    wk_p = _head_cols(wkv[:, :, :QK_NOPE].reshape(KV_LORA, -1), MLA_HEADS, QK_NOPE).astype(BF16)
    wv_p = _head_cols(wkv[:, :, QK_NOPE:].reshape(KV_LORA, -1), MLA_HEADS, V_DIM).astype(BF16)
    q, kt, v = _proj_mla(x, wd_p, row(mla_q_norm[0]), row(mla_kv_norm[0]), wq_p, wk_p, wv_p, tabs_mq, tabs_mk)
    oc = _flash(q, kt, v)
    w_oc = _head_rows(mla_w_out[0], MLA_HEADS, V_DIM).astype(BF16)
    wr_hi, wr_lo = _split_bf16(moe_router[1].T)
    x1, x1b, aff = _out_ln_router([oc], [w_oc], x, row(ln_mix_g[1]), row(ln_mix_b[1]), wr_hi, wr_lo)
    x = _moe_layer(x1, x1b, aff, moe_w_gate[1], moe_w_up[1], moe_w_down[1], row(ln_ffn_g[1]), row(ln_ffn_b[1]))
    return x
```

```python
import functools

import jax
import jax.numpy as jnp
from jax import lax
from jax.experimental import pallas as pl
from jax.experimental.pallas import tpu as pltpu

F32 = jnp.float32
BF16 = jnp.bfloat16

GRID_W = 64
ROPE_THETA = 10000.0
HEAD_DIM = 64
A_HEADS, A_KV = 8, 2
B_HEADS, B_KV = 8, 2
WINDOW = 128
MLA_HEADS = 16
Q_LORA, KV_LORA = 256, 128
QK_NOPE, QK_ROPE, V_DIM = 64, 32, 64
N_EXPERTS = 16
EC_FACTOR = 2
DEPTH = 2
ALPHA = (2.0 * DEPTH) ** 0.25
NEG_INF = -1e30
RMS_EPS = 1e-6
LN_EPS = 1e-5

LANES = 128
VMEM_LIMIT = 56 * 1024 * 1024

PROJ_ROWS = 512
ATTN_ROWS = 2048
ATTN_KEYS = 512
ATTN_COLS = 256
WIN_Q = 256
MOE_TOKENS = 2048
MOE_SLOTS = 128
CUMSUM_CHUNK = 256


def _cparams(sem):
    return pltpu.CompilerParams(dimension_semantics=sem, vmem_limit_bytes=VMEM_LIMIT)


def _rope(x, c, s_lo, s_hi, shift):
    return x * c + pltpu.roll(x, LANES - shift, 1) * s_lo + pltpu.roll(x, shift, 1) * s_hi


def _rms(x, g, n):
    ms = jnp.sum(x * x, axis=1, keepdims=True) * (1.0 / n)
    return x * lax.rsqrt(ms + RMS_EPS) * g


def _layer_norm(y, g, b):
    mu = jnp.mean(y, axis=1, keepdims=True)
    d = y - mu
    var = jnp.mean(d * d, axis=1, keepdims=True)
    return d * lax.rsqrt(var + LN_EPS) * g + b


def _with_ones_lane(v, n):
    lane = lax.broadcasted_iota(jnp.int32, (1, LANES), 1)
    return jnp.where(lane == n, 1.0, v)


def _proj_ab_kernel(x_ref, w_ref, gq_ref, gk_ref, ca_ref, sla_ref, sha_ref, cb_ref, slb_ref, shb_ref,
                    qa_ref, ka_ref, va_ref, qb_ref, kb_ref, vb_ref):
    xb = x_ref[0].astype(BF16)
    scale = HEAD_DIM ** -0.5
    ca, sla, sha = ca_ref[...], sla_ref[...], sha_ref[...]
    cb, slb, shb = cb_ref[...], slb_ref[...], shb_ref[...]
    n_groups = A_HEADS + 2 * A_KV + B_HEADS + 2 * B_KV
    for c in range(n_groups // 2):
        pr = jnp.dot(xb, w_ref[:, c * 2 * LANES:(c + 1) * 2 * LANES], preferred_element_type=F32)
        for half in range(2):
            g = 2 * c + half
            ph = pr[:, half * LANES:(half + 1) * LANES]
            if g < A_HEADS:
                q = _rope(_rms(ph, gq_ref[...], HEAD_DIM), ca, sla, sha, HEAD_DIM // 4) * scale
                qa_ref[0, g] = q.T.astype(BF16)
            elif g < A_HEADS + A_KV:
                k = _rope(_rms(ph, gk_ref[...], HEAD_DIM), ca, sla, sha, HEAD_DIM // 4)
                ka_ref[0, g - A_HEADS] = k.astype(BF16)
            elif g < A_HEADS + 2 * A_KV:
                va_ref[0, g - A_HEADS - A_KV] = _with_ones_lane(ph, HEAD_DIM).T.astype(BF16)
            elif g < A_HEADS + 2 * A_KV + B_HEADS:
                q = _rope(ph, cb, slb, shb, HEAD_DIM // 2) * scale
                qb_ref[0, g - A_HEADS - 2 * A_KV] = q.T.astype(BF16)
            elif g < A_HEADS + 2 * A_KV + B_HEADS + B_KV:
                k = _rope(ph, cb, slb, shb, HEAD_DIM // 2)
                kb_ref[0, g - A_HEADS - 2 * A_KV - B_HEADS] = k.astype(BF16)
            else:
                vb_ref[0, g - A_HEADS - 2 * A_KV - B_HEADS - B_KV] = _with_ones_lane(ph, HEAD_DIM).T.astype(BF16)


def _proj_ab(x, w_p, gq, gk, tabs_a, tabs_b):
    B, S, D = x.shape
    tm = min(PROJ_ROWS, S)
    ns = S // tm
    row = lambda b, i: (b, i, 0)
    tab = pl.BlockSpec((tm, LANES), lambda b, i: (i, 0))
    vec = pl.BlockSpec((1, LANES), lambda b, i: (0, 0))
    hm = lambda h: pl.BlockSpec((1, h, tm, LANES), lambda b, i: (b, 0, i, 0))
    tr = lambda h: pl.BlockSpec((1, h, LANES, tm), lambda b, i: (b, 0, 0, i))
    sd = lambda h: jax.ShapeDtypeStruct((B, h, S, LANES), BF16)
    sdt = lambda h: jax.ShapeDtypeStruct((B, h, LANES, S), BF16)
    return pl.pallas_call(
        _proj_ab_kernel,
        grid=(B, ns),
        in_specs=[pl.BlockSpec((1, tm, D), row),
                  pl.BlockSpec(w_p.shape, lambda b, i: (0, 0)),
                  vec, vec, tab, tab, tab, tab, tab, tab],
        out_specs=[tr(A_HEADS), hm(A_KV), tr(A_KV), tr(B_HEADS), hm(B_KV), tr(B_KV)],
        out_shape=[sdt(A_HEADS), sd(A_KV), sdt(A_KV), sdt(B_HEADS), sd(B_KV), sdt(B_KV)],
        compiler_params=_cparams(("parallel", "parallel")),
        name="proj_ab",
    )(x, w_p, gq, gk, *tabs_a, *tabs_b)


def _flash_kernel(qt_ref, k_ref, vt_ref, o_ref, m_sc, acc_sc, *, G, tq, tk, nk, cw, sum_row):
    M = G * tq
    m_sc[...] = jnp.full((1, M), -jnp.inf, F32)
    acc_sc[...] = jnp.zeros((LANES, M), F32)

    def body(kb, carry):
        off = pl.multiple_of(kb * tk, tk)
        k = k_ref[0, 0, pl.ds(off, tk), :]
        vt = vt_ref[0, 0, :, pl.ds(off, tk)]

        def scores(c):
            g, j = divmod(c * cw, tq)
            return jnp.dot(k, qt_ref[0, g, :, j:j + cw], preferred_element_type=F32)

        def rescale(cols, alpha, pv):
            acc_sc[:, cols] = acc_sc[:, cols] * alpha + pv

        n_sub = M // cw
        st = {c: scores(c) for c in range(min(2, n_sub))}
        pending = None
        for c in range(n_sub):
            cols = slice(c * cw, (c + 1) * cw)
            m_prev = m_sc[:, cols]
            m_new = jnp.maximum(m_prev, jnp.max(st[c], axis=0, keepdims=True))
            alpha = jnp.exp(m_prev - m_new)
            pt = jnp.exp(st.pop(c) - m_new).astype(BF16)
            m_sc[:, cols] = m_new
            pv = jnp.dot(vt, pt, preferred_element_type=F32)
            if pending is not None:
                rescale(*pending)
            pending = (cols, alpha, pv)
            if c + 2 < n_sub:
                st[c + 2] = scores(c + 2)
        rescale(*pending)
        return carry

    lax.fori_loop(0, nk, body, 0)
    acc = acc_sc[...]
    o = acc / acc[sum_row:sum_row + 1, :]
    for g in range(G):
        o_ref[0, :, g * LANES:(g + 1) * LANES] = o[:, g * tq:(g + 1) * tq].T.astype(BF16)


def _flash(qt, k, vt, sum_row):
    B, H, _, S = qt.shape
    HK = k.shape[1]
    G = H // HK
    tq = min(ATTN_ROWS // G, S)
    tk = min(ATTN_KEYS, S)
    M = G * tq
    cw = min(ATTN_COLS, tq)
    return pl.pallas_call(
        functools.partial(_flash_kernel, G=G, tq=tq, tk=tk, nk=S // tk, cw=cw, sum_row=sum_row),
        grid=(B, HK, S // tq),
        in_specs=[pl.BlockSpec((1, G, LANES, tq), lambda b, h, i: (b, h, 0, i)),
                  pl.BlockSpec((1, 1, S, LANES), lambda b, h, i: (b, h, 0, 0)),
                  pl.BlockSpec((1, 1, LANES, S), lambda b, h, i: (b, h, 0, 0))],
        out_specs=pl.BlockSpec((1, tq, G * LANES), lambda b, h, i: (b, i, h)),
        out_shape=jax.ShapeDtypeStruct((B, S, H * LANES), BF16),
        scratch_shapes=[pltpu.VMEM((1, M), F32), pltpu.VMEM((LANES, M), F32)],
        compiler_params=_cparams(("parallel", "parallel", "parallel")),
        name="flash_attn",
    )(qt, k, vt)


def _window_kernel(sink_ref, qt_ref, k_ref, vt_ref, o_ref, *, G, tq, span, S, sum_row):
    kvh = pl.program_id(1)
    start = pl.program_id(2) * tq
    kstart = pl.multiple_of(jnp.clip(start - WINDOW, 0, S - span), LANES)
    k = k_ref[0, 0, pl.ds(kstart, span), :]
    vt = vt_ref[0, 0, :, pl.ds(kstart, span)]
    kpos = kstart + lax.broadcasted_iota(jnp.int32, (span, 1), 0)
    qpos = start + lax.broadcasted_iota(jnp.int32, (1, tq), 1)
    valid = jnp.abs(qpos - kpos) <= WINDOW
    for g in range(G):
        st = jnp.dot(k, qt_ref[0, g], preferred_element_type=F32)
        st = jnp.where(valid, st, NEG_INF)
        sink = sink_ref[kvh * G + g]
        m = jnp.maximum(jnp.max(st, axis=0, keepdims=True), sink)
        pt = jnp.exp(st - m).astype(BF16)
        acc = jnp.dot(vt, pt, preferred_element_type=F32)
        o = acc / (acc[sum_row:sum_row + 1, :] + jnp.exp(sink - m))
        o_ref[0, :, g * LANES:(g + 1) * LANES] = o.T.astype(BF16)


def _window(qt, k, vt, sink, sum_row):
    B, H, _, S = qt.shape
    HK = k.shape[1]
    G = H // HK
    tq = min(WIN_Q, S)
    span = min(tq + 2 * WINDOW, S)
    return pl.pallas_call(
        functools.partial(_window_kernel, G=G, tq=tq, span=span, S=S, sum_row=sum_row),
        grid_spec=pltpu.PrefetchScalarGridSpec(
            num_scalar_prefetch=1,
            grid=(B, HK, S // tq),
            in_specs=[pl.BlockSpec((1, G, LANES, tq), lambda b, h, i, sk: (b, h, 0, i)),
                      pl.BlockSpec((1, 1, S, LANES), lambda b, h, i, sk: (b, h, 0, 0)),
                      pl.BlockSpec((1, 1, LANES, S), lambda b, h, i, sk: (b, h, 0, 0))],
            out_specs=pl.BlockSpec((1, tq, G * LANES), lambda b, h, i, sk: (b, i, h))),
        out_shape=jax.ShapeDtypeStruct((B, S, H * LANES), BF16),
        compiler_params=_cparams(("parallel", "parallel", "parallel")),
        name="window_attn",
    )(sink, qt, k, vt)


def _out_ln_router_kernel(*refs, n_in):
    o_refs = refs[:n_in]
    w_refs = refs[n_in:2 * n_in]
    x_ref, g_ref, b_ref, wrh_ref, wrl_ref, x1_ref, x1b_ref, aff_ref = refs[2 * n_in:]
    h = jnp.dot(o_refs[0][0], w_refs[0][...], preferred_element_type=F32)
    for i in range(1, n_in):
        h = h + jnp.dot(o_refs[i][0], w_refs[i][...], preferred_element_type=F32)
    x1 = _layer_norm(ALPHA * x_ref[0] + h, g_ref[...], b_ref[...])
    x1_ref[0] = x1
    x_hi = x1.astype(BF16)
    x1b_ref[0] = x_hi
    x_lo = (x1 - x_hi.astype(F32)).astype(BF16)
    nt = (((1,), (1,)), ((), ()))
    logits = (lax.dot_general(wrh_ref[...], x_hi, nt, preferred_element_type=F32)
              + lax.dot_general(wrh_ref[...], x_lo, nt, preferred_element_type=F32)
              + lax.dot_general(wrl_ref[...], x_hi, nt, preferred_element_type=F32))
    z = jnp.exp(logits - jnp.max(logits, axis=0, keepdims=True))
    aff_ref[0] = z / jnp.sum(z, axis=0, keepdims=True)


def _out_ln_router(os_, ws_, x, g, b, wr_hi, wr_lo):
    B, S, D = x.shape
    E = wr_hi.shape[0]
    tm = min(PROJ_ROWS, S)
    n_in = len(os_)
    row = lambda bb, i: (bb, i, 0)
    const = lambda bb, i: (0, 0)
    in_specs = ([pl.BlockSpec((1, tm, o.shape[2]), row) for o in os_]
                + [pl.BlockSpec(w.shape, const) for w in ws_]
                + [pl.BlockSpec((1, tm, D), row), pl.BlockSpec((1, D), const), pl.BlockSpec((1, D), const),
                   pl.BlockSpec((E, D), const), pl.BlockSpec((E, D), const)])
    return pl.pallas_call(
        functools.partial(_out_ln_router_kernel, n_in=n_in),
        grid=(B, S // tm),
        in_specs=in_specs,
        out_specs=[pl.BlockSpec((1, tm, D), row), pl.BlockSpec((1, tm, D), row),
                   pl.BlockSpec((1, E, tm), lambda bb, i: (bb, 0, i))],
        out_shape=[jax.ShapeDtypeStruct((B, S, D), F32), jax.ShapeDtypeStruct((B, S, D), BF16),
                   jax.ShapeDtypeStruct((B, E, S), F32)],
        compiler_params=_cparams(("parallel", "parallel")),
        name="out_ln_router",
    )(*os_, *ws_, x, g, b, wr_hi, wr_lo)


def _topk_kernel(aff_ref, tri_ref, rel_ref, cnt_ref, *, S, E, cap, T, CH):
    aff = aff_ref[0]
    bits = pltpu.bitcast(aff, jnp.int32)
    capf = jnp.float32(cap)

    def count(mask):
        return jnp.sum(jnp.where(mask, 1.0, 0.0), axis=1, keepdims=True)

    def thr_body(i, t):
        cand = t | jnp.left_shift(jnp.int32(1), 30 - i)
        return jnp.where(count(bits >= cand) >= capf, cand, t)

    thr = lax.fori_loop(0, 31, thr_body, jnp.zeros((E, 1), jnp.int32))
    gt = bits > thr
    ties = bits == thr
    need = capf - count(gt)
    idx = lax.broadcasted_iota(jnp.int32, (E, S), 1)
    nbits = max(1, (S - 1).bit_length())

    def cut_body(i, c):
        cand = c | jnp.left_shift(jnp.int32(1), nbits - 1 - i)
        return jnp.where(count(ties & (idx < cand)) < need, cand, c)

    cut = lax.fori_loop(0, nbits, cut_body, jnp.zeros((E, 1), jnp.int32))
    sel = gt | (ties & (idx <= cut))

    tri = tri_ref[...]
    nsc = S // T
    lane_sc = lax.broadcasted_iota(jnp.int32, (E, nsc), 1)
    cnt = jnp.zeros((E, nsc), F32)
    for sc in range(nsc):
        run = jnp.zeros((E, 1), F32)
        for ch in range(T // CH):
            lo = sc * T + ch * CH
            selc = jnp.where(sel[:, lo:lo + CH], 1.0, 0.0)
            incl = jnp.dot(selc.astype(BF16), tri, preferred_element_type=F32)
            rel = jnp.where(selc > 0.0, incl - 1.0 + run, -1.0)
            rel_ref[0, :, lo:lo + CH] = rel
            run = run + incl[:, CH - 1:CH]
        cnt = jnp.where(lane_sc == sc, run, cnt)
    cnt_ref[0] = cnt


def _topk(aff, T):
    B, E, S = aff.shape
    cap = EC_FACTOR * S // N_EXPERTS
    CH = min(CUMSUM_CHUNK, T)
    r = lax.broadcasted_iota(jnp.int32, (CH, CH), 0)
    c = lax.broadcasted_iota(jnp.int32, (CH, CH), 1)
    tri = jnp.where(r <= c, 1.0, 0.0).astype(BF16)
    nsc = S // T
    return pl.pallas_call(
        functools.partial(_topk_kernel, S=S, E=E, cap=cap, T=T, CH=CH),
        grid=(B,),
        in_specs=[pl.BlockSpec((1, E, S), lambda b: (b, 0, 0)),
                  pl.BlockSpec((CH, CH), lambda b: (0, 0))],
        out_specs=[pl.BlockSpec((1, E, S), lambda b: (b, 0, 0)),
                   pl.BlockSpec((1, E, nsc), lambda b: (b, 0, 0))],
        out_shape=[jax.ShapeDtypeStruct((B, E, S), F32), jax.ShapeDtypeStruct((B, E, nsc), F32)],
        compiler_params=_cparams(("parallel",)),
        name="topk_select",
    )(aff, tri)


def _moe_kernel(nblk_ref, xb_ref, relt_ref, relc_ref, affc_ref, wg_ref, wu_ref, wd_ref, f_ref, y_sc,
                *, T, RB, E, nsc, TS):
    b, sc, e = pl.program_id(0), pl.program_id(1), pl.program_id(2)

    @pl.when(e == 0)
    def _():
        f_ref[...] = jnp.zeros(f_ref.shape, F32)

    nb = nblk_ref[(b * nsc + sc) * E + e]
    relt = relt_ref[0, 0]

    def ffn_block(j, carry):
        slot = (j * RB + lax.broadcasted_iota(jnp.int32, (RB, 1), 0)).astype(F32)
        onehot = jnp.where(relt == slot, 1.0, 0.0).astype(BF16)
        xg = jnp.dot(onehot, xb_ref[0], preferred_element_type=F32).astype(BF16)
        hg = jnp.dot(xg, wg_ref[0], preferred_element_type=F32)
        hu = jnp.dot(xg, wu_ref[0], preferred_element_type=F32)
        h = (hg / (1.0 + jnp.exp(-hg)) * hu).astype(BF16)
        y = jnp.dot(h, wd_ref[0], preferred_element_type=F32)
        y_sc[pl.ds(pl.multiple_of(j * RB, RB), RB), :] = y.astype(BF16)
        return carry

    lax.fori_loop(0, nb, ffn_block, 0)

    @pl.when(nb % 2 == 1)
    def _():
        y_sc[pl.ds(pl.multiple_of(nb * RB, RB), RB), :] = jnp.zeros((RB, y_sc.shape[1]), BF16)

    lane_e = lax.broadcasted_iota(jnp.int32, (1, E), 1) == e

    def scatter_block(kg, carry):
        koff = pl.multiple_of(kg * 2 * RB, 2 * RB)
        slot = (koff + lax.broadcasted_iota(jnp.int32, (1, 2 * RB), 1)).astype(F32)
        yk = y_sc[pl.ds(koff, 2 * RB), :]
        for ts in range(T // TS):
            rows = slice(ts * TS, (ts + 1) * TS)
            relc = jnp.sum(jnp.where(lane_e, relc_ref[0, rows, :], 0.0), axis=1, keepdims=True)
            gate = jnp.sum(jnp.where(lane_e, affc_ref[0, rows, :], 0.0), axis=1, keepdims=True)
            onehot_t = jnp.where(relc == slot, 1.0, 0.0).astype(BF16)
            f_ref[0, rows, :] += gate * jnp.dot(onehot_t, yk, preferred_element_type=F32)
        return carry

    lax.fori_loop(0, (nb + 1) // 2, scatter_block, 0)


def _moe(xb, relt, relc, affc, nblk, wg, wu, wd, T):
    B, S, D = xb.shape
    E, _, F = wg.shape
    nsc = S // T
    RB = MOE_SLOTS
    cap = EC_FACTOR * S // N_EXPERTS
    max_blocks = -(-min(cap, T) // RB)
    y_rows = (max_blocks + 1) * RB
    TS = min(512, T)
    return pl.pallas_call(
        functools.partial(_moe_kernel, T=T, RB=RB, E=E, nsc=nsc, TS=TS),
        grid_spec=pltpu.PrefetchScalarGridSpec(
            num_scalar_prefetch=1,
            grid=(B, nsc, E),
            in_specs=[pl.BlockSpec((1, T, D), lambda b, s, e, n: (b, s, 0)),
                      pl.BlockSpec((1, 1, 1, T), lambda b, s, e, n: (b, e, 0, s)),
                      pl.BlockSpec((1, T, E), lambda b, s, e, n: (b, s, 0)),
                      pl.BlockSpec((1, T, E), lambda b, s, e, n: (b, s, 0)),
                      pl.BlockSpec((1, D, F), lambda b, s, e, n: (e, 0, 0)),
                      pl.BlockSpec((1, D, F), lambda b, s, e, n: (e, 0, 0)),
                      pl.BlockSpec((1, F, D), lambda b, s, e, n: (e, 0, 0))],
            out_specs=pl.BlockSpec((1, T, D), lambda b, s, e, n: (b, s, 0)),
            scratch_shapes=[pltpu.VMEM((y_rows, D), BF16)]),
        out_shape=jax.ShapeDtypeStruct((B, S, D), F32),
        compiler_params=_cparams(("parallel", "parallel", "arbitrary")),
        name="moe_ffn",
    )(nblk, xb, relt, relc, affc, wg, wu, wd)


def _add_ln_kernel(x_ref, f_ref, g_ref, b_ref, o_ref):
    o_ref[0] = _layer_norm(ALPHA * x_ref[0] + f_ref[0], g_ref[...], b_ref[...])


def _add_ln(x, f, g, b):
    B, S, D = x.shape
    tm = min(PROJ_ROWS, S)
    row = lambda bb, i: (bb, i, 0)
    const = lambda bb, i: (0, 0)
    return pl.pallas_call(
        _add_ln_kernel,
        grid=(B, S // tm),
        in_specs=[pl.BlockSpec((1, tm, D), row), pl.BlockSpec((1, tm, D), row),
                  pl.BlockSpec((1, D), const), pl.BlockSpec((1, D), const)],
        out_specs=pl.BlockSpec((1, tm, D), row),
        out_shape=jax.ShapeDtypeStruct((B, S, D), F32),
        compiler_params=_cparams(("parallel", "parallel")),
        name="add_ln",
    )(x, f, g, b)


def _proj_mla_kernel(x_ref, wd_ref, gq_ref, gkv_ref, wq_ref, wk_ref, wv_ref,
                     cq_ref, slq_ref, shq_ref, ck_ref, slk_ref, shk_ref, qt_ref, k_ref, vt_ref):
    xb = x_ref[0].astype(BF16)
    d = jnp.dot(xb, wd_ref[...], preferred_element_type=F32)
    cq = _rms(d[:, :Q_LORA], gq_ref[...], Q_LORA).astype(BF16)
    ckv = _rms(d[:, Q_LORA:Q_LORA + KV_LORA], gkv_ref[...], KV_LORA).astype(BF16)
    k_rope = _rope(d[:, Q_LORA + KV_LORA:], ck_ref[...], slk_ref[...], shk_ref[...], QK_ROPE // 2)
    cqt, slq, shq = cq_ref[...], slq_ref[...], shq_ref[...]
    for c in range(MLA_HEADS // 2):
        cols = slice(c * 2 * LANES, (c + 1) * 2 * LANES)
        q2 = jnp.dot(cq, wq_ref[:, cols], preferred_element_type=F32)
        k2 = jnp.dot(ckv, wk_ref[:, cols], preferred_element_type=F32)
        v2 = jnp.dot(ckv, wv_ref[:, cols], preferred_element_type=F32)
        for half in range(2):
            h = 2 * c + half
            lanes = slice(half * LANES, (half + 1) * LANES)
            qt_ref[0, h] = _rope(q2[:, lanes], cqt, slq, shq, QK_ROPE // 2).T.astype(BF16)
            k_ref[0, h] = (k2[:, lanes] + k_rope).astype(BF16)
            vt_ref[0, h] = _with_ones_lane(v2[:, lanes], V_DIM).T.astype(BF16)


def _proj_mla(x, wd_p, gq, gkv, wq_p, wk_p, wv_p, tabs_q, tabs_k):
    B, S, D = x.shape
    tm = min(PROJ_ROWS, S)
    H = MLA_HEADS
    const = lambda b, i: (0, 0)
    tab = pl.BlockSpec((tm, LANES), lambda b, i: (i, 0))
    full = lambda a: pl.BlockSpec(a.shape, const)
    return pl.pallas_call(
        _proj_mla_kernel,
        grid=(B, S // tm),
        in_specs=[pl.BlockSpec((1, tm, D), lambda b, i: (b, i, 0)),
                  full(wd_p), full(gq), full(gkv), full(wq_p), full(wk_p), full(wv_p),
                  tab, tab, tab, tab, tab, tab],
        out_specs=[pl.BlockSpec((1, H, LANES, tm), lambda b, i: (b, 0, 0, i)),
                   pl.BlockSpec((1, H, tm, LANES), lambda b, i: (b, 0, i, 0)),
                   pl.BlockSpec((1, H, LANES, tm), lambda b, i: (b, 0, 0, i))],
        out_shape=[jax.ShapeDtypeStruct((B, H, LANES, S), BF16),
                   jax.ShapeDtypeStruct((B, H, S, LANES), BF16),
                   jax.ShapeDtypeStruct((B, H, LANES, S), BF16)],
        compiler_params=_cparams(("parallel", "parallel")),
        name="proj_mla",
    )(x, wd_p, gq, gkv, wq_p, wk_p, wv_p, *tabs_q, *tabs_k)


def _pad_lanes(a, width=LANES):
    return jnp.pad(a, [(0, 0)] * (a.ndim - 1) + [(0, width - a.shape[-1])])


def _head_cols(w, n_heads, dim):
    return _pad_lanes(w.reshape(w.shape[0], n_heads, dim)).reshape(w.shape[0], n_heads * LANES)


def _head_rows(w, n_heads, dim):
    n = w.shape[1]
    return jnp.pad(w.reshape(n_heads, dim, n), ((0, 0), (0, LANES - dim), (0, 0))).reshape(n_heads * LANES, n)


def _angles(pos, dim):
    freqs = ROPE_THETA ** (-(jnp.arange(0, dim, 2, dtype=F32) / dim))
    return pos[:, None] * freqs[None, :]


def _place(S, pieces):
    out = jnp.zeros((S, LANES), F32)
    for off, val in pieces:
        out = out.at[:, off:off + val.shape[1]].set(val)
    return out


def _tables(S):
    t = jnp.arange(S)
    ar = _angles((t // GRID_W).astype(F32), HEAD_DIM // 2)
    ac = _angles((t % GRID_W).astype(F32), HEAD_DIM // 2)
    q4 = HEAD_DIM // 4
    tabs_a = (_place(S, [(0, jnp.cos(ar)), (q4, jnp.cos(ar)), (2 * q4, jnp.cos(ac)), (3 * q4, jnp.cos(ac))]),
              _place(S, [(0, -jnp.sin(ar)), (2 * q4, -jnp.sin(ac))]),
              _place(S, [(q4, jnp.sin(ar)), (3 * q4, jnp.sin(ac))]))
    asq = _angles(t.astype(F32), HEAD_DIM)
    h2 = HEAD_DIM // 2
    tabs_b = (_place(S, [(0, jnp.cos(asq)), (h2, jnp.cos(asq))]),
              _place(S, [(0, -jnp.sin(asq))]),
              _place(S, [(h2, jnp.sin(asq))]))
    am = _angles(t.astype(F32), QK_ROPE)
    r2 = QK_ROPE // 2
    ones = jnp.ones((S, QK_NOPE), F32)
    tabs_mk = (_place(S, [(0, ones), (QK_NOPE, jnp.cos(am)), (QK_NOPE + r2, jnp.cos(am))]),
               _place(S, [(QK_NOPE, -jnp.sin(am))]),
               _place(S, [(QK_NOPE + r2, jnp.sin(am))]))
    scale = (QK_NOPE + QK_ROPE) ** -0.5
    tabs_mq = tuple(tb * scale for tb in tabs_mk)
    return tabs_a, tabs_b, tabs_mq, tabs_mk


def _split_bf16(w):
    hi = w.astype(BF16)
    return hi, (w - hi.astype(F32)).astype(BF16)


def _moe_layer(x1, x1b, aff, w_gate, w_up, w_down, g, b):
    B, S, D = x1.shape
    T = min(MOE_TOKENS, S)
    rel, cnt = _topk(aff, T)
    nblk = ((cnt.astype(jnp.int32) + MOE_SLOTS - 1) // MOE_SLOTS).transpose(0, 2, 1).reshape(-1)
    relt = rel.reshape(B, N_EXPERTS, 1, S)
    relc = rel.transpose(0, 2, 1)
    affc = aff.transpose(0, 2, 1)
    f = _moe(x1b, relt, relc, affc, nblk, w_gate.astype(BF16), w_up.astype(BF16), w_down.astype(BF16), T)
    return _add_ln(x1, f, g, b)


def kernel(x, ab_w_in, ab_q_norm, ab_k_norm, ab_sink, ab_w_out, mla_w_down, mla_q_norm, mla_kv_norm,
           mla_w_uq, mla_w_ukv, mla_w_out, ln_mix_g, ln_mix_b, moe_router, moe_w_gate, moe_w_up, moe_w_down,
           ln_ffn_g, ln_ffn_b):
    B, S, D = x.shape
    tabs_a, tabs_b, tabs_mq, tabs_mk = _tables(S)
    row = lambda v: v.reshape(1, -1)

    n_ab = A_HEADS + 2 * A_KV + B_HEADS + 2 * B_KV
    w_in_p = _head_cols(ab_w_in[0], n_ab, HEAD_DIM).astype(BF16)
    qta, ka, vta, qtb, kb, vtb = _proj_ab(x, w_in_p, _pad_lanes(row(ab_q_norm[0])), _pad_lanes(row(ab_k_norm[0])),
                                        tabs_a, tabs_b)
    oa = _flash(qta, ka, vta, HEAD_DIM)
    ob = _window(qtb, kb, vtb, ab_sink[0], HEAD_DIM)
    n_a = A_HEADS * HEAD_DIM
    w_oa = _head_rows(ab_w_out[0][:n_a], A_HEADS, HEAD_DIM).astype(BF16)
    w_ob = _head_rows(ab_w_out[0][n_a:], B_HEADS, HEAD_DIM).astype(BF16)
    wr_hi, wr_lo = _split_bf16(moe_router[0].T)
    x1, x1b, aff = _out_ln_router([oa, ob], [w_oa, w_ob], x, row(ln_mix_g[0]), row(ln_mix_b[0]), wr_hi, wr_lo)
    x = _moe_layer(x1, x1b, aff, moe_w_gate[0], moe_w_up[0], moe_w_down[0], row(ln_ffn_g[0]), row(ln_ffn_b[0]))

    wd = mla_w_down[0]
    wd_p = jnp.concatenate(
        [wd[:, :Q_LORA + KV_LORA],
         jnp.zeros((D, QK_NOPE), F32), wd[:, Q_LORA + KV_LORA:], jnp.zeros((D, LANES - QK_NOPE - QK_ROPE), F32)],
        axis=1).astype(BF16)
    wq_p = _head_cols(mla_w_uq[0], MLA_HEADS, QK_NOPE + QK_ROPE).astype(BF16)
    wkv = mla_w_ukv[0].reshape(KV_LORA, MLA_HEADS, QK_NOPE + V_DIM)
    wk_p = _head_cols(wkv[:, :, :QK_NOPE].reshape(KV_LORA, -1), MLA_HEADS, QK_NOPE).astype(BF16)
    wv_p = _head_cols(wkv[:, :, QK_NOPE:].reshape(KV_LORA, -1), MLA_HEADS, V_DIM).astype(BF16)
    qt, k, vt = _proj_mla(x, wd_p, row(mla_q_norm[0]), row(mla_kv_norm[0]), wq_p, wk_p, wv_p, tabs_mq, tabs_mk)
    oc = _flash(qt, k, vt, V_DIM)
    w_oc = _head_rows(mla_w_out[0], MLA_HEADS, V_DIM).astype(BF16)
    wr_hi, wr_lo = _split_bf16(moe_router[1].T)
    x1, x1b, aff = _out_ln_router([oc], [w_oc], x, row(ln_mix_g[1]), row(ln_mix_b[1]), wr_hi, wr_lo)
    x = _moe_layer(x1, x1b, aff, moe_w_gate[1], moe_w_up[1], moe_w_down[1], row(ln_ffn_g[1]), row(ln_ffn_b[1]))
    return x
```

```python
import functools

import jax
import jax.numpy as jnp
from jax import lax
from jax.experimental import pallas as pl
from jax.experimental.pallas import tpu as pltpu

F32 = jnp.float32
BF16 = jnp.bfloat16

GRID_W = 64
ROPE_THETA = 10000.0
HEAD_DIM = 64
A_HEADS, A_KV = 8, 2
B_HEADS, B_KV = 8, 2
WINDOW = 128
MLA_HEADS = 16
Q_LORA, KV_LORA = 256, 128
QK_NOPE, QK_ROPE, V_DIM = 64, 32, 64
N_EXPERTS = 16
EC_FACTOR = 2
DEPTH = 2
ALPHA = (2.0 * DEPTH) ** 0.25
NEG_INF = -1e30
RMS_EPS = 1e-6
LOG2E = 1.4426950408889634
LN_EPS = 1e-5

LANES = 128
VMEM_LIMIT = 56 * 1024 * 1024

PROJ_ROWS = 512
ATTN_ROWS = 4096
ATTN_KEYS = 512
ATTN_COLS = 512
WIN_Q = 1024
MOE_TOKENS = 2048
MOE_SLOTS = 128
CUMSUM_CHUNK = 256


def _cparams(sem):
    return pltpu.CompilerParams(dimension_semantics=sem, vmem_limit_bytes=VMEM_LIMIT)


def _rope(x, c, s_lo, s_hi, shift):
    return x * c + pltpu.roll(x, LANES - shift, 1) * s_lo + pltpu.roll(x, shift, 1) * s_hi


def _rms(x, g, n):
    ms = jnp.sum(x * x, axis=1, keepdims=True) * (1.0 / n)
    return x * lax.rsqrt(ms + RMS_EPS) * g


def _layer_norm(y, g, b):
    mu = jnp.mean(y, axis=1, keepdims=True)
    d = y - mu
    var = jnp.mean(d * d, axis=1, keepdims=True)
    return d * lax.rsqrt(var + LN_EPS) * g + b


def _with_ones_lane(v, n):
    lane = lax.broadcasted_iota(jnp.int32, (1, LANES), 1)
    return jnp.where(lane == n, 1.0, v)


def _proj_ab_kernel(x_ref, w_ref, gq_ref, gk_ref, ca_ref, sla_ref, sha_ref, cb_ref, slb_ref, shb_ref,
                    qa_ref, ka_ref, va_ref, qb_ref, kb_ref, vb_ref):
    xb = x_ref[0].astype(BF16)
    scale = HEAD_DIM ** -0.5 * LOG2E
    ca, sla, sha = ca_ref[...], sla_ref[...], sha_ref[...]
    cb, slb, shb = cb_ref[...], slb_ref[...], shb_ref[...]
    n_groups = A_HEADS + 2 * A_KV + B_HEADS + 2 * B_KV
    for c in range(n_groups // 2):
        pr = jnp.dot(xb, w_ref[:, c * 2 * LANES:(c + 1) * 2 * LANES], preferred_element_type=F32)
        for half in range(2):
            g = 2 * c + half
            ph = pr[:, half * LANES:(half + 1) * LANES]
            if g < A_HEADS:
                q = _rope(_rms(ph, gq_ref[...], HEAD_DIM), ca, sla, sha, HEAD_DIM // 4) * scale
                qa_ref[0, g] = q.T.astype(BF16)
            elif g < A_HEADS + A_KV:
                k = _rope(_rms(ph, gk_ref[...], HEAD_DIM), ca, sla, sha, HEAD_DIM // 4)
                ka_ref[0, g - A_HEADS] = k.astype(BF16)
            elif g < A_HEADS + 2 * A_KV:
                va_ref[0, g - A_HEADS - A_KV] = _with_ones_lane(ph, HEAD_DIM).T.astype(BF16)
            elif g < A_HEADS + 2 * A_KV + B_HEADS:
                q = _rope(ph, cb, slb, shb, HEAD_DIM // 2) * scale
                qb_ref[0, g - A_HEADS - 2 * A_KV] = q.T.astype(BF16)
            elif g < A_HEADS + 2 * A_KV + B_HEADS + B_KV:
                k = _rope(ph, cb, slb, shb, HEAD_DIM // 2)
                kb_ref[0, g - A_HEADS - 2 * A_KV - B_HEADS] = k.astype(BF16)
            else:
                vb_ref[0, g - A_HEADS - 2 * A_KV - B_HEADS - B_KV] = _with_ones_lane(ph, HEAD_DIM).T.astype(BF16)


def _proj_ab(x, w_p, gq, gk, tabs_a, tabs_b):
    B, S, D = x.shape
    tm = min(PROJ_ROWS, S)
    ns = S // tm
    row = lambda b, i: (b, i, 0)
    tab = pl.BlockSpec((tm, LANES), lambda b, i: (i, 0))
    vec = pl.BlockSpec((1, LANES), lambda b, i: (0, 0))
    hm = lambda h: pl.BlockSpec((1, h, tm, LANES), lambda b, i: (b, 0, i, 0))
    tr = lambda h: pl.BlockSpec((1, h, LANES, tm), lambda b, i: (b, 0, 0, i))
    sd = lambda h: jax.ShapeDtypeStruct((B, h, S, LANES), BF16)
    sdt = lambda h: jax.ShapeDtypeStruct((B, h, LANES, S), BF16)
    return pl.pallas_call(
        _proj_ab_kernel,
        grid=(B, ns),
        in_specs=[pl.BlockSpec((1, tm, D), row),
                  pl.BlockSpec(w_p.shape, lambda b, i: (0, 0)),
                  vec, vec, tab, tab, tab, tab, tab, tab],
        out_specs=[tr(A_HEADS), hm(A_KV), tr(A_KV), tr(B_HEADS), hm(B_KV), tr(B_KV)],
        out_shape=[sdt(A_HEADS), sd(A_KV), sdt(A_KV), sdt(B_HEADS), sd(B_KV), sdt(B_KV)],
        compiler_params=_cparams(("parallel", "parallel")),
        name="proj_ab",
    )(x, w_p, gq, gk, *tabs_a, *tabs_b)


def _flash_kernel(qt_ref, k_ref, vt_ref, o_ref, m_sc, acc_sc, *, G, tq, tk, nk, cw, sum_row):
    M = G * tq
    m_sc[...] = jnp.full((1, M), -jnp.inf, F32)
    acc_sc[...] = jnp.zeros((LANES, M), F32)

    def body(kb, carry):
        off = pl.multiple_of(kb * tk, tk)
        k = k_ref[0, 0, pl.ds(off, tk), :]
        vt = vt_ref[0, 0, :, pl.ds(off, tk)]

        def scores(c):
            g, j = divmod(c * cw, tq)
            return jnp.dot(k, qt_ref[0, g, :, j:j + cw], preferred_element_type=F32)

        def rescale(cols, alpha, pv):
            acc_sc[:, cols] = acc_sc[:, cols] * alpha + pv

        n_sub = M // cw
        st = {c: scores(c) for c in range(min(2, n_sub))}
        pending = None
        for c in range(n_sub):
            cols = slice(c * cw, (c + 1) * cw)
            m_prev = m_sc[:, cols]
            m_new = jnp.maximum(m_prev, jnp.max(st[c], axis=0, keepdims=True))
            alpha = jnp.exp2(m_prev - m_new)
            pt = jnp.exp2(st.pop(c) - m_new).astype(BF16)
            m_sc[:, cols] = m_new
            pv = jnp.dot(vt, pt, preferred_element_type=F32)
            if pending is not None:
                rescale(*pending)
            pending = (cols, alpha, pv)
            if c + 2 < n_sub:
                st[c + 2] = scores(c + 2)
        rescale(*pending)
        return carry

    lax.fori_loop(0, nk, body, 0)
    acc = acc_sc[...]
    o = acc / acc[sum_row:sum_row + 1, :]
    for g in range(G):
        o_ref[0, :, g * LANES:(g + 1) * LANES] = o[:, g * tq:(g + 1) * tq].T.astype(BF16)


def _flash(qt, k, vt, sum_row):
    B, H, _, S = qt.shape
    HK = k.shape[1]
    G = H // HK
    tq = min(ATTN_ROWS // G, S)
    tk = min(ATTN_KEYS, S)
    M = G * tq
    cw = min(ATTN_COLS, tq)
    return pl.pallas_call(
        functools.partial(_flash_kernel, G=G, tq=tq, tk=tk, nk=S // tk, cw=cw, sum_row=sum_row),
        grid=(B, HK, S // tq),
        in_specs=[pl.BlockSpec((1, G, LANES, tq), lambda b, h, i: (b, h, 0, i)),
                  pl.BlockSpec((1, 1, S, LANES), lambda b, h, i: (b, h, 0, 0)),
                  pl.BlockSpec((1, 1, LANES, S), lambda b, h, i: (b, h, 0, 0))],
        out_specs=pl.BlockSpec((1, tq, G * LANES), lambda b, h, i: (b, i, h)),
        out_shape=jax.ShapeDtypeStruct((B, S, H * LANES), BF16),
        scratch_shapes=[pltpu.VMEM((1, M), F32), pltpu.VMEM((LANES, M), F32)],
        compiler_params=_cparams(("parallel", "parallel", "parallel")),
        name="flash_attn",
    )(qt, k, vt)


def _window_kernel(sink_ref, qt_ref, k_ref, vt_ref, o_ref, *, G, tq, span, S, sum_row):
    kvh = pl.program_id(1)
    start = pl.program_id(2) * tq
    kstart = pl.multiple_of(jnp.clip(start - WINDOW, 0, S - span), LANES)
    k = k_ref[0, 0, pl.ds(kstart, span), :]
    vt = vt_ref[0, 0, :, pl.ds(kstart, span)]
    kpos = kstart + lax.broadcasted_iota(jnp.int32, (span, 1), 0)
    qpos = start + lax.broadcasted_iota(jnp.int32, (1, tq), 1)
    valid = jnp.abs(qpos - kpos) <= WINDOW
    for g in range(G):
        st = jnp.dot(k, qt_ref[0, g], preferred_element_type=F32)
        st = jnp.where(valid, st, NEG_INF)
        sink = sink_ref[kvh * G + g] * LOG2E
        m = jnp.maximum(jnp.max(st, axis=0, keepdims=True), sink)
        pt = jnp.exp2(st - m).astype(BF16)
        acc = jnp.dot(vt, pt, preferred_element_type=F32)
        o = acc / (acc[sum_row:sum_row + 1, :] + jnp.exp2(sink - m))
        o_ref[0, :, g * LANES:(g + 1) * LANES] = o.T.astype(BF16)


def _window(qt, k, vt, sink, sum_row):
    B, H, _, S = qt.shape
    HK = k.shape[1]
    G = H // HK
    tq = min(WIN_Q, S)
    span = min(tq + 2 * WINDOW, S)
    return pl.pallas_call(
        functools.partial(_window_kernel, G=G, tq=tq, span=span, S=S, sum_row=sum_row),
        grid_spec=pltpu.PrefetchScalarGridSpec(
            num_scalar_prefetch=1,
            grid=(B, HK, S // tq),
            in_specs=[pl.BlockSpec((1, G, LANES, tq), lambda b, h, i, sk: (b, h, 0, i)),
                      pl.BlockSpec((1, 1, S, LANES), lambda b, h, i, sk: (b, h, 0, 0)),
                      pl.BlockSpec((1, 1, LANES, S), lambda b, h, i, sk: (b, h, 0, 0))],
            out_specs=pl.BlockSpec((1, tq, G * LANES), lambda b, h, i, sk: (b, i, h))),
        out_shape=jax.ShapeDtypeStruct((B, S, H * LANES), BF16),
        compiler_params=_cparams(("parallel", "parallel", "parallel")),
        name="window_attn",
    )(sink, qt, k, vt)


def _out_ln_router_kernel(*refs, n_in):
    o_refs = refs[:n_in]
    w_refs = refs[n_in:2 * n_in]
    x_ref, g_ref, b_ref, wrh_ref, wrl_ref, x1_ref, x1b_ref, aff_ref = refs[2 * n_in:]
    h = jnp.dot(o_refs[0][0], w_refs[0][...], preferred_element_type=F32)
    for i in range(1, n_in):
        h = h + jnp.dot(o_refs[i][0], w_refs[i][...], preferred_element_type=F32)
    x1 = _layer_norm(ALPHA * x_ref[0] + h, g_ref[...], b_ref[...])
    x1_ref[0] = x1
    x_hi = x1.astype(BF16)
    x1b_ref[0] = x_hi
    x_lo = (x1 - x_hi.astype(F32)).astype(BF16)
    nt = (((1,), (1,)), ((), ()))
    logits = (lax.dot_general(wrh_ref[...], x_hi, nt, preferred_element_type=F32)
              + lax.dot_general(wrh_ref[...], x_lo, nt, preferred_element_type=F32)
              + lax.dot_general(wrl_ref[...], x_hi, nt, preferred_element_type=F32))
    z = jnp.exp(logits - jnp.max(logits, axis=0, keepdims=True))
    aff_ref[0] = z / jnp.sum(z, axis=0, keepdims=True)


def _out_ln_router(os_, ws_, x, g, b, wr_hi, wr_lo):
    B, S, D = x.shape
    E = wr_hi.shape[0]
    tm = min(PROJ_ROWS, S)
    n_in = len(os_)
    row = lambda bb, i: (bb, i, 0)
    const = lambda bb, i: (0, 0)
    in_specs = ([pl.BlockSpec((1, tm, o.shape[2]), row) for o in os_]
                + [pl.BlockSpec(w.shape, const) for w in ws_]
                + [pl.BlockSpec((1, tm, D), row), pl.BlockSpec((1, D), const), pl.BlockSpec((1, D), const),
                   pl.BlockSpec((E, D), const), pl.BlockSpec((E, D), const)])
    return pl.pallas_call(
        functools.partial(_out_ln_router_kernel, n_in=n_in),
        grid=(B, S // tm),
        in_specs=in_specs,
        out_specs=[pl.BlockSpec((1, tm, D), row), pl.BlockSpec((1, tm, D), row),
                   pl.BlockSpec((1, E, tm), lambda bb, i: (bb, 0, i))],
        out_shape=[jax.ShapeDtypeStruct((B, S, D), F32), jax.ShapeDtypeStruct((B, S, D), BF16),
                   jax.ShapeDtypeStruct((B, E, S), F32)],
        compiler_params=_cparams(("parallel", "parallel")),
        name="out_ln_router",
    )(*os_, *ws_, x, g, b, wr_hi, wr_lo)


def _topk_kernel(aff_ref, tri_ref, rel_ref, cnt_ref, *, S, E, cap, T, CH):
    aff = aff_ref[0]
    bits = pltpu.bitcast(aff, jnp.int32)
    capf = jnp.float32(cap)

    def count(mask):
        return jnp.sum(jnp.where(mask, 1.0, 0.0), axis=1, keepdims=True)

    def thr_body(i, t):
        cand = t | jnp.left_shift(jnp.int32(1), 30 - i)
        return jnp.where(count(bits >= cand) >= capf, cand, t)

    thr = lax.fori_loop(0, 31, thr_body, jnp.zeros((E, 1), jnp.int32))
    gt = bits > thr
    ties = bits == thr
    need = capf - count(gt)
    idx = lax.broadcasted_iota(jnp.int32, (E, S), 1)
    nbits = max(1, (S - 1).bit_length())

    def cut_body(i, c):
        cand = c | jnp.left_shift(jnp.int32(1), nbits - 1 - i)
        return jnp.where(count(ties & (idx < cand)) < need, cand, c)

    cut = lax.fori_loop(0, nbits, cut_body, jnp.zeros((E, 1), jnp.int32))
    sel = gt | (ties & (idx <= cut))

    tri = tri_ref[...]
    nsc = S // T
    lane_sc = lax.broadcasted_iota(jnp.int32, (E, nsc), 1)
    cnt = jnp.zeros((E, nsc), F32)
    for sc in range(nsc):
        run = jnp.zeros((E, 1), F32)
        for ch in range(T // CH):
            lo = sc * T + ch * CH
            selc = jnp.where(sel[:, lo:lo + CH], 1.0, 0.0)
            incl = jnp.dot(selc.astype(BF16), tri, preferred_element_type=F32)
            rel = jnp.where(selc > 0.0, incl - 1.0 + run, -1.0)
            rel_ref[0, :, lo:lo + CH] = rel
            run = run + incl[:, CH - 1:CH]
        cnt = jnp.where(lane_sc == sc, run, cnt)
    cnt_ref[0] = cnt


def _topk(aff, T):
    B, E, S = aff.shape
    cap = EC_FACTOR * S // N_EXPERTS
    CH = min(CUMSUM_CHUNK, T)
    r = lax.broadcasted_iota(jnp.int32, (CH, CH), 0)
    c = lax.broadcasted_iota(jnp.int32, (CH, CH), 1)
    tri = jnp.where(r <= c, 1.0, 0.0).astype(BF16)
    nsc = S // T
    return pl.pallas_call(
        functools.partial(_topk_kernel, S=S, E=E, cap=cap, T=T, CH=CH),
        grid=(B,),
        in_specs=[pl.BlockSpec((1, E, S), lambda b: (b, 0, 0)),
                  pl.BlockSpec((CH, CH), lambda b: (0, 0))],
        out_specs=[pl.BlockSpec((1, E, S), lambda b: (b, 0, 0)),
                   pl.BlockSpec((1, E, nsc), lambda b: (b, 0, 0))],
        out_shape=[jax.ShapeDtypeStruct((B, E, S), F32), jax.ShapeDtypeStruct((B, E, nsc), F32)],
        compiler_params=_cparams(("parallel",)),
        name="topk_select",
    )(aff, tri)


def _moe_kernel(nblk_ref, xb_ref, relt_ref, relc_ref, affc_ref, wg_ref, wu_ref, wd_ref, f_ref, y_sc,
                *, T, RB, E, nsc, TS):
    b, sc, e = pl.program_id(0), pl.program_id(1), pl.program_id(2)

    @pl.when(e == 0)
    def _():
        f_ref[...] = jnp.zeros(f_ref.shape, F32)

    nb = nblk_ref[(b * nsc + sc) * E + e]
    relt = relt_ref[0, 0]

    def ffn_block(j, carry):
        slot = (j * RB + lax.broadcasted_iota(jnp.int32, (RB, 1), 0)).astype(F32)
        onehot = jnp.where(relt == slot, 1.0, 0.0).astype(BF16)
        xg = jnp.dot(onehot, xb_ref[0], preferred_element_type=F32).astype(BF16)
        hg = jnp.dot(xg, wg_ref[0], preferred_element_type=F32)
        hu = jnp.dot(xg, wu_ref[0], preferred_element_type=F32)
        h = (hg / (1.0 + jnp.exp(-hg)) * hu).astype(BF16)
        y = jnp.dot(h, wd_ref[0], preferred_element_type=F32)
        y_sc[pl.ds(pl.multiple_of(j * RB, RB), RB), :] = y.astype(BF16)
        return carry

    lax.fori_loop(0, nb, ffn_block, 0)

    @pl.when(nb % 2 == 1)
    def _():
        y_sc[pl.ds(pl.multiple_of(nb * RB, RB), RB), :] = jnp.zeros((RB, y_sc.shape[1]), BF16)

    lane_e = lax.broadcasted_iota(jnp.int32, (1, E), 1) == e

    def scatter_block(kg, carry):
        koff = pl.multiple_of(kg * 2 * RB, 2 * RB)
        slot = (koff + lax.broadcasted_iota(jnp.int32, (1, 2 * RB), 1)).astype(F32)
        yk = y_sc[pl.ds(koff, 2 * RB), :]
        for ts in range(T // TS):
            rows = slice(ts * TS, (ts + 1) * TS)
            relc = jnp.sum(jnp.where(lane_e, relc_ref[0, rows, :], 0.0), axis=1, keepdims=True)
            gate = jnp.sum(jnp.where(lane_e, affc_ref[0, rows, :], 0.0), axis=1, keepdims=True)
            onehot_t = jnp.where(relc == slot, 1.0, 0.0).astype(BF16)
            f_ref[0, rows, :] += gate * jnp.dot(onehot_t, yk, preferred_element_type=F32)
        return carry

    lax.fori_loop(0, (nb + 1) // 2, scatter_block, 0)


def _moe(xb, relt, relc, affc, nblk, wg, wu, wd, T):
    B, S, D = xb.shape
    E, _, F = wg.shape
    nsc = S // T
    RB = MOE_SLOTS
    cap = EC_FACTOR * S // N_EXPERTS
    max_blocks = -(-min(cap, T) // RB)
    y_rows = (max_blocks + 1) * RB
    TS = min(512, T)
    return pl.pallas_call(
        functools.partial(_moe_kernel, T=T, RB=RB, E=E, nsc=nsc, TS=TS),
        grid_spec=pltpu.PrefetchScalarGridSpec(
            num_scalar_prefetch=1,
            grid=(B, nsc, E),
            in_specs=[pl.BlockSpec((1, T, D), lambda b, s, e, n: (b, s, 0)),
                      pl.BlockSpec((1, 1, 1, T), lambda b, s, e, n: (b, e, 0, s)),
                      pl.BlockSpec((1, T, E), lambda b, s, e, n: (b, s, 0)),
                      pl.BlockSpec((1, T, E), lambda b, s, e, n: (b, s, 0)),
                      pl.BlockSpec((1, D, F), lambda b, s, e, n: (e, 0, 0)),
                      pl.BlockSpec((1, D, F), lambda b, s, e, n: (e, 0, 0)),
                      pl.BlockSpec((1, F, D), lambda b, s, e, n: (e, 0, 0))],
            out_specs=pl.BlockSpec((1, T, D), lambda b, s, e, n: (b, s, 0)),
            scratch_shapes=[pltpu.VMEM((y_rows, D), BF16)]),
        out_shape=jax.ShapeDtypeStruct((B, S, D), F32),
        compiler_params=_cparams(("parallel", "parallel", "arbitrary")),
        name="moe_ffn",
    )(nblk, xb, relt, relc, affc, wg, wu, wd)


def _add_ln_kernel(x_ref, f_ref, g_ref, b_ref, o_ref):
    o_ref[0] = _layer_norm(ALPHA * x_ref[0] + f_ref[0], g_ref[...], b_ref[...])


def _add_ln(x, f, g, b):
    B, S, D = x.shape
    tm = min(PROJ_ROWS, S)
    row = lambda bb, i: (bb, i, 0)
    const = lambda bb, i: (0, 0)
    return pl.pallas_call(
        _add_ln_kernel,
        grid=(B, S // tm),
        in_specs=[pl.BlockSpec((1, tm, D), row), pl.BlockSpec((1, tm, D), row),
                  pl.BlockSpec((1, D), const), pl.BlockSpec((1, D), const)],
        out_specs=pl.BlockSpec((1, tm, D), row),
        out_shape=jax.ShapeDtypeStruct((B, S, D), F32),
        compiler_params=_cparams(("parallel", "parallel")),
        name="add_ln",
    )(x, f, g, b)


def _proj_mla_kernel(x_ref, wd_ref, gq_ref, gkv_ref, wq_ref, wk_ref, wv_ref,
                     cq_ref, slq_ref, shq_ref, ck_ref, slk_ref, shk_ref, qt_ref, k_ref, vt_ref):
    xb = x_ref[0].astype(BF16)
    d = jnp.dot(xb, wd_ref[...], preferred_element_type=F32)
    cq = _rms(d[:, :Q_LORA], gq_ref[...], Q_LORA).astype(BF16)
    ckv = _rms(d[:, Q_LORA:Q_LORA + KV_LORA], gkv_ref[...], KV_LORA).astype(BF16)
    k_rope = _rope(d[:, Q_LORA + KV_LORA:], ck_ref[...], slk_ref[...], shk_ref[...], QK_ROPE // 2)
    cqt, slq, shq = cq_ref[...], slq_ref[...], shq_ref[...]
    for c in range(MLA_HEADS // 2):
        cols = slice(c * 2 * LANES, (c + 1) * 2 * LANES)
        q2 = jnp.dot(cq, wq_ref[:, cols], preferred_element_type=F32)
        k2 = jnp.dot(ckv, wk_ref[:, cols], preferred_element_type=F32)
        v2 = jnp.dot(ckv, wv_ref[:, cols], preferred_element_type=F32)
        for half in range(2):
            h = 2 * c + half
            lanes = slice(half * LANES, (half + 1) * LANES)
            qt_ref[0, h] = _rope(q2[:, lanes], cqt, slq, shq, QK_ROPE // 2).T.astype(BF16)
            k_ref[0, h] = (k2[:, lanes] + k_rope).astype(BF16)
            vt_ref[0, h] = _with_ones_lane(v2[:, lanes], V_DIM).T.astype(BF16)


def _proj_mla(x, wd_p, gq, gkv, wq_p, wk_p, wv_p, tabs_q, tabs_k):
    B, S, D = x.shape
    tm = min(PROJ_ROWS, S)
    H = MLA_HEADS
    const = lambda b, i: (0, 0)
    tab = pl.BlockSpec((tm, LANES), lambda b, i: (i, 0))
    full = lambda a: pl.BlockSpec(a.shape, const)
    return pl.pallas_call(
        _proj_mla_kernel,
        grid=(B, S // tm),
        in_specs=[pl.BlockSpec((1, tm, D), lambda b, i: (b, i, 0)),
                  full(wd_p), full(gq), full(gkv), full(wq_p), full(wk_p), full(wv_p),
                  tab, tab, tab, tab, tab, tab],
        out_specs=[pl.BlockSpec((1, H, LANES, tm), lambda b, i: (b, 0, 0, i)),
                   pl.BlockSpec((1, H, tm, LANES), lambda b, i: (b, 0, i, 0)),
                   pl.BlockSpec((1, H, LANES, tm), lambda b, i: (b, 0, 0, i))],
        out_shape=[jax.ShapeDtypeStruct((B, H, LANES, S), BF16),
                   jax.ShapeDtypeStruct((B, H, S, LANES), BF16),
                   jax.ShapeDtypeStruct((B, H, LANES, S), BF16)],
        compiler_params=_cparams(("parallel", "parallel")),
        name="proj_mla",
    )(x, wd_p, gq, gkv, wq_p, wk_p, wv_p, *tabs_q, *tabs_k)


def _pad_lanes(a, width=LANES):
    return jnp.pad(a, [(0, 0)] * (a.ndim - 1) + [(0, width - a.shape[-1])])


def _head_cols(w, n_heads, dim):
    return _pad_lanes(w.reshape(w.shape[0], n_heads, dim)).reshape(w.shape[0], n_heads * LANES)


def _head_rows(w, n_heads, dim):
    n = w.shape[1]
    return jnp.pad(w.reshape(n_heads, dim, n), ((0, 0), (0, LANES - dim), (0, 0))).reshape(n_heads * LANES, n)


def _angles(pos, dim):
    freqs = ROPE_THETA ** (-(jnp.arange(0, dim, 2, dtype=F32) / dim))
    return pos[:, None] * freqs[None, :]


def _place(S, pieces):
    out = jnp.zeros((S, LANES), F32)
    for off, val in pieces:
        out = out.at[:, off:off + val.shape[1]].set(val)
    return out


def _tables(S):
    t = jnp.arange(S)
    ar = _angles((t // GRID_W).astype(F32), HEAD_DIM // 2)
    ac = _angles((t % GRID_W).astype(F32), HEAD_DIM // 2)
    q4 = HEAD_DIM // 4
    tabs_a = (_place(S, [(0, jnp.cos(ar)), (q4, jnp.cos(ar)), (2 * q4, jnp.cos(ac)), (3 * q4, jnp.cos(ac))]),
              _place(S, [(0, -jnp.sin(ar)), (2 * q4, -jnp.sin(ac))]),
              _place(S, [(q4, jnp.sin(ar)), (3 * q4, jnp.sin(ac))]))
    asq = _angles(t.astype(F32), HEAD_DIM)
    h2 = HEAD_DIM // 2
    tabs_b = (_place(S, [(0, jnp.cos(asq)), (h2, jnp.cos(asq))]),
              _place(S, [(0, -jnp.sin(asq))]),
              _place(S, [(h2, jnp.sin(asq))]))
    am = _angles(t.astype(F32), QK_ROPE)
    r2 = QK_ROPE // 2
    ones = jnp.ones((S, QK_NOPE), F32)
    tabs_mk = (_place(S, [(0, ones), (QK_NOPE, jnp.cos(am)), (QK_NOPE + r2, jnp.cos(am))]),
               _place(S, [(QK_NOPE, -jnp.sin(am))]),
               _place(S, [(QK_NOPE + r2, jnp.sin(am))]))
    scale = (QK_NOPE + QK_ROPE) ** -0.5 * LOG2E
    tabs_mq = tuple(tb * scale for tb in tabs_mk)
    return tabs_a, tabs_b, tabs_mq, tabs_mk


def _split_bf16(w):
    hi = w.astype(BF16)
    return hi, (w - hi.astype(F32)).astype(BF16)


def _moe_layer(x1, x1b, aff, w_gate, w_up, w_down, g, b):
    B, S, D = x1.shape
    T = min(MOE_TOKENS, S)
    rel, cnt = _topk(aff, T)
    nblk = ((cnt.astype(jnp.int32) + MOE_SLOTS - 1) // MOE_SLOTS).transpose(0, 2, 1).reshape(-1)
    relt = rel.reshape(B, N_EXPERTS, 1, S)
    relc = rel.transpose(0, 2, 1)
    affc = aff.transpose(0, 2, 1)
    f = _moe(x1b, relt, relc, affc, nblk, w_gate.astype(BF16), w_up.astype(BF16), w_down.astype(BF16), T)
    return _add_ln(x1, f, g, b)


def kernel(x, ab_w_in, ab_q_norm, ab_k_norm, ab_sink, ab_w_out, mla_w_down, mla_q_norm, mla_kv_norm,
           mla_w_uq, mla_w_ukv, mla_w_out, ln_mix_g, ln_mix_b, moe_router, moe_w_gate, moe_w_up, moe_w_down,
           ln_ffn_g, ln_ffn_b):
    B, S, D = x.shape
    tabs_a, tabs_b, tabs_mq, tabs_mk = _tables(S)
    row = lambda v: v.reshape(1, -1)

    n_ab = A_HEADS + 2 * A_KV + B_HEADS + 2 * B_KV
    w_in_p = _head_cols(ab_w_in[0], n_ab, HEAD_DIM).astype(BF16)
    qta, ka, vta, qtb, kb, vtb = _proj_ab(x, w_in_p, _pad_lanes(row(ab_q_norm[0])), _pad_lanes(row(ab_k_norm[0])),
                                        tabs_a, tabs_b)
    oa = _flash(qta, ka, vta, HEAD_DIM)
    ob = _window(qtb, kb, vtb, ab_sink[0], HEAD_DIM)
    n_a = A_HEADS * HEAD_DIM
    w_oa = _head_rows(ab_w_out[0][:n_a], A_HEADS, HEAD_DIM).astype(BF16)
    w_ob = _head_rows(ab_w_out[0][n_a:], B_HEADS, HEAD_DIM).astype(BF16)
    wr_hi, wr_lo = _split_bf16(moe_router[0].T)
    x1, x1b, aff = _out_ln_router([oa, ob], [w_oa, w_ob], x, row(ln_mix_g[0]), row(ln_mix_b[0]), wr_hi, wr_lo)
    x = _moe_layer(x1, x1b, aff, moe_w_gate[0], moe_w_up[0], moe_w_down[0], row(ln_ffn_g[0]), row(ln_ffn_b[0]))

    wd = mla_w_down[0]
    wd_p = jnp.concatenate(
        [wd[:, :Q_LORA + KV_LORA],
         jnp.zeros((D, QK_NOPE), F32), wd[:, Q_LORA + KV_LORA:], jnp.zeros((D, LANES - QK_NOPE - QK_ROPE), F32)],
        axis=1).astype(BF16)
    wq_p = _head_cols(mla_w_uq[0], MLA_HEADS, QK_NOPE + QK_ROPE).astype(BF16)
    wkv = mla_w_ukv[0].reshape(KV_LORA, MLA_HEADS, QK_NOPE + V_DIM)
    wk_p = _head_cols(wkv[:, :, :QK_NOPE].reshape(KV_LORA, -1), MLA_HEADS, QK_NOPE).astype(BF16)
    wv_p = _head_cols(wkv[:, :, QK_NOPE:].reshape(KV_LORA, -1), MLA_HEADS, V_DIM).astype(BF16)
    qt, k, vt = _proj_mla(x, wd_p, row(mla_q_norm[0]), row(mla_kv_norm[0]), wq_p, wk_p, wv_p, tabs_mq, tabs_mk)
    oc = _flash(qt, k, vt, V_DIM)
    w_oc = _head_rows(mla_w_out[0], MLA_HEADS, V_DIM).astype(BF16)
    wr_hi, wr_lo = _split_bf16(moe_router[1].T)
    x1, x1b, aff = _out_ln_router([oc], [w_oc], x, row(ln_mix_g[1]), row(ln_mix_b[1]), wr_hi, wr_lo)
    x = _moe_layer(x1, x1b, aff, moe_w_gate[1], moe_w_up[1], moe_w_down[1], row(ln_ffn_g[1]), row(ln_ffn_b[1]))
    return x
```

```python
import functools

import jax
import jax.numpy as jnp
from jax import lax
from jax.experimental import pallas as pl
from jax.experimental.pallas import tpu as pltpu

F32 = jnp.float32
BF16 = jnp.bfloat16

GRID_W = 64
ROPE_THETA = 10000.0
HEAD_DIM = 64
A_HEADS, A_KV = 8, 2
B_HEADS, B_KV = 8, 2
WINDOW = 128
MLA_HEADS = 16
Q_LORA, KV_LORA = 256, 128
QK_NOPE, QK_ROPE, V_DIM = 64, 32, 64
N_EXPERTS = 16
EC_FACTOR = 2
DEPTH = 2
ALPHA = (2.0 * DEPTH) ** 0.25
NEG_INF = -1e30
SHIFT_SLACK = 64.0
RMS_EPS = 1e-6
LOG2E = 1.4426950408889634
LN_EPS = 1e-5

LANES = 128
VMEM_LIMIT = 56 * 1024 * 1024

PROJ_ROWS = 512
ATTN_ROWS = 4096
ATTN_KEYS = 1024
ATTN_KEY_SUB = 512
ATTN_COLS = 512
AHEAD = 2
WIN_Q = 1024
MOE_TOKENS = 2048
MOE_SLOTS = 128
CUMSUM_CHUNK = 256


def _cparams(sem):
    return pltpu.CompilerParams(dimension_semantics=sem, vmem_limit_bytes=VMEM_LIMIT)


def _rope(x, c, s_lo, s_hi, shift):
    return x * c + pltpu.roll(x, LANES - shift, 1) * s_lo + pltpu.roll(x, shift, 1) * s_hi


def _rms(x, g, n):
    ms = jnp.sum(x * x, axis=1, keepdims=True) * (1.0 / n)
    return x * lax.rsqrt(ms + RMS_EPS) * g


def _layer_norm(y, g, b):
    mu = jnp.mean(y, axis=1, keepdims=True)
    d = y - mu
    var = jnp.mean(d * d, axis=1, keepdims=True)
    return d * lax.rsqrt(var + LN_EPS) * g + b


def _with_ones_lane(v, n):
    lane = lax.broadcasted_iota(jnp.int32, (1, LANES), 1)
    return jnp.where(lane == n, 1.0, v)


def _proj_ab_kernel(x_ref, w_ref, gq_ref, gk_ref, ca_ref, sla_ref, sha_ref, cb_ref, slb_ref, shb_ref,
                    qa_ref, ka_ref, va_ref, qb_ref, kb_ref, vb_ref):
    xb = x_ref[0].astype(BF16)
    scale = HEAD_DIM ** -0.5 * LOG2E
    ca, sla, sha = ca_ref[...], sla_ref[...], sha_ref[...]
    cb, slb, shb = cb_ref[...], slb_ref[...], shb_ref[...]
    n_groups = A_HEADS + 2 * A_KV + B_HEADS + 2 * B_KV
    for c in range(n_groups // 2):
        pr = jnp.dot(xb, w_ref[:, c * 2 * LANES:(c + 1) * 2 * LANES], preferred_element_type=F32)
        for half in range(2):
            g = 2 * c + half
            ph = pr[:, half * LANES:(half + 1) * LANES]
            if g < A_HEADS:
                q = _rope(_rms(ph, gq_ref[...], HEAD_DIM), ca, sla, sha, HEAD_DIM // 4) * scale
                qa_ref[0, g] = q.T.astype(BF16)
            elif g < A_HEADS + A_KV:
                k = _rope(_rms(ph, gk_ref[...], HEAD_DIM), ca, sla, sha, HEAD_DIM // 4)
                ka_ref[0, g - A_HEADS] = k.astype(BF16)
            elif g < A_HEADS + 2 * A_KV:
                va_ref[0, g - A_HEADS - A_KV] = _with_ones_lane(ph, HEAD_DIM).T.astype(BF16)
            elif g < A_HEADS + 2 * A_KV + B_HEADS:
                q = _rope(ph, cb, slb, shb, HEAD_DIM // 2) * scale
                qb_ref[0, g - A_HEADS - 2 * A_KV] = q.T.astype(BF16)
            elif g < A_HEADS + 2 * A_KV + B_HEADS + B_KV:
                k = _rope(ph, cb, slb, shb, HEAD_DIM // 2)
                kb_ref[0, g - A_HEADS - 2 * A_KV - B_HEADS] = k.astype(BF16)
            else:
                vb_ref[0, g - A_HEADS - 2 * A_KV - B_HEADS - B_KV] = _with_ones_lane(ph, HEAD_DIM).T.astype(BF16)


def _proj_ab(x, w_p, gq, gk, tabs_a, tabs_b):
    B, S, D = x.shape
    tm = min(PROJ_ROWS, S)
    ns = S // tm
    row = lambda b, i: (b, i, 0)
    tab = pl.BlockSpec((tm, LANES), lambda b, i: (i, 0))
    vec = pl.BlockSpec((1, LANES), lambda b, i: (0, 0))
    hm = lambda h: pl.BlockSpec((1, h, tm, LANES), lambda b, i: (b, 0, i, 0))
    tr = lambda h: pl.BlockSpec((1, h, LANES, tm), lambda b, i: (b, 0, 0, i))
    sd = lambda h: jax.ShapeDtypeStruct((B, h, S, LANES), BF16)
    sdt = lambda h: jax.ShapeDtypeStruct((B, h, LANES, S), BF16)
    return pl.pallas_call(
        _proj_ab_kernel,
        grid=(B, ns),
        in_specs=[pl.BlockSpec((1, tm, D), row),
                  pl.BlockSpec(w_p.shape, lambda b, i: (0, 0)),
                  vec, vec, tab, tab, tab, tab, tab, tab],
        out_specs=[tr(A_HEADS), hm(A_KV), tr(A_KV), tr(B_HEADS), hm(B_KV), tr(B_KV)],
        out_shape=[sdt(A_HEADS), sd(A_KV), sdt(A_KV), sdt(B_HEADS), sd(B_KV), sdt(B_KV)],
        compiler_params=_cparams(("parallel", "parallel")),
        name="proj_ab",
    )(x, w_p, gq, gk, *tabs_a, *tabs_b)


def _flash_kernel(qt_ref, k_ref, vt_ref, o_ref, m_sc, acc_sc, pv_sc, *, G, tq, tk, ks, nk, cw, sum_row):
    M = G * tq
    n_sub = M // cw

    def q_tile(c):
        g, j = divmod(c * cw, tq)
        return qt_ref[0, g, :, j:j + cw]

    def cols(c):
        return slice(c * cw, (c + 1) * cw)

    k_first = k_ref[0, 0, 0:LANES, :]
    for c in range(n_sub):
        s0 = jnp.dot(k_first, q_tile(c), preferred_element_type=F32)
        m_sc[:, cols(c)] = jnp.max(s0, axis=0, keepdims=True)
    acc_sc[...] = jnp.zeros((LANES, M), F32)

    def body(kb, carry):
        off = pl.multiple_of(kb * tk, tk)
        k = k_ref[0, 0, pl.ds(off, tk), :]
        vt = vt_ref[0, 0, :, pl.ds(off, tk)]

        tiles = [(c, s) for c in range(n_sub) for s in range(tk // ks)]

        def scores(t):
            c, s = tiles[t]
            return jnp.dot(k[s * ks:(s + 1) * ks], q_tile(c), preferred_element_type=F32)

        excess = None
        ahead = {t: scores(t) for t in range(min(AHEAD, len(tiles)))}
        pts = []
        for t, (c, s) in enumerate(tiles):
            st = ahead.pop(t)
            if t + AHEAD < len(tiles):
                ahead[t + AHEAD] = scores(t + AHEAD)
            m = m_sc[:, cols(c)]
            over = jnp.max(st, axis=0, keepdims=True) - m
            excess = over if excess is None else jnp.maximum(excess, over)
            pts.append(jnp.exp2(st - m).astype(BF16))
            if s == tk // ks - 1:
                pt = pts[0] if len(pts) == 1 else jnp.concatenate(pts, axis=0)
                pv_sc[:, cols(c)] = jnp.dot(vt, pt, preferred_element_type=F32)
                pts = []
        renew = jnp.max(excess) > SHIFT_SLACK

        @pl.when(jnp.logical_not(renew))
        def _():
            acc_sc[...] += pv_sc[...]

        @pl.when(renew)
        def _():
            for c, s in tiles:
                rows = slice(s * ks, (s + 1) * ks)
                st = jnp.dot(k[rows], q_tile(c), preferred_element_type=F32)
                m_prev = m_sc[:, cols(c)]
                m_new = jnp.maximum(m_prev, jnp.max(st, axis=0, keepdims=True))
                alpha = jnp.exp2(m_prev - m_new)
                pt = jnp.exp2(st - m_new).astype(BF16)
                acc_sc[:, cols(c)] = (acc_sc[:, cols(c)] * alpha
                                      + jnp.dot(vt[:, rows], pt, preferred_element_type=F32))
                m_sc[:, cols(c)] = m_new

        return carry

    lax.fori_loop(0, nk, body, 0)
    acc = acc_sc[...]
    o = acc / acc[sum_row:sum_row + 1, :]
    for g in range(G):
        o_ref[0, :, g * LANES:(g + 1) * LANES] = o[:, g * tq:(g + 1) * tq].T.astype(BF16)


def _flash(qt, k, vt, sum_row):
    B, H, _, S = qt.shape
    HK = k.shape[1]
    G = H // HK
    tq = min(ATTN_ROWS // G, S)
    tk = min(ATTN_KEYS, S)
    M = G * tq
    cw = min(ATTN_COLS, tq)
    ks = min(ATTN_KEY_SUB, tk)
    return pl.pallas_call(
        functools.partial(_flash_kernel, G=G, tq=tq, tk=tk, ks=ks, nk=S // tk, cw=cw, sum_row=sum_row),
        grid=(B, HK, S // tq),
        in_specs=[pl.BlockSpec((1, G, LANES, tq), lambda b, h, i: (b, h, 0, i)),
                  pl.BlockSpec((1, 1, S, LANES), lambda b, h, i: (b, h, 0, 0)),
                  pl.BlockSpec((1, 1, LANES, S), lambda b, h, i: (b, h, 0, 0))],
        out_specs=pl.BlockSpec((1, tq, G * LANES), lambda b, h, i: (b, i, h)),
        out_shape=jax.ShapeDtypeStruct((B, S, H * LANES), BF16),
        scratch_shapes=[pltpu.VMEM((1, M), F32), pltpu.VMEM((LANES, M), F32), pltpu.VMEM((LANES, M), F32)],
        compiler_params=_cparams(("parallel", "parallel", "parallel")),
        name="flash_attn",
    )(qt, k, vt)


def _window_kernel(sink_ref, qt_ref, k_ref, vt_ref, o_ref, *, G, tq, span, S, sum_row):
    kvh = pl.program_id(1)
    start = pl.program_id(2) * tq
    for j in range(tq // LANES):
        q0 = start + j * LANES
        kstart = pl.multiple_of(jnp.clip(q0 - WINDOW, 0, S - span), LANES)
        k = k_ref[0, 0, pl.ds(kstart, span), :]
        vt = vt_ref[0, 0, :, pl.ds(kstart, span)]
        kpos = kstart + lax.broadcasted_iota(jnp.int32, (span, 1), 0)
        qpos = q0 + lax.broadcasted_iota(jnp.int32, (1, LANES), 1)
        valid = jnp.abs(qpos - kpos) <= WINDOW
        for g in range(G):
            st = jnp.dot(k, qt_ref[0, g, :, j * LANES:(j + 1) * LANES], preferred_element_type=F32)
            st = jnp.where(valid, st, NEG_INF)
            sink = sink_ref[kvh * G + g] * LOG2E
            m = jnp.maximum(jnp.max(st, axis=0, keepdims=True), sink)
            pt = jnp.exp2(st - m).astype(BF16)
            acc = jnp.dot(vt, pt, preferred_element_type=F32)
            o = acc / (acc[sum_row:sum_row + 1, :] + jnp.exp2(sink - m))
            o_ref[0, j * LANES:(j + 1) * LANES, g * LANES:(g + 1) * LANES] = o.T.astype(BF16)


def _window(qt, k, vt, sink, sum_row):
    B, H, _, S = qt.shape
    HK = k.shape[1]
    G = H // HK
    tq = min(WIN_Q, S)
    span = min(LANES + 2 * WINDOW, S)
    return pl.pallas_call(
        functools.partial(_window_kernel, G=G, tq=tq, span=span, S=S, sum_row=sum_row),
        grid_spec=pltpu.PrefetchScalarGridSpec(
            num_scalar_prefetch=1,
            grid=(B, HK, S // tq),
            in_specs=[pl.BlockSpec((1, G, LANES, tq), lambda b, h, i, sk: (b, h, 0, i)),
                      pl.BlockSpec((1, 1, S, LANES), lambda b, h, i, sk: (b, h, 0, 0)),
                      pl.BlockSpec((1, 1, LANES, S), lambda b, h, i, sk: (b, h, 0, 0))],
            out_specs=pl.BlockSpec((1, tq, G * LANES), lambda b, h, i, sk: (b, i, h))),
        out_shape=jax.ShapeDtypeStruct((B, S, H * LANES), BF16),
        compiler_params=_cparams(("parallel", "parallel", "parallel")),
        name="window_attn",
    )(sink, qt, k, vt)


def _out_ln_router_kernel(*refs, n_in):
    o_refs = refs[:n_in]
    w_refs = refs[n_in:2 * n_in]
    x_ref, g_ref, b_ref, wrh_ref, wrl_ref, x1_ref, x1b_ref, aff_ref = refs[2 * n_in:]
    h = jnp.dot(o_refs[0][0], w_refs[0][...], preferred_element_type=F32)
    for i in range(1, n_in):
        h = h + jnp.dot(o_refs[i][0], w_refs[i][...], preferred_element_type=F32)
    x1 = _layer_norm(ALPHA * x_ref[0] + h, g_ref[...], b_ref[...])
    x1_ref[0] = x1
    x_hi = x1.astype(BF16)
    x1b_ref[0] = x_hi
    x_lo = (x1 - x_hi.astype(F32)).astype(BF16)
    nt = (((1,), (1,)), ((), ()))
    logits = (lax.dot_general(wrh_ref[...], x_hi, nt, preferred_element_type=F32)
              + lax.dot_general(wrh_ref[...], x_lo, nt, preferred_element_type=F32)
              + lax.dot_general(wrl_ref[...], x_hi, nt, preferred_element_type=F32))
    z = jnp.exp(logits - jnp.max(logits, axis=0, keepdims=True))
    aff_ref[0] = z / jnp.sum(z, axis=0, keepdims=True)


def _out_ln_router(os_, ws_, x, g, b, wr_hi, wr_lo):
    B, S, D = x.shape
    E = wr_hi.shape[0]
    tm = min(PROJ_ROWS, S)
    n_in = len(os_)
    row = lambda bb, i: (bb, i, 0)
    const = lambda bb, i: (0, 0)
    in_specs = ([pl.BlockSpec((1, tm, o.shape[2]), row) for o in os_]
                + [pl.BlockSpec(w.shape, const) for w in ws_]
                + [pl.BlockSpec((1, tm, D), row), pl.BlockSpec((1, D), const), pl.BlockSpec((1, D), const),
                   pl.BlockSpec((E, D), const), pl.BlockSpec((E, D), const)])
    return pl.pallas_call(
        functools.partial(_out_ln_router_kernel, n_in=n_in),
        grid=(B, S // tm),
        in_specs=in_specs,
        out_specs=[pl.BlockSpec((1, tm, D), row), pl.BlockSpec((1, tm, D), row),
                   pl.BlockSpec((1, E, tm), lambda bb, i: (bb, 0, i))],
        out_shape=[jax.ShapeDtypeStruct((B, S, D), F32), jax.ShapeDtypeStruct((B, S, D), BF16),
                   jax.ShapeDtypeStruct((B, E, S), F32)],
        compiler_params=_cparams(("parallel", "parallel")),
        name="out_ln_router",
    )(*os_, *ws_, x, g, b, wr_hi, wr_lo)


def _topk_kernel(aff_ref, tri_ref, rel_ref, cnt_ref, *, S, E, cap, T, CH):
    aff = aff_ref[0]
    bits = pltpu.bitcast(aff, jnp.int32)
    capf = jnp.float32(cap)

    def count(mask):
        return jnp.sum(jnp.where(mask, 1.0, 0.0), axis=1, keepdims=True)

    def thr_body(i, t):
        cand = t | jnp.left_shift(jnp.int32(1), 30 - i)
        return jnp.where(count(bits >= cand) >= capf, cand, t)

    thr = lax.fori_loop(0, 31, thr_body, jnp.zeros((E, 1), jnp.int32))
    gt = bits > thr
    ties = bits == thr
    need = capf - count(gt)
    idx = lax.broadcasted_iota(jnp.int32, (E, S), 1)
    nbits = max(1, (S - 1).bit_length())

    def cut_body(i, c):
        cand = c | jnp.left_shift(jnp.int32(1), nbits - 1 - i)
        return jnp.where(count(ties & (idx < cand)) < need, cand, c)

    cut = lax.fori_loop(0, nbits, cut_body, jnp.zeros((E, 1), jnp.int32))
    sel = gt | (ties & (idx <= cut))

    tri = tri_ref[...]
    nsc = S // T
    lane_sc = lax.broadcasted_iota(jnp.int32, (E, nsc), 1)
    cnt = jnp.zeros((E, nsc), F32)
    for sc in range(nsc):
        run = jnp.zeros((E, 1), F32)
        for ch in range(T // CH):
            lo = sc * T + ch * CH
            selc = jnp.where(sel[:, lo:lo + CH], 1.0, 0.0)
            incl = jnp.dot(selc.astype(BF16), tri, preferred_element_type=F32)
            rel = jnp.where(selc > 0.0, incl - 1.0 + run, -1.0)
            rel_ref[0, :, lo:lo + CH] = rel
            run = run + incl[:, CH - 1:CH]
        cnt = jnp.where(lane_sc == sc, run, cnt)
    cnt_ref[0] = cnt


def _topk(aff, T):
    B, E, S = aff.shape
    cap = EC_FACTOR * S // N_EXPERTS
    CH = min(CUMSUM_CHUNK, T)
    r = lax.broadcasted_iota(jnp.int32, (CH, CH), 0)
    c = lax.broadcasted_iota(jnp.int32, (CH, CH), 1)
    tri = jnp.where(r <= c, 1.0, 0.0).astype(BF16)
    nsc = S // T
    return pl.pallas_call(
        functools.partial(_topk_kernel, S=S, E=E, cap=cap, T=T, CH=CH),
        grid=(B,),
        in_specs=[pl.BlockSpec((1, E, S), lambda b: (b, 0, 0)),
                  pl.BlockSpec((CH, CH), lambda b: (0, 0))],
        out_specs=[pl.BlockSpec((1, E, S), lambda b: (b, 0, 0)),
                   pl.BlockSpec((1, E, nsc), lambda b: (b, 0, 0))],
        out_shape=[jax.ShapeDtypeStruct((B, E, S), F32), jax.ShapeDtypeStruct((B, E, nsc), F32)],
        compiler_params=_cparams(("parallel",)),
        name="topk_select",
    )(aff, tri)


def _moe_kernel(nblk_ref, xb_ref, relt_ref, relc_ref, affc_ref, wg_ref, wu_ref, wd_ref, f_ref, y_sc,
                *, T, RB, E, nsc, TS):
    b, sc, e = pl.program_id(0), pl.program_id(1), pl.program_id(2)

    @pl.when(e == 0)
    def _():
        f_ref[...] = jnp.zeros(f_ref.shape, F32)

    nb = nblk_ref[(b * nsc + sc) * E + e]
    relt = relt_ref[0, 0]

    def ffn_block(j, carry):
        slot = (j * RB + lax.broadcasted_iota(jnp.int32, (RB, 1), 0)).astype(F32)
        onehot = jnp.where(relt == slot, 1.0, 0.0).astype(BF16)
        xg = jnp.dot(onehot, xb_ref[0], preferred_element_type=F32).astype(BF16)
        hg = jnp.dot(xg, wg_ref[0], preferred_element_type=F32)
        hu = jnp.dot(xg, wu_ref[0], preferred_element_type=F32)
        h = (hg / (1.0 + jnp.exp(-hg)) * hu).astype(BF16)
        y = jnp.dot(h, wd_ref[0], preferred_element_type=F32)
        y_sc[pl.ds(pl.multiple_of(j * RB, RB), RB), :] = y.astype(BF16)
        return carry

    lax.fori_loop(0, nb, ffn_block, 0)

    @pl.when(nb % 2 == 1)
    def _():
        y_sc[pl.ds(pl.multiple_of(nb * RB, RB), RB), :] = jnp.zeros((RB, y_sc.shape[1]), BF16)

    lane_e = lax.broadcasted_iota(jnp.int32, (1, E), 1) == e

    def scatter_block(kg, carry):
        koff = pl.multiple_of(kg * 2 * RB, 2 * RB)
        slot = (koff + lax.broadcasted_iota(jnp.int32, (1, 2 * RB), 1)).astype(F32)
        yk = y_sc[pl.ds(koff, 2 * RB), :]
        for ts in range(T // TS):
            rows = slice(ts * TS, (ts + 1) * TS)
            relc = jnp.sum(jnp.where(lane_e, relc_ref[0, rows, :], 0.0), axis=1, keepdims=True)
            gate = jnp.sum(jnp.where(lane_e, affc_ref[0, rows, :], 0.0), axis=1, keepdims=True)
            onehot_t = jnp.where(relc == slot, 1.0, 0.0).astype(BF16)
            f_ref[0, rows, :] += gate * jnp.dot(onehot_t, yk, preferred_element_type=F32)
        return carry

    lax.fori_loop(0, (nb + 1) // 2, scatter_block, 0)


def _moe(xb, relt, relc, affc, nblk, wg, wu, wd, T):
    B, S, D = xb.shape
    E, _, F = wg.shape
    nsc = S // T
    RB = MOE_SLOTS
    cap = EC_FACTOR * S // N_EXPERTS
    max_blocks = -(-min(cap, T) // RB)
    y_rows = (max_blocks + 1) * RB
    TS = min(512, T)
    return pl.pallas_call(
        functools.partial(_moe_kernel, T=T, RB=RB, E=E, nsc=nsc, TS=TS),
        grid_spec=pltpu.PrefetchScalarGridSpec(
            num_scalar_prefetch=1,
            grid=(B, nsc, E),
            in_specs=[pl.BlockSpec((1, T, D), lambda b, s, e, n: (b, s, 0)),
                      pl.BlockSpec((1, 1, 1, T), lambda b, s, e, n: (b, e, 0, s)),
                      pl.BlockSpec((1, T, E), lambda b, s, e, n: (b, s, 0)),
                      pl.BlockSpec((1, T, E), lambda b, s, e, n: (b, s, 0)),
                      pl.BlockSpec((1, D, F), lambda b, s, e, n: (e, 0, 0)),
                      pl.BlockSpec((1, D, F), lambda b, s, e, n: (e, 0, 0)),
                      pl.BlockSpec((1, F, D), lambda b, s, e, n: (e, 0, 0))],
            out_specs=pl.BlockSpec((1, T, D), lambda b, s, e, n: (b, s, 0)),
            scratch_shapes=[pltpu.VMEM((y_rows, D), BF16)]),
        out_shape=jax.ShapeDtypeStruct((B, S, D), F32),
        compiler_params=_cparams(("parallel", "parallel", "arbitrary")),
        name="moe_ffn",
    )(nblk, xb, relt, relc, affc, wg, wu, wd)


def _add_ln_kernel(x_ref, f_ref, g_ref, b_ref, o_ref):
    o_ref[0] = _layer_norm(ALPHA * x_ref[0] + f_ref[0], g_ref[...], b_ref[...])


def _add_ln(x, f, g, b):
    B, S, D = x.shape
    tm = min(PROJ_ROWS, S)
    row = lambda bb, i: (bb, i, 0)
    const = lambda bb, i: (0, 0)
    return pl.pallas_call(
        _add_ln_kernel,
        grid=(B, S // tm),
        in_specs=[pl.BlockSpec((1, tm, D), row), pl.BlockSpec((1, tm, D), row),
                  pl.BlockSpec((1, D), const), pl.BlockSpec((1, D), const)],
        out_specs=pl.BlockSpec((1, tm, D), row),
        out_shape=jax.ShapeDtypeStruct((B, S, D), F32),
        compiler_params=_cparams(("parallel", "parallel")),
        name="add_ln",
    )(x, f, g, b)


def _proj_mla_kernel(x_ref, wd_ref, gq_ref, gkv_ref, wq_ref, wk_ref, wv_ref,
                     cq_ref, slq_ref, shq_ref, ck_ref, slk_ref, shk_ref, qt_ref, k_ref, vt_ref):
    xb = x_ref[0].astype(BF16)
    d = jnp.dot(xb, wd_ref[...], preferred_element_type=F32)
    cq = _rms(d[:, :Q_LORA], gq_ref[...], Q_LORA).astype(BF16)
    ckv = _rms(d[:, Q_LORA:Q_LORA + KV_LORA], gkv_ref[...], KV_LORA).astype(BF16)
    k_rope = _rope(d[:, Q_LORA + KV_LORA:], ck_ref[...], slk_ref[...], shk_ref[...], QK_ROPE // 2)
    cqt, slq, shq = cq_ref[...], slq_ref[...], shq_ref[...]
    for c in range(MLA_HEADS // 2):
        cols = slice(c * 2 * LANES, (c + 1) * 2 * LANES)
        q2 = jnp.dot(cq, wq_ref[:, cols], preferred_element_type=F32)
        k2 = jnp.dot(ckv, wk_ref[:, cols], preferred_element_type=F32)
        v2 = jnp.dot(ckv, wv_ref[:, cols], preferred_element_type=F32)
        for half in range(2):
            h = 2 * c + half
            lanes = slice(half * LANES, (half + 1) * LANES)
            qt_ref[0, h] = _rope(q2[:, lanes], cqt, slq, shq, QK_ROPE // 2).T.astype(BF16)
            k_ref[0, h] = (k2[:, lanes] + k_rope).astype(BF16)
            vt_ref[0, h] = _with_ones_lane(v2[:, lanes], V_DIM).T.astype(BF16)


def _proj_mla(x, wd_p, gq, gkv, wq_p, wk_p, wv_p, tabs_q, tabs_k):
    B, S, D = x.shape
    tm = min(PROJ_ROWS, S)
    H = MLA_HEADS
    const = lambda b, i: (0, 0)
    tab = pl.BlockSpec((tm, LANES), lambda b, i: (i, 0))
    full = lambda a: pl.BlockSpec(a.shape, const)
    return pl.pallas_call(
        _proj_mla_kernel,
        grid=(B, S // tm),
        in_specs=[pl.BlockSpec((1, tm, D), lambda b, i: (b, i, 0)),
                  full(wd_p), full(gq), full(gkv), full(wq_p), full(wk_p), full(wv_p),
                  tab, tab, tab, tab, tab, tab],
        out_specs=[pl.BlockSpec((1, H, LANES, tm), lambda b, i: (b, 0, 0, i)),
                   pl.BlockSpec((1, H, tm, LANES), lambda b, i: (b, 0, i, 0)),
                   pl.BlockSpec((1, H, LANES, tm), lambda b, i: (b, 0, 0, i))],
        out_shape=[jax.ShapeDtypeStruct((B, H, LANES, S), BF16),
                   jax.ShapeDtypeStruct((B, H, S, LANES), BF16),
                   jax.ShapeDtypeStruct((B, H, LANES, S), BF16)],
        compiler_params=_cparams(("parallel", "parallel")),
        name="proj_mla",
    )(x, wd_p, gq, gkv, wq_p, wk_p, wv_p, *tabs_q, *tabs_k)


def _pad_lanes(a, width=LANES):
    return jnp.pad(a, [(0, 0)] * (a.ndim - 1) + [(0, width - a.shape[-1])])


def _head_cols(w, n_heads, dim):
    return _pad_lanes(w.reshape(w.shape[0], n_heads, dim)).reshape(w.shape[0], n_heads * LANES)


def _head_rows(w, n_heads, dim):
    n = w.shape[1]
    return jnp.pad(w.reshape(n_heads, dim, n), ((0, 0), (0, LANES - dim), (0, 0))).reshape(n_heads * LANES, n)


def _angles(pos, dim):
    freqs = ROPE_THETA ** (-(jnp.arange(0, dim, 2, dtype=F32) / dim))
    return pos[:, None] * freqs[None, :]


def _place(S, pieces):
    out = jnp.zeros((S, LANES), F32)
    for off, val in pieces:
        out = out.at[:, off:off + val.shape[1]].set(val)
    return out


def _tables(S):
    t = jnp.arange(S)
    ar = _angles((t // GRID_W).astype(F32), HEAD_DIM // 2)
    ac = _angles((t % GRID_W).astype(F32), HEAD_DIM // 2)
    q4 = HEAD_DIM // 4
    tabs_a = (_place(S, [(0, jnp.cos(ar)), (q4, jnp.cos(ar)), (2 * q4, jnp.cos(ac)), (3 * q4, jnp.cos(ac))]),
              _place(S, [(0, -jnp.sin(ar)), (2 * q4, -jnp.sin(ac))]),
              _place(S, [(q4, jnp.sin(ar)), (3 * q4, jnp.sin(ac))]))
    asq = _angles(t.astype(F32), HEAD_DIM)
    h2 = HEAD_DIM // 2
    tabs_b = (_place(S, [(0, jnp.cos(asq)), (h2, jnp.cos(asq))]),
              _place(S, [(0, -jnp.sin(asq))]),
              _place(S, [(h2, jnp.sin(asq))]))
    am = _angles(t.astype(F32), QK_ROPE)
    r2 = QK_ROPE // 2
    ones = jnp.ones((S, QK_NOPE), F32)
    tabs_mk = (_place(S, [(0, ones), (QK_NOPE, jnp.cos(am)), (QK_NOPE + r2, jnp.cos(am))]),
               _place(S, [(QK_NOPE, -jnp.sin(am))]),
               _place(S, [(QK_NOPE + r2, jnp.sin(am))]))
    scale = (QK_NOPE + QK_ROPE) ** -0.5 * LOG2E
    tabs_mq = tuple(tb * scale for tb in tabs_mk)
    return tabs_a, tabs_b, tabs_mq, tabs_mk


def _split_bf16(w):
    hi = w.astype(BF16)
    return hi, (w - hi.astype(F32)).astype(BF16)


def _moe_layer(x1, x1b, aff, w_gate, w_up, w_down, g, b):
    B, S, D = x1.shape
    T = min(MOE_TOKENS, S)
    rel, cnt = _topk(aff, T)
    nblk = ((cnt.astype(jnp.int32) + MOE_SLOTS - 1) // MOE_SLOTS).transpose(0, 2, 1).reshape(-1)
    relt = rel.reshape(B, N_EXPERTS, 1, S)
    relc = rel.transpose(0, 2, 1)
    affc = aff.transpose(0, 2, 1)
    f = _moe(x1b, relt, relc, affc, nblk, w_gate.astype(BF16), w_up.astype(BF16), w_down.astype(BF16), T)
    return _add_ln(x1, f, g, b)


def kernel(x, ab_w_in, ab_q_norm, ab_k_norm, ab_sink, ab_w_out, mla_w_down, mla_q_norm, mla_kv_norm,
           mla_w_uq, mla_w_ukv, mla_w_out, ln_mix_g, ln_mix_b, moe_router, moe_w_gate, moe_w_up, moe_w_down,
           ln_ffn_g, ln_ffn_b):
    B, S, D = x.shape
    tabs_a, tabs_b, tabs_mq, tabs_mk = _tables(S)
    row = lambda v: v.reshape(1, -1)

    n_ab = A_HEADS + 2 * A_KV + B_HEADS + 2 * B_KV
    w_in_p = _head_cols(ab_w_in[0], n_ab, HEAD_DIM).astype(BF16)
    qta, ka, vta, qtb, kb, vtb = _proj_ab(x, w_in_p, _pad_lanes(row(ab_q_norm[0])), _pad_lanes(row(ab_k_norm[0])),
                                        tabs_a, tabs_b)
    oa = _flash(qta, ka, vta, HEAD_DIM)
    ob = _window(qtb, kb, vtb, ab_sink[0], HEAD_DIM)
    n_a = A_HEADS * HEAD_DIM
    w_oa = _head_rows(ab_w_out[0][:n_a], A_HEADS, HEAD_DIM).astype(BF16)
    w_ob = _head_rows(ab_w_out[0][n_a:], B_HEADS, HEAD_DIM).astype(BF16)
    wr_hi, wr_lo = _split_bf16(moe_router[0].T)
    x1, x1b, aff = _out_ln_router([oa, ob], [w_oa, w_ob], x, row(ln_mix_g[0]), row(ln_mix_b[0]), wr_hi, wr_lo)
    x = _moe_layer(x1, x1b, aff, moe_w_gate[0], moe_w_up[0], moe_w_down[0], row(ln_ffn_g[0]), row(ln_ffn_b[0]))

    wd = mla_w_down[0]
    wd_p = jnp.concatenate(
        [wd[:, :Q_LORA + KV_LORA],
         jnp.zeros((D, QK_NOPE), F32), wd[:, Q_LORA + KV_LORA:], jnp.zeros((D, LANES - QK_NOPE - QK_ROPE), F32)],
        axis=1).astype(BF16)
    wq_p = _head_cols(mla_w_uq[0], MLA_HEADS, QK_NOPE + QK_ROPE).astype(BF16)
    wkv = mla_w_ukv[0].reshape(KV_LORA, MLA_HEADS, QK_NOPE + V_DIM)
    wk_p = _head_cols(wkv[:, :, :QK_NOPE].reshape(KV_LORA, -1), MLA_HEADS, QK_NOPE).astype(BF16)
    wv_p = _head_cols(wkv[:, :, QK_NOPE:].reshape(KV_LORA, -1), MLA_HEADS, V_DIM).astype(BF16)
    qt, k, vt = _proj_mla(x, wd_p, row(mla_q_norm[0]), row(mla_kv_norm[0]), wq_p, wk_p, wv_p, tabs_mq, tabs_mk)
    oc = _flash(qt, k, vt, V_DIM)
    w_oc = _head_rows(mla_w_out[0], MLA_HEADS, V_DIM).astype(BF16)
    wr_hi, wr_lo = _split_bf16(moe_router[1].T)
    x1, x1b, aff = _out_ln_router([oc], [w_oc], x, row(ln_mix_g[1]), row(ln_mix_b[1]), wr_hi, wr_lo)
    x = _moe_layer(x1, x1b, aff, moe_w_gate[1], moe_w_up[1], moe_w_down[1], row(ln_ffn_g[1]), row(ln_ffn_b[1]))
    return x
```

```python
import functools

import jax
import jax.numpy as jnp
from jax import lax
from jax.experimental import pallas as pl
from jax.experimental.pallas import tpu as pltpu

F32 = jnp.float32
BF16 = jnp.bfloat16

GRID_W = 64
ROPE_THETA = 10000.0
HEAD_DIM = 64
A_HEADS, A_KV = 8, 2
B_HEADS, B_KV = 8, 2
WINDOW = 128
MLA_HEADS = 16
Q_LORA, KV_LORA = 256, 128
QK_NOPE, QK_ROPE, V_DIM = 64, 32, 64
N_EXPERTS = 16
EC_FACTOR = 2
DEPTH = 2
ALPHA = (2.0 * DEPTH) ** 0.25
NEG_INF = -1e30
SHIFT_SLACK = 64.0
RMS_EPS = 1e-6
LOG2E = 1.4426950408889634
LN_EPS = 1e-5

LANES = 128
VMEM_LIMIT = 56 * 1024 * 1024

PROJ_ROWS = 512
ATTN_ROWS = 4096
ATTN_KEYS = 2048
ATTN_KEY_SUB = 512
ATTN_COLS = 256
AHEAD = 2
WIN_Q = 1024
MOE_TOKENS = 2048
MOE_SLOTS = 128
CUMSUM_CHUNK = 256


def _cparams(sem):
    return pltpu.CompilerParams(dimension_semantics=sem, vmem_limit_bytes=VMEM_LIMIT)


def _rope(x, c, s_lo, s_hi, shift):
    return x * c + pltpu.roll(x, LANES - shift, 1) * s_lo + pltpu.roll(x, shift, 1) * s_hi


def _rms(x, g, n):
    ms = jnp.sum(x * x, axis=1, keepdims=True) * (1.0 / n)
    return x * lax.rsqrt(ms + RMS_EPS) * g


def _layer_norm(y, g, b):
    mu = jnp.mean(y, axis=1, keepdims=True)
    d = y - mu
    var = jnp.mean(d * d, axis=1, keepdims=True)
    return d * lax.rsqrt(var + LN_EPS) * g + b


def _with_ones_lane(v, n):
    lane = lax.broadcasted_iota(jnp.int32, (1, LANES), 1)
    return jnp.where(lane == n, 1.0, v)


def _pack_head_pair(a, b, vd):
    lane = lax.broadcasted_iota(jnp.int32, (1, LANES), 1)
    return jnp.where(lane < vd, a, pltpu.roll(b, vd, 1))


def _proj_ab_kernel(x_ref, w_ref, gq_ref, gk_ref, ca_ref, sla_ref, sha_ref, cb_ref, slb_ref, shb_ref,
                    qa_ref, ka_ref, va_ref, qb_ref, kb_ref, vb_ref):
    xb = x_ref[0].astype(BF16)
    scale = HEAD_DIM ** -0.5 * LOG2E
    ca, sla, sha = ca_ref[...], sla_ref[...], sha_ref[...]
    cb, slb, shb = cb_ref[...], slb_ref[...], shb_ref[...]
    n_groups = A_HEADS + 2 * A_KV + B_HEADS + 2 * B_KV
    for c in range(n_groups // 2):
        pr = jnp.dot(xb, w_ref[:, c * 2 * LANES:(c + 1) * 2 * LANES], preferred_element_type=F32)
        for half in range(2):
            g = 2 * c + half
            ph = pr[:, half * LANES:(half + 1) * LANES]
            if g < A_HEADS:
                q = _rope(_rms(ph, gq_ref[...], HEAD_DIM), ca, sla, sha, HEAD_DIM // 4) * scale
                qa_ref[0, g] = q.T.astype(BF16)
            elif g < A_HEADS + A_KV:
                k = _rope(_rms(ph, gk_ref[...], HEAD_DIM), ca, sla, sha, HEAD_DIM // 4)
                ka_ref[0, g - A_HEADS] = k.astype(BF16)
            elif g < A_HEADS + 2 * A_KV:
                va_ref[0, g - A_HEADS - A_KV] = _with_ones_lane(ph, HEAD_DIM).T.astype(BF16)
            elif g < A_HEADS + 2 * A_KV + B_HEADS:
                q = _rope(ph, cb, slb, shb, HEAD_DIM // 2) * scale
                qb_ref[0, g - A_HEADS - 2 * A_KV] = q.T.astype(BF16)
            elif g < A_HEADS + 2 * A_KV + B_HEADS + B_KV:
                k = _rope(ph, cb, slb, shb, HEAD_DIM // 2)
                kb_ref[0, g - A_HEADS - 2 * A_KV - B_HEADS] = k.astype(BF16)
            else:
                vb_ref[0, g - A_HEADS - 2 * A_KV - B_HEADS - B_KV] = _with_ones_lane(ph, HEAD_DIM).T.astype(BF16)


def _proj_ab(x, w_p, gq, gk, tabs_a, tabs_b):
    B, S, D = x.shape
    tm = min(PROJ_ROWS, S)
    ns = S // tm
    row = lambda b, i: (b, i, 0)
    tab = pl.BlockSpec((tm, LANES), lambda b, i: (i, 0))
    vec = pl.BlockSpec((1, LANES), lambda b, i: (0, 0))
    hm = lambda h: pl.BlockSpec((1, h, tm, LANES), lambda b, i: (b, 0, i, 0))
    tr = lambda h: pl.BlockSpec((1, h, LANES, tm), lambda b, i: (b, 0, 0, i))
    sd = lambda h: jax.ShapeDtypeStruct((B, h, S, LANES), BF16)
    sdt = lambda h: jax.ShapeDtypeStruct((B, h, LANES, S), BF16)
    return pl.pallas_call(
        _proj_ab_kernel,
        grid=(B, ns),
        in_specs=[pl.BlockSpec((1, tm, D), row),
                  pl.BlockSpec(w_p.shape, lambda b, i: (0, 0)),
                  vec, vec, tab, tab, tab, tab, tab, tab],
        out_specs=[tr(A_HEADS), hm(A_KV), tr(A_KV), tr(B_HEADS), hm(B_KV), tr(B_KV)],
        out_shape=[sdt(A_HEADS), sd(A_KV), sdt(A_KV), sdt(B_HEADS), sd(B_KV), sdt(B_KV)],
        compiler_params=_cparams(("parallel", "parallel")),
        name="proj_ab",
    )(x, w_p, gq, gk, *tabs_a, *tabs_b)


def _flash_kernel(qt_ref, k_ref, vt_ref, o_ref, m_sc, acc_sc, pv_sc, *, NKV, G, tq, tk, ks, nk, cw, vd):
    heads = NKV * G
    M = heads * tq
    n_sub = M // cw

    def q_tile(c):
        h, j = divmod(c * cw, tq)
        return qt_ref[0, h, :, j:j + cw]

    def kv_of(c):
        return (c * cw // tq) // G

    def cols(c):
        return slice(c * cw, (c + 1) * cw)

    for c in range(n_sub):
        s0 = jnp.dot(k_ref[0, kv_of(c), 0:LANES, :], q_tile(c), preferred_element_type=F32)
        m_sc[:, cols(c)] = jnp.max(s0, axis=0, keepdims=True)
    acc_sc[...] = jnp.zeros((LANES, M), F32)

    def body(kb, carry):
        off = pl.multiple_of(kb * tk, tk)

        def keys(c, s):
            return k_ref[0, kv_of(c), pl.ds(pl.multiple_of(off + s * ks, ks), ks), :]

        def values_t(c, s=None):
            if s is None:
                return vt_ref[0, kv_of(c), :, pl.ds(off, tk)]
            return vt_ref[0, kv_of(c), :, pl.ds(pl.multiple_of(off + s * ks, ks), ks)]

        tiles = [(c, s) for c in range(n_sub) for s in range(tk // ks)]

        def scores(t):
            c, s = tiles[t]
            return jnp.dot(keys(c, s), q_tile(c), preferred_element_type=F32)

        excess = None
        ahead = {t: scores(t) for t in range(min(AHEAD, len(tiles)))}
        pts = []
        for t, (c, s) in enumerate(tiles):
            st = ahead.pop(t)
            if t + AHEAD < len(tiles):
                ahead[t + AHEAD] = scores(t + AHEAD)
            m = m_sc[:, cols(c)]
            over = jnp.max(st, axis=0, keepdims=True) - m
            excess = over if excess is None else jnp.maximum(excess, over)
            pts.append(jnp.exp2(st - m).astype(BF16))
            if s == tk // ks - 1:
                pt = pts[0] if len(pts) == 1 else jnp.concatenate(pts, axis=0)
                pv_sc[:, cols(c)] = jnp.dot(values_t(c), pt, preferred_element_type=F32)
                pts = []
        renew = jnp.max(excess) > SHIFT_SLACK

        @pl.when(jnp.logical_not(renew))
        def _():
            acc_sc[...] += pv_sc[...]

        @pl.when(renew)
        def _():
            for c, s in tiles:
                st = jnp.dot(keys(c, s), q_tile(c), preferred_element_type=F32)
                m_prev = m_sc[:, cols(c)]
                m_new = jnp.maximum(m_prev, jnp.max(st, axis=0, keepdims=True))
                alpha = jnp.exp2(m_prev - m_new)
                pt = jnp.exp2(st - m_new).astype(BF16)
                acc_sc[:, cols(c)] = (acc_sc[:, cols(c)] * alpha
                                      + jnp.dot(values_t(c, s), pt, preferred_element_type=F32))
                m_sc[:, cols(c)] = m_new

        return carry

    lax.fori_loop(0, nk, body, 0)
    acc = acc_sc[...]
    o = acc / acc[vd:vd + 1, :]
    for p in range(heads // 2):
        o_ref[0, :, p * LANES:(p + 1) * LANES] = _pack_head_pair(
            o[:, 2 * p * tq:(2 * p + 1) * tq].T, o[:, (2 * p + 1) * tq:(2 * p + 2) * tq].T, vd).astype(BF16)


def _flash(qt, k, vt, vd):
    B, H, _, S = qt.shape
    HK = k.shape[1]
    G = H // HK
    NKV = max(1, 2 // G)
    heads = NKV * G
    tq = min(ATTN_ROWS // heads, S)
    tk = min(ATTN_KEYS, S)
    M = heads * tq
    cw = min(ATTN_COLS, tq)
    ks = min(ATTN_KEY_SUB, tk)
    return pl.pallas_call(
        functools.partial(_flash_kernel, NKV=NKV, G=G, tq=tq, tk=tk, ks=ks, nk=S // tk, cw=cw, vd=vd),
        grid=(B, HK // NKV, S // tq),
        in_specs=[pl.BlockSpec((1, heads, LANES, tq), lambda b, h, i: (b, h, 0, i)),
                  pl.BlockSpec((1, NKV, S, LANES), lambda b, h, i: (b, h, 0, 0)),
                  pl.BlockSpec((1, NKV, LANES, S), lambda b, h, i: (b, h, 0, 0))],
        out_specs=pl.BlockSpec((1, tq, heads * vd), lambda b, h, i: (b, i, h)),
        out_shape=jax.ShapeDtypeStruct((B, S, H * vd), BF16),
        scratch_shapes=[pltpu.VMEM((1, M), F32), pltpu.VMEM((LANES, M), F32), pltpu.VMEM((LANES, M), F32)],
        compiler_params=_cparams(("parallel", "parallel", "parallel")),
        name="flash_attn",
    )(qt, k, vt)


def _window_kernel(sink_ref, qt_ref, k_ref, vt_ref, o_ref, *, G, tq, span, S, vd):
    kvh = pl.program_id(1)
    start = pl.program_id(2) * tq
    for j in range(tq // LANES):
        q0 = start + j * LANES
        kstart = pl.multiple_of(jnp.clip(q0 - WINDOW, 0, S - span), LANES)
        k = k_ref[0, 0, pl.ds(kstart, span), :]
        vt = vt_ref[0, 0, :, pl.ds(kstart, span)]
        kpos = kstart + lax.broadcasted_iota(jnp.int32, (span, 1), 0)
        qpos = q0 + lax.broadcasted_iota(jnp.int32, (1, LANES), 1)
        valid = jnp.abs(qpos - kpos) <= WINDOW
        outs = []
        for g in range(G):
            st = jnp.dot(k, qt_ref[0, g, :, j * LANES:(j + 1) * LANES], preferred_element_type=F32)
            st = jnp.where(valid, st, NEG_INF)
            sink = sink_ref[kvh * G + g] * LOG2E
            m = jnp.maximum(jnp.max(st, axis=0, keepdims=True), sink)
            pt = jnp.exp2(st - m).astype(BF16)
            acc = jnp.dot(vt, pt, preferred_element_type=F32)
            outs.append((acc / (acc[vd:vd + 1, :] + jnp.exp2(sink - m))).T)
        for p in range(G // 2):
            o_ref[0, j * LANES:(j + 1) * LANES, p * LANES:(p + 1) * LANES] = _pack_head_pair(
                outs[2 * p], outs[2 * p + 1], vd).astype(BF16)


def _window(qt, k, vt, sink, vd):
    B, H, _, S = qt.shape
    HK = k.shape[1]
    G = H // HK
    tq = min(WIN_Q, S)
    span = min(LANES + 2 * WINDOW, S)
    return pl.pallas_call(
        functools.partial(_window_kernel, G=G, tq=tq, span=span, S=S, vd=vd),
        grid_spec=pltpu.PrefetchScalarGridSpec(
            num_scalar_prefetch=1,
            grid=(B, HK, S // tq),
            in_specs=[pl.BlockSpec((1, G, LANES, tq), lambda b, h, i, sk: (b, h, 0, i)),
                      pl.BlockSpec((1, 1, S, LANES), lambda b, h, i, sk: (b, h, 0, 0)),
                      pl.BlockSpec((1, 1, LANES, S), lambda b, h, i, sk: (b, h, 0, 0))],
            out_specs=pl.BlockSpec((1, tq, G * vd), lambda b, h, i, sk: (b, i, h))),
        out_shape=jax.ShapeDtypeStruct((B, S, H * vd), BF16),
        compiler_params=_cparams(("parallel", "parallel", "parallel")),
        name="window_attn",
    )(sink, qt, k, vt)


def _out_ln_router_kernel(*refs, n_in):
    o_refs = refs[:n_in]
    w_refs = refs[n_in:2 * n_in]
    x_ref, g_ref, b_ref, wrh_ref, wrl_ref, x1_ref, x1b_ref, aff_ref = refs[2 * n_in:]
    h = jnp.dot(o_refs[0][0], w_refs[0][...], preferred_element_type=F32)
    for i in range(1, n_in):
        h = h + jnp.dot(o_refs[i][0], w_refs[i][...], preferred_element_type=F32)
    x1 = _layer_norm(ALPHA * x_ref[0] + h, g_ref[...], b_ref[...])
    x1_ref[0] = x1
    x_hi = x1.astype(BF16)
    x1b_ref[0] = x_hi
    x_lo = (x1 - x_hi.astype(F32)).astype(BF16)
    nt = (((1,), (1,)), ((), ()))
    logits = (lax.dot_general(wrh_ref[...], x_hi, nt, preferred_element_type=F32)
              + lax.dot_general(wrh_ref[...], x_lo, nt, preferred_element_type=F32)
              + lax.dot_general(wrl_ref[...], x_hi, nt, preferred_element_type=F32))
    z = jnp.exp(logits - jnp.max(logits, axis=0, keepdims=True))
    aff_ref[0] = z / jnp.sum(z, axis=0, keepdims=True)


def _out_ln_router(os_, ws_, x, g, b, wr_hi, wr_lo):
    B, S, D = x.shape
    E = wr_hi.shape[0]
    tm = min(PROJ_ROWS, S)
    n_in = len(os_)
    row = lambda bb, i: (bb, i, 0)
    const = lambda bb, i: (0, 0)
    in_specs = ([pl.BlockSpec((1, tm, o.shape[2]), row) for o in os_]
                + [pl.BlockSpec(w.shape, const) for w in ws_]
                + [pl.BlockSpec((1, tm, D), row), pl.BlockSpec((1, D), const), pl.BlockSpec((1, D), const),
                   pl.BlockSpec((E, D), const), pl.BlockSpec((E, D), const)])
    return pl.pallas_call(
        functools.partial(_out_ln_router_kernel, n_in=n_in),
        grid=(B, S // tm),
        in_specs=in_specs,
        out_specs=[pl.BlockSpec((1, tm, D), row), pl.BlockSpec((1, tm, D), row),
                   pl.BlockSpec((1, E, tm), lambda bb, i: (bb, 0, i))],
        out_shape=[jax.ShapeDtypeStruct((B, S, D), F32), jax.ShapeDtypeStruct((B, S, D), BF16),
                   jax.ShapeDtypeStruct((B, E, S), F32)],
        compiler_params=_cparams(("parallel", "parallel")),
        name="out_ln_router",
    )(*os_, *ws_, x, g, b, wr_hi, wr_lo)


def _topk_kernel(aff_ref, tri_ref, rel_ref, cnt_ref, *, S, E, cap, T, CH):
    aff = aff_ref[0]
    bits = pltpu.bitcast(aff, jnp.int32)
    capf = jnp.float32(cap)

    def count(mask):
        return jnp.sum(jnp.where(mask, 1.0, 0.0), axis=1, keepdims=True)

    def thr_body(i, t):
        cand = t | jnp.left_shift(jnp.int32(1), 30 - i)
        return jnp.where(count(bits >= cand) >= capf, cand, t)

    thr = lax.fori_loop(0, 31, thr_body, jnp.zeros((E, 1), jnp.int32))
    gt = bits > thr
    ties = bits == thr
    need = capf - count(gt)
    idx = lax.broadcasted_iota(jnp.int32, (E, S), 1)
    nbits = max(1, (S - 1).bit_length())

    def cut_body(i, c):
        cand = c | jnp.left_shift(jnp.int32(1), nbits - 1 - i)
        return jnp.where(count(ties & (idx < cand)) < need, cand, c)

    cut = lax.fori_loop(0, nbits, cut_body, jnp.zeros((E, 1), jnp.int32))
    sel = gt | (ties & (idx <= cut))

    tri = tri_ref[...]
    nsc = S // T
    lane_sc = lax.broadcasted_iota(jnp.int32, (E, nsc), 1)
    cnt = jnp.zeros((E, nsc), F32)
    for sc in range(nsc):
        run = jnp.zeros((E, 1), F32)
        for ch in range(T // CH):
            lo = sc * T + ch * CH
            selc = jnp.where(sel[:, lo:lo + CH], 1.0, 0.0)
            incl = jnp.dot(selc.astype(BF16), tri, preferred_element_type=F32)
            rel = jnp.where(selc > 0.0, incl - 1.0 + run, -1.0)
            rel_ref[0, :, lo:lo + CH] = rel
            run = run + incl[:, CH - 1:CH]
        cnt = jnp.where(lane_sc == sc, run, cnt)
    cnt_ref[0] = cnt


def _topk(aff, T):
    B, E, S = aff.shape
    cap = EC_FACTOR * S // N_EXPERTS
    CH = min(CUMSUM_CHUNK, T)
    r = lax.broadcasted_iota(jnp.int32, (CH, CH), 0)
    c = lax.broadcasted_iota(jnp.int32, (CH, CH), 1)
    tri = jnp.where(r <= c, 1.0, 0.0).astype(BF16)
    nsc = S // T
    return pl.pallas_call(
        functools.partial(_topk_kernel, S=S, E=E, cap=cap, T=T, CH=CH),
        grid=(B,),
        in_specs=[pl.BlockSpec((1, E, S), lambda b: (b, 0, 0)),
                  pl.BlockSpec((CH, CH), lambda b: (0, 0))],
        out_specs=[pl.BlockSpec((1, E, S), lambda b: (b, 0, 0)),
                   pl.BlockSpec((1, E, nsc), lambda b: (b, 0, 0))],
        out_shape=[jax.ShapeDtypeStruct((B, E, S), F32), jax.ShapeDtypeStruct((B, E, nsc), F32)],
        compiler_params=_cparams(("parallel",)),
        name="topk_select",
    )(aff, tri)


def _moe_kernel(nblk_ref, xb_ref, relt_ref, relc_ref, affc_ref, wg_ref, wu_ref, wd_ref, f_ref, y_sc,
                *, T, RB, E, nsc, TS):
    b, sc, e = pl.program_id(0), pl.program_id(1), pl.program_id(2)

    @pl.when(e == 0)
    def _():
        f_ref[...] = jnp.zeros(f_ref.shape, F32)

    nb = nblk_ref[(b * nsc + sc) * E + e]
    relt = relt_ref[0, 0]

    def ffn_block(j, carry):
        slot = (j * RB + lax.broadcasted_iota(jnp.int32, (RB, 1), 0)).astype(F32)
        onehot = jnp.where(relt == slot, 1.0, 0.0).astype(BF16)
        xg = jnp.dot(onehot, xb_ref[0], preferred_element_type=F32).astype(BF16)
        hg = jnp.dot(xg, wg_ref[0], preferred_element_type=F32)
        hu = jnp.dot(xg, wu_ref[0], preferred_element_type=F32)
        h = (hg / (1.0 + jnp.exp(-hg)) * hu).astype(BF16)
        y = jnp.dot(h, wd_ref[0], preferred_element_type=F32)
        y_sc[pl.ds(pl.multiple_of(j * RB, RB), RB), :] = y.astype(BF16)
        return carry

    lax.fori_loop(0, nb, ffn_block, 0)

    @pl.when(nb % 2 == 1)
    def _():
        y_sc[pl.ds(pl.multiple_of(nb * RB, RB), RB), :] = jnp.zeros((RB, y_sc.shape[1]), BF16)

    lane_e = lax.broadcasted_iota(jnp.int32, (1, E), 1) == e

    def scatter_block(kg, carry):
        koff = pl.multiple_of(kg * 2 * RB, 2 * RB)
        slot = (koff + lax.broadcasted_iota(jnp.int32, (1, 2 * RB), 1)).astype(F32)
        yk = y_sc[pl.ds(koff, 2 * RB), :]
        for ts in range(T // TS):
            rows = slice(ts * TS, (ts + 1) * TS)
            relc = jnp.sum(jnp.where(lane_e, relc_ref[0, rows, :], 0.0), axis=1, keepdims=True)
            gate = jnp.sum(jnp.where(lane_e, affc_ref[0, rows, :], 0.0), axis=1, keepdims=True)
            onehot_t = jnp.where(relc == slot, 1.0, 0.0).astype(BF16)
            f_ref[0, rows, :] += gate * jnp.dot(onehot_t, yk, preferred_element_type=F32)
        return carry

    lax.fori_loop(0, (nb + 1) // 2, scatter_block, 0)


def _moe(xb, relt, relc, affc, nblk, wg, wu, wd, T):
    B, S, D = xb.shape
    E, _, F = wg.shape
    nsc = S // T
    RB = MOE_SLOTS
    cap = EC_FACTOR * S // N_EXPERTS
    max_blocks = -(-min(cap, T) // RB)
    y_rows = (max_blocks + 1) * RB
    TS = min(512, T)
    return pl.pallas_call(
        functools.partial(_moe_kernel, T=T, RB=RB, E=E, nsc=nsc, TS=TS),
        grid_spec=pltpu.PrefetchScalarGridSpec(
            num_scalar_prefetch=1,
            grid=(B, nsc, E),
            in_specs=[pl.BlockSpec((1, T, D), lambda b, s, e, n: (b, s, 0)),
                      pl.BlockSpec((1, 1, 1, T), lambda b, s, e, n: (b, e, 0, s)),
                      pl.BlockSpec((1, T, E), lambda b, s, e, n: (b, s, 0)),
                      pl.BlockSpec((1, T, E), lambda b, s, e, n: (b, s, 0)),
                      pl.BlockSpec((1, D, F), lambda b, s, e, n: (e, 0, 0)),
                      pl.BlockSpec((1, D, F), lambda b, s, e, n: (e, 0, 0)),
                      pl.BlockSpec((1, F, D), lambda b, s, e, n: (e, 0, 0))],
            out_specs=pl.BlockSpec((1, T, D), lambda b, s, e, n: (b, s, 0)),
            scratch_shapes=[pltpu.VMEM((y_rows, D), BF16)]),
        out_shape=jax.ShapeDtypeStruct((B, S, D), F32),
        compiler_params=_cparams(("parallel", "parallel", "arbitrary")),
        name="moe_ffn",
    )(nblk, xb, relt, relc, affc, wg, wu, wd)


def _add_ln_kernel(x_ref, f_ref, g_ref, b_ref, o_ref):
    o_ref[0] = _layer_norm(ALPHA * x_ref[0] + f_ref[0], g_ref[...], b_ref[...])


def _add_ln(x, f, g, b):
    B, S, D = x.shape
    tm = min(PROJ_ROWS, S)
    row = lambda bb, i: (bb, i, 0)
    const = lambda bb, i: (0, 0)
    return pl.pallas_call(
        _add_ln_kernel,
        grid=(B, S // tm),
        in_specs=[pl.BlockSpec((1, tm, D), row), pl.BlockSpec((1, tm, D), row),
                  pl.BlockSpec((1, D), const), pl.BlockSpec((1, D), const)],
        out_specs=pl.BlockSpec((1, tm, D), row),
        out_shape=jax.ShapeDtypeStruct((B, S, D), F32),
        compiler_params=_cparams(("parallel", "parallel")),
        name="add_ln",
    )(x, f, g, b)


def _proj_mla_kernel(x_ref, wd_ref, gq_ref, gkv_ref, wq_ref, wk_ref, wv_ref,
                     cq_ref, slq_ref, shq_ref, ck_ref, slk_ref, shk_ref, qt_ref, k_ref, vt_ref):
    xb = x_ref[0].astype(BF16)
    d = jnp.dot(xb, wd_ref[...], preferred_element_type=F32)
    cq = _rms(d[:, :Q_LORA], gq_ref[...], Q_LORA).astype(BF16)
    ckv = _rms(d[:, Q_LORA:Q_LORA + KV_LORA], gkv_ref[...], KV_LORA).astype(BF16)
    k_rope = _rope(d[:, Q_LORA + KV_LORA:], ck_ref[...], slk_ref[...], shk_ref[...], QK_ROPE // 2)
    cqt, slq, shq = cq_ref[...], slq_ref[...], shq_ref[...]
    for c in range(MLA_HEADS // 2):
        cols = slice(c * 2 * LANES, (c + 1) * 2 * LANES)
        q2 = jnp.dot(cq, wq_ref[:, cols], preferred_element_type=F32)
        k2 = jnp.dot(ckv, wk_ref[:, cols], preferred_element_type=F32)
        v2 = jnp.dot(ckv, wv_ref[:, cols], preferred_element_type=F32)
        for half in range(2):
            h = 2 * c + half
            lanes = slice(half * LANES, (half + 1) * LANES)
            qt_ref[0, h] = _rope(q2[:, lanes], cqt, slq, shq, QK_ROPE // 2).T.astype(BF16)
            k_ref[0, h] = (k2[:, lanes] + k_rope).astype(BF16)
            vt_ref[0, h] = _with_ones_lane(v2[:, lanes], V_DIM).T.astype(BF16)


def _proj_mla(x, wd_p, gq, gkv, wq_p, wk_p, wv_p, tabs_q, tabs_k):
    B, S, D = x.shape
    tm = min(PROJ_ROWS, S)
    H = MLA_HEADS
    const = lambda b, i: (0, 0)
    tab = pl.BlockSpec((tm, LANES), lambda b, i: (i, 0))
    full = lambda a: pl.BlockSpec(a.shape, const)
    return pl.pallas_call(
        _proj_mla_kernel,
        grid=(B, S // tm),
        in_specs=[pl.BlockSpec((1, tm, D), lambda b, i: (b, i, 0)),
                  full(wd_p), full(gq), full(gkv), full(wq_p), full(wk_p), full(wv_p),
                  tab, tab, tab, tab, tab, tab],
        out_specs=[pl.BlockSpec((1, H, LANES, tm), lambda b, i: (b, 0, 0, i)),
                   pl.BlockSpec((1, H, tm, LANES), lambda b, i: (b, 0, i, 0)),
                   pl.BlockSpec((1, H, LANES, tm), lambda b, i: (b, 0, 0, i))],
        out_shape=[jax.ShapeDtypeStruct((B, H, LANES, S), BF16),
                   jax.ShapeDtypeStruct((B, H, S, LANES), BF16),
                   jax.ShapeDtypeStruct((B, H, LANES, S), BF16)],
        compiler_params=_cparams(("parallel", "parallel")),
        name="proj_mla",
    )(x, wd_p, gq, gkv, wq_p, wk_p, wv_p, *tabs_q, *tabs_k)


def _pad_lanes(a, width=LANES):
    return jnp.pad(a, [(0, 0)] * (a.ndim - 1) + [(0, width - a.shape[-1])])


def _head_cols(w, n_heads, dim):
    return _pad_lanes(w.reshape(w.shape[0], n_heads, dim)).reshape(w.shape[0], n_heads * LANES)


def _angles(pos, dim):
    freqs = ROPE_THETA ** (-(jnp.arange(0, dim, 2, dtype=F32) / dim))
    return pos[:, None] * freqs[None, :]


def _place(S, pieces):
    parts, lane = [], 0
    for off, val in pieces:
        if off > lane:
            parts.append(jnp.zeros((S, off - lane), F32))
        parts.append(val)
        lane = off + val.shape[1]
    if lane < LANES:
        parts.append(jnp.zeros((S, LANES - lane), F32))
    return jnp.concatenate(parts, axis=1)


def _tables(S):
    t = jnp.arange(S)
    ar = _angles((t // GRID_W).astype(F32), HEAD_DIM // 2)
    ac = _angles((t % GRID_W).astype(F32), HEAD_DIM // 2)
    q4 = HEAD_DIM // 4
    tabs_a = (_place(S, [(0, jnp.cos(ar)), (q4, jnp.cos(ar)), (2 * q4, jnp.cos(ac)), (3 * q4, jnp.cos(ac))]),
              _place(S, [(0, -jnp.sin(ar)), (2 * q4, -jnp.sin(ac))]),
              _place(S, [(q4, jnp.sin(ar)), (3 * q4, jnp.sin(ac))]))
    asq = _angles(t.astype(F32), HEAD_DIM)
    h2 = HEAD_DIM // 2
    tabs_b = (_place(S, [(0, jnp.cos(asq)), (h2, jnp.cos(asq))]),
              _place(S, [(0, -jnp.sin(asq))]),
              _place(S, [(h2, jnp.sin(asq))]))
    am = _angles(t.astype(F32), QK_ROPE)
    r2 = QK_ROPE // 2
    ones = jnp.ones((S, QK_NOPE), F32)
    tabs_mk = (_place(S, [(0, ones), (QK_NOPE, jnp.cos(am)), (QK_NOPE + r2, jnp.cos(am))]),
               _place(S, [(QK_NOPE, -jnp.sin(am))]),
               _place(S, [(QK_NOPE + r2, jnp.sin(am))]))
    scale = (QK_NOPE + QK_ROPE) ** -0.5 * LOG2E
    tabs_mq = tuple(tb * scale for tb in tabs_mk)
    return tabs_a, tabs_b, tabs_mq, tabs_mk


def _split_bf16(w):
    hi = w.astype(BF16)
    return hi, (w - hi.astype(F32)).astype(BF16)


def _moe_layer(x1, x1b, aff, w_gate, w_up, w_down, g, b):
    B, S, D = x1.shape
    T = min(MOE_TOKENS, S)
    rel, cnt = _topk(aff, T)
    nblk = ((cnt.astype(jnp.int32) + MOE_SLOTS - 1) // MOE_SLOTS).transpose(0, 2, 1).reshape(-1)
    relt = rel.reshape(B, N_EXPERTS, 1, S)
    relc = rel.transpose(0, 2, 1)
    affc = aff.transpose(0, 2, 1)
    f = _moe(x1b, relt, relc, affc, nblk, w_gate.astype(BF16), w_up.astype(BF16), w_down.astype(BF16), T)
    return _add_ln(x1, f, g, b)


def kernel(x, ab_w_in, ab_q_norm, ab_k_norm, ab_sink, ab_w_out, mla_w_down, mla_q_norm, mla_kv_norm,
           mla_w_uq, mla_w_ukv, mla_w_out, ln_mix_g, ln_mix_b, moe_router, moe_w_gate, moe_w_up, moe_w_down,
           ln_ffn_g, ln_ffn_b):
    B, S, D = x.shape
    tabs_a, tabs_b, tabs_mq, tabs_mk = _tables(S)
    row = lambda v: v.reshape(1, -1)

    n_ab = A_HEADS + 2 * A_KV + B_HEADS + 2 * B_KV
    w_in_p = _head_cols(ab_w_in[0], n_ab, HEAD_DIM).astype(BF16)
    qta, ka, vta, qtb, kb, vtb = _proj_ab(x, w_in_p, _pad_lanes(row(ab_q_norm[0])), _pad_lanes(row(ab_k_norm[0])),
                                        tabs_a, tabs_b)
    oa = _flash(qta, ka, vta, HEAD_DIM)
    ob = _window(qtb, kb, vtb, ab_sink[0], HEAD_DIM)
    n_a = A_HEADS * HEAD_DIM
    w_oa = ab_w_out[0][:n_a].astype(BF16)
    w_ob = ab_w_out[0][n_a:].astype(BF16)
    wr_hi, wr_lo = _split_bf16(moe_router[0].T)
    x1, x1b, aff = _out_ln_router([oa, ob], [w_oa, w_ob], x, row(ln_mix_g[0]), row(ln_mix_b[0]), wr_hi, wr_lo)
    x = _moe_layer(x1, x1b, aff, moe_w_gate[0], moe_w_up[0], moe_w_down[0], row(ln_ffn_g[0]), row(ln_ffn_b[0]))

    wd = mla_w_down[0]
    wd_p = jnp.concatenate(
        [wd[:, :Q_LORA + KV_LORA],
         jnp.zeros((D, QK_NOPE), F32), wd[:, Q_LORA + KV_LORA:], jnp.zeros((D, LANES - QK_NOPE - QK_ROPE), F32)],
        axis=1).astype(BF16)
    wq_p = _head_cols(mla_w_uq[0], MLA_HEADS, QK_NOPE + QK_ROPE).astype(BF16)
    wkv = mla_w_ukv[0].reshape(KV_LORA, MLA_HEADS, QK_NOPE + V_DIM)
    wk_p = _head_cols(wkv[:, :, :QK_NOPE].reshape(KV_LORA, -1), MLA_HEADS, QK_NOPE).astype(BF16)
    wv_p = _head_cols(wkv[:, :, QK_NOPE:].reshape(KV_LORA, -1), MLA_HEADS, V_DIM).astype(BF16)
    qt, k, vt = _proj_mla(x, wd_p, row(mla_q_norm[0]), row(mla_kv_norm[0]), wq_p, wk_p, wv_p, tabs_mq, tabs_mk)
    oc = _flash(qt, k, vt, V_DIM)
    w_oc = mla_w_out[0].astype(BF16)
    wr_hi, wr_lo = _split_bf16(moe_router[1].T)
    x1, x1b, aff = _out_ln_router([oc], [w_oc], x, row(ln_mix_g[1]), row(ln_mix_b[1]), wr_hi, wr_lo)
    x = _moe_layer(x1, x1b, aff, moe_w_gate[1], moe_w_up[1], moe_w_down[1], row(ln_ffn_g[1]), row(ln_ffn_b[1]))
    return x
```

```python
import functools

import jax
import jax.numpy as jnp
from jax import lax
from jax.experimental import pallas as pl
from jax.experimental.pallas import tpu as pltpu

F32 = jnp.float32
BF16 = jnp.bfloat16

GRID_W = 64
ROPE_THETA = 10000.0
HEAD_DIM = 64
A_HEADS, A_KV = 8, 2
B_HEADS, B_KV = 8, 2
WINDOW = 128
MLA_HEADS = 16
Q_LORA, KV_LORA = 256, 128
QK_NOPE, QK_ROPE, V_DIM = 64, 32, 64
N_EXPERTS = 16
EC_FACTOR = 2
DEPTH = 2
ALPHA = (2.0 * DEPTH) ** 0.25
NEG_INF = -1e30
SHIFT_SLACK = 64.0
RMS_EPS = 1e-6
LOG2E = 1.4426950408889634
LN_EPS = 1e-5

LANES = 128
VMEM_LIMIT = 56 * 1024 * 1024

PROJ_ROWS = 512
ATTN_ROWS = 4096
ATTN_KEYS = 2048
ATTN_KEY_SUB = 512
ATTN_COLS = 256
AHEAD = 2
WIN_Q = 1024
MOE_TOKENS = 1024
MOE_SLOTS = 128
CUMSUM_CHUNK = 256


def _cparams(sem):
    return pltpu.CompilerParams(dimension_semantics=sem, vmem_limit_bytes=VMEM_LIMIT)


def _rope(x, c, s_lo, s_hi, shift):
    return x * c + pltpu.roll(x, LANES - shift, 1) * s_lo + pltpu.roll(x, shift, 1) * s_hi


def _rms(x, g, n):
    ms = jnp.sum(x * x, axis=1, keepdims=True) * (1.0 / n)
    return x * lax.rsqrt(ms + RMS_EPS) * g


def _layer_norm(y, g, b):
    mu = jnp.mean(y, axis=1, keepdims=True)
    d = y - mu
    var = jnp.mean(d * d, axis=1, keepdims=True)
    return d * lax.rsqrt(var + LN_EPS) * g + b


def _with_ones_lane(v, n):
    lane = lax.broadcasted_iota(jnp.int32, (1, LANES), 1)
    return jnp.where(lane == n, 1.0, v)


def _pack_head_pair(a, b, vd):
    lane = lax.broadcasted_iota(jnp.int32, (1, LANES), 1)
    return jnp.where(lane < vd, a, pltpu.roll(b, vd, 1))


def _proj_ab_kernel(x_ref, w_ref, gq_ref, gk_ref, ca_ref, sla_ref, sha_ref, cb_ref, slb_ref, shb_ref,
                    qa_ref, ka_ref, va_ref, qb_ref, kb_ref, vb_ref):
    xb = x_ref[0].astype(BF16)
    scale = HEAD_DIM ** -0.5 * LOG2E
    ca, sla, sha = ca_ref[...], sla_ref[...], sha_ref[...]
    cb, slb, shb = cb_ref[...], slb_ref[...], shb_ref[...]
    n_groups = A_HEADS + 2 * A_KV + B_HEADS + 2 * B_KV
    for c in range(n_groups // 2):
        pr = jnp.dot(xb, w_ref[:, c * 2 * LANES:(c + 1) * 2 * LANES], preferred_element_type=F32)
        for half in range(2):
            g = 2 * c + half
            ph = pr[:, half * LANES:(half + 1) * LANES]
            if g < A_HEADS:
                q = _rope(_rms(ph, gq_ref[...], HEAD_DIM), ca, sla, sha, HEAD_DIM // 4) * scale
                qa_ref[0, g] = q.T.astype(BF16)
            elif g < A_HEADS + A_KV:
                k = _rope(_rms(ph, gk_ref[...], HEAD_DIM), ca, sla, sha, HEAD_DIM // 4)
                ka_ref[0, g - A_HEADS] = k.astype(BF16)
            elif g < A_HEADS + 2 * A_KV:
                va_ref[0, g - A_HEADS - A_KV] = _with_ones_lane(ph, HEAD_DIM).T.astype(BF16)
            elif g < A_HEADS + 2 * A_KV + B_HEADS:
                q = _rope(ph, cb, slb, shb, HEAD_DIM // 2) * scale
                qb_ref[0, g - A_HEADS - 2 * A_KV] = q.T.astype(BF16)
            elif g < A_HEADS + 2 * A_KV + B_HEADS + B_KV:
                k = _rope(ph, cb, slb, shb, HEAD_DIM // 2)
                kb_ref[0, g - A_HEADS - 2 * A_KV - B_HEADS] = k.astype(BF16)
            else:
                vb_ref[0, g - A_HEADS - 2 * A_KV - B_HEADS - B_KV] = _with_ones_lane(ph, HEAD_DIM).T.astype(BF16)


def _proj_ab(x, w_p, gq, gk, tabs_a, tabs_b):
    B, S, D = x.shape
    tm = min(PROJ_ROWS, S)
    ns = S // tm
    row = lambda b, i: (b, i, 0)
    tab = pl.BlockSpec((tm, LANES), lambda b, i: (i, 0))
    vec = pl.BlockSpec((1, LANES), lambda b, i: (0, 0))
    hm = lambda h: pl.BlockSpec((1, h, tm, LANES), lambda b, i: (b, 0, i, 0))
    tr = lambda h: pl.BlockSpec((1, h, LANES, tm), lambda b, i: (b, 0, 0, i))
    sd = lambda h: jax.ShapeDtypeStruct((B, h, S, LANES), BF16)
    sdt = lambda h: jax.ShapeDtypeStruct((B, h, LANES, S), BF16)
    return pl.pallas_call(
        _proj_ab_kernel,
        grid=(B, ns),
        in_specs=[pl.BlockSpec((1, tm, D), row),
                  pl.BlockSpec(w_p.shape, lambda b, i: (0, 0)),
                  vec, vec, tab, tab, tab, tab, tab, tab],
        out_specs=[tr(A_HEADS), hm(A_KV), tr(A_KV), tr(B_HEADS), hm(B_KV), tr(B_KV)],
        out_shape=[sdt(A_HEADS), sd(A_KV), sdt(A_KV), sdt(B_HEADS), sd(B_KV), sdt(B_KV)],
        compiler_params=_cparams(("parallel", "parallel")),
        name="proj_ab",
    )(x, w_p, gq, gk, *tabs_a, *tabs_b)


def _flash_kernel(qt_ref, k_ref, vt_ref, o_ref, m_sc, acc_sc, pv_sc, *, NKV, G, tq, tk, ks, nk, cw, vd):
    heads = NKV * G
    M = heads * tq
    n_sub = M // cw

    def q_tile(c):
        h, j = divmod(c * cw, tq)
        return qt_ref[0, h, :, j:j + cw]

    def kv_of(c):
        return (c * cw // tq) // G

    def cols(c):
        return slice(c * cw, (c + 1) * cw)

    for c in range(n_sub):
        s0 = jnp.dot(k_ref[0, kv_of(c), 0:LANES, :], q_tile(c), preferred_element_type=F32)
        m_sc[:, cols(c)] = jnp.max(s0, axis=0, keepdims=True)
    acc_sc[...] = jnp.zeros((LANES, M), F32)

    def body(kb, carry):
        off = pl.multiple_of(kb * tk, tk)

        def keys(c, s):
            return k_ref[0, kv_of(c), pl.ds(pl.multiple_of(off + s * ks, ks), ks), :]

        def values_t(c, s=None):
            if s is None:
                return vt_ref[0, kv_of(c), :, pl.ds(off, tk)]
            return vt_ref[0, kv_of(c), :, pl.ds(pl.multiple_of(off + s * ks, ks), ks)]

        tiles = [(c, s) for c in range(n_sub) for s in range(tk // ks)]

        def scores(t):
            c, s = tiles[t]
            return jnp.dot(keys(c, s), q_tile(c), preferred_element_type=F32)

        excess = None
        ahead = {t: scores(t) for t in range(min(AHEAD, len(tiles)))}
        pts = []
        for t, (c, s) in enumerate(tiles):
            st = ahead.pop(t)
            if t + AHEAD < len(tiles):
                ahead[t + AHEAD] = scores(t + AHEAD)
            m = m_sc[:, cols(c)]
            over = jnp.max(st, axis=0, keepdims=True) - m
            excess = over if excess is None else jnp.maximum(excess, over)
            pts.append(jnp.exp2(st - m).astype(BF16))
            if s == tk // ks - 1:
                pt = pts[0] if len(pts) == 1 else jnp.concatenate(pts, axis=0)
                pv_sc[:, cols(c)] = jnp.dot(values_t(c), pt, preferred_element_type=F32)
                pts = []
        renew = jnp.max(excess) > SHIFT_SLACK

        @pl.when(jnp.logical_not(renew))
        def _():
            acc_sc[...] += pv_sc[...]

        @pl.when(renew)
        def _():
            for c, s in tiles:
                st = jnp.dot(keys(c, s), q_tile(c), preferred_element_type=F32)
                m_prev = m_sc[:, cols(c)]
                m_new = jnp.maximum(m_prev, jnp.max(st, axis=0, keepdims=True))
                alpha = jnp.exp2(m_prev - m_new)
                pt = jnp.exp2(st - m_new).astype(BF16)
                acc_sc[:, cols(c)] = (acc_sc[:, cols(c)] * alpha
                                      + jnp.dot(values_t(c, s), pt, preferred_element_type=F32))
                m_sc[:, cols(c)] = m_new

        return carry

    lax.fori_loop(0, nk, body, 0)
    acc = acc_sc[...]
    o = acc / acc[vd:vd + 1, :]
    for p in range(heads // 2):
        o_ref[0, :, p * LANES:(p + 1) * LANES] = _pack_head_pair(
            o[:, 2 * p * tq:(2 * p + 1) * tq].T, o[:, (2 * p + 1) * tq:(2 * p + 2) * tq].T, vd).astype(BF16)


def _flash(qt, k, vt, vd):
    B, H, _, S = qt.shape
    HK = k.shape[1]
    G = H // HK
    NKV = max(1, 2 // G)
    heads = NKV * G
    tq = min(ATTN_ROWS // heads, S)
    tk = min(ATTN_KEYS, S)
    M = heads * tq
    cw = min(ATTN_COLS, tq)
    ks = min(ATTN_KEY_SUB, tk)
    return pl.pallas_call(
        functools.partial(_flash_kernel, NKV=NKV, G=G, tq=tq, tk=tk, ks=ks, nk=S // tk, cw=cw, vd=vd),
        grid=(B, HK // NKV, S // tq),
        in_specs=[pl.BlockSpec((1, heads, LANES, tq), lambda b, h, i: (b, h, 0, i)),
                  pl.BlockSpec((1, NKV, S, LANES), lambda b, h, i: (b, h, 0, 0)),
                  pl.BlockSpec((1, NKV, LANES, S), lambda b, h, i: (b, h, 0, 0))],
        out_specs=pl.BlockSpec((1, tq, heads * vd), lambda b, h, i: (b, i, h)),
        out_shape=jax.ShapeDtypeStruct((B, S, H * vd), BF16),
        scratch_shapes=[pltpu.VMEM((1, M), F32), pltpu.VMEM((LANES, M), F32), pltpu.VMEM((LANES, M), F32)],
        compiler_params=_cparams(("parallel", "parallel", "parallel")),
        name="flash_attn",
    )(qt, k, vt)


def _window_kernel(sink_ref, qt_ref, k_ref, vt_ref, o_ref, *, G, tq, span, S, vd):
    kvh = pl.program_id(1)
    start = pl.program_id(2) * tq
    for j in range(tq // LANES):
        q0 = start + j * LANES
        kstart = pl.multiple_of(jnp.clip(q0 - WINDOW, 0, S - span), LANES)
        k = k_ref[0, 0, pl.ds(kstart, span), :]
        vt = vt_ref[0, 0, :, pl.ds(kstart, span)]
        kpos = kstart + lax.broadcasted_iota(jnp.int32, (span, 1), 0)
        qpos = q0 + lax.broadcasted_iota(jnp.int32, (1, LANES), 1)
        valid = jnp.abs(qpos - kpos) <= WINDOW
        outs = []
        for g in range(G):
            st = jnp.dot(k, qt_ref[0, g, :, j * LANES:(j + 1) * LANES], preferred_element_type=F32)
            st = jnp.where(valid, st, NEG_INF)
            sink = sink_ref[kvh * G + g] * LOG2E
            m = jnp.maximum(jnp.max(st, axis=0, keepdims=True), sink)
            pt = jnp.exp2(st - m).astype(BF16)
            acc = jnp.dot(vt, pt, preferred_element_type=F32)
            outs.append((acc / (acc[vd:vd + 1, :] + jnp.exp2(sink - m))).T)
        for p in range(G // 2):
            o_ref[0, j * LANES:(j + 1) * LANES, p * LANES:(p + 1) * LANES] = _pack_head_pair(
                outs[2 * p], outs[2 * p + 1], vd).astype(BF16)


def _window(qt, k, vt, sink, vd):
    B, H, _, S = qt.shape
    HK = k.shape[1]
    G = H // HK
    tq = min(WIN_Q, S)
    span = min(LANES + 2 * WINDOW, S)
    return pl.pallas_call(
        functools.partial(_window_kernel, G=G, tq=tq, span=span, S=S, vd=vd),
        grid_spec=pltpu.PrefetchScalarGridSpec(
            num_scalar_prefetch=1,
            grid=(B, HK, S // tq),
            in_specs=[pl.BlockSpec((1, G, LANES, tq), lambda b, h, i, sk: (b, h, 0, i)),
                      pl.BlockSpec((1, 1, S, LANES), lambda b, h, i, sk: (b, h, 0, 0)),
                      pl.BlockSpec((1, 1, LANES, S), lambda b, h, i, sk: (b, h, 0, 0))],
            out_specs=pl.BlockSpec((1, tq, G * vd), lambda b, h, i, sk: (b, i, h))),
        out_shape=jax.ShapeDtypeStruct((B, S, H * vd), BF16),
        compiler_params=_cparams(("parallel", "parallel", "parallel")),
        name="window_attn",
    )(sink, qt, k, vt)


def _out_ln_router_kernel(*refs, n_in):
    o_refs = refs[:n_in]
    w_refs = refs[n_in:2 * n_in]
    x_ref, g_ref, b_ref, wrh_ref, wrl_ref, x1_ref, x1b_ref, aff_ref = refs[2 * n_in:]
    h = jnp.dot(o_refs[0][0], w_refs[0][...], preferred_element_type=F32)
    for i in range(1, n_in):
        h = h + jnp.dot(o_refs[i][0], w_refs[i][...], preferred_element_type=F32)
    x1 = _layer_norm(ALPHA * x_ref[0] + h, g_ref[...], b_ref[...])
    x1_ref[0] = x1
    x_hi = x1.astype(BF16)
    x1b_ref[0] = x_hi
    x_lo = (x1 - x_hi.astype(F32)).astype(BF16)
    nt = (((1,), (1,)), ((), ()))
    logits = (lax.dot_general(wrh_ref[...], x_hi, nt, preferred_element_type=F32)
              + lax.dot_general(wrh_ref[...], x_lo, nt, preferred_element_type=F32)
              + lax.dot_general(wrl_ref[...], x_hi, nt, preferred_element_type=F32))
    z = jnp.exp(logits - jnp.max(logits, axis=0, keepdims=True))
    aff_ref[0] = z / jnp.sum(z, axis=0, keepdims=True)


def _out_ln_router(os_, ws_, x, g, b, wr_hi, wr_lo):
    B, S, D = x.shape
    E = wr_hi.shape[0]
    tm = min(PROJ_ROWS, S)
    n_in = len(os_)
    row = lambda bb, i: (bb, i, 0)
    const = lambda bb, i: (0, 0)
    in_specs = ([pl.BlockSpec((1, tm, o.shape[2]), row) for o in os_]
                + [pl.BlockSpec(w.shape, const) for w in ws_]
                + [pl.BlockSpec((1, tm, D), row), pl.BlockSpec((1, D), const), pl.BlockSpec((1, D), const),
                   pl.BlockSpec((E, D), const), pl.BlockSpec((E, D), const)])
    return pl.pallas_call(
        functools.partial(_out_ln_router_kernel, n_in=n_in),
        grid=(B, S // tm),
        in_specs=in_specs,
        out_specs=[pl.BlockSpec((1, tm, D), row), pl.BlockSpec((1, tm, D), row),
                   pl.BlockSpec((1, E, tm), lambda bb, i: (bb, 0, i))],
        out_shape=[jax.ShapeDtypeStruct((B, S, D), F32), jax.ShapeDtypeStruct((B, S, D), BF16),
                   jax.ShapeDtypeStruct((B, E, S), F32)],
        compiler_params=_cparams(("parallel", "parallel")),
        name="out_ln_router",
    )(*os_, *ws_, x, g, b, wr_hi, wr_lo)


def _topk_kernel(aff_ref, tri_ref, rel_ref, cnt_ref, *, S, E, cap, T, CH):
    aff = aff_ref[0]
    bits = pltpu.bitcast(aff, jnp.int32)
    capf = jnp.float32(cap)

    def count(mask):
        return jnp.sum(jnp.where(mask, 1.0, 0.0), axis=1, keepdims=True)

    def thr_body(i, t):
        cand = t | jnp.left_shift(jnp.int32(1), 30 - i)
        return jnp.where(count(bits >= cand) >= capf, cand, t)

    thr = lax.fori_loop(0, 31, thr_body, jnp.zeros((E, 1), jnp.int32))
    gt = bits > thr
    ties = bits == thr
    need = capf - count(gt)
    idx = lax.broadcasted_iota(jnp.int32, (E, S), 1)
    nbits = max(1, (S - 1).bit_length())

    def cut_body(i, c):
        cand = c | jnp.left_shift(jnp.int32(1), nbits - 1 - i)
        return jnp.where(count(ties & (idx < cand)) < need, cand, c)

    cut = lax.fori_loop(0, nbits, cut_body, jnp.zeros((E, 1), jnp.int32))
    sel = gt | (ties & (idx <= cut))

    tri = tri_ref[...]
    nsc = S // T
    lane_sc = lax.broadcasted_iota(jnp.int32, (E, nsc), 1)
    cnt = jnp.zeros((E, nsc), F32)
    for sc in range(nsc):
        run = jnp.zeros((E, 1), F32)
        for ch in range(T // CH):
            lo = sc * T + ch * CH
            selc = jnp.where(sel[:, lo:lo + CH], 1.0, 0.0)
            incl = jnp.dot(selc.astype(BF16), tri, preferred_element_type=F32)
            rel = jnp.where(selc > 0.0, incl - 1.0 + run, -1.0)
            rel_ref[0, :, lo:lo + CH] = rel
            run = run + incl[:, CH - 1:CH]
        cnt = jnp.where(lane_sc == sc, run, cnt)
    cnt_ref[0] = cnt


def _topk(aff, T):
    B, E, S = aff.shape
    cap = EC_FACTOR * S // N_EXPERTS
    CH = min(CUMSUM_CHUNK, T)
    r = lax.broadcasted_iota(jnp.int32, (CH, CH), 0)
    c = lax.broadcasted_iota(jnp.int32, (CH, CH), 1)
    tri = jnp.where(r <= c, 1.0, 0.0).astype(BF16)
    nsc = S // T
    return pl.pallas_call(
        functools.partial(_topk_kernel, S=S, E=E, cap=cap, T=T, CH=CH),
        grid=(B,),
        in_specs=[pl.BlockSpec((1, E, S), lambda b: (b, 0, 0)),
                  pl.BlockSpec((CH, CH), lambda b: (0, 0))],
        out_specs=[pl.BlockSpec((1, E, S), lambda b: (b, 0, 0)),
                   pl.BlockSpec((1, E, nsc), lambda b: (b, 0, 0))],
        out_shape=[jax.ShapeDtypeStruct((B, E, S), F32), jax.ShapeDtypeStruct((B, E, nsc), F32)],
        compiler_params=_cparams(("parallel",)),
        name="topk_select",
    )(aff, tri)


def _moe_kernel(nblk_ref, xb_ref, relt_ref, relc_ref, affc_ref, wg_ref, wu_ref, wd_ref, x1_ref, g_ref, b_ref,
                f_ref, y_sc, *, T, RB, E, nsc, TS):
    b, sc, e = pl.program_id(0), pl.program_id(1), pl.program_id(2)

    @pl.when(e == 0)
    def _():
        f_ref[...] = jnp.zeros(f_ref.shape, F32)

    nb = nblk_ref[(b * nsc + sc) * E + e]
    relt = relt_ref[0, 0]

    def ffn_block(j, carry):
        slot = (j * RB + lax.broadcasted_iota(jnp.int32, (RB, 1), 0)).astype(F32)
        onehot = jnp.where(relt == slot, 1.0, 0.0).astype(BF16)
        xg = jnp.dot(onehot, xb_ref[0], preferred_element_type=F32).astype(BF16)
        hg = jnp.dot(xg, wg_ref[0], preferred_element_type=F32)
        hu = jnp.dot(xg, wu_ref[0], preferred_element_type=F32)
        h = (hg / (1.0 + jnp.exp(-hg)) * hu).astype(BF16)
        y = jnp.dot(h, wd_ref[0], preferred_element_type=F32)
        y_sc[pl.ds(pl.multiple_of(j * RB, RB), RB), :] = y.astype(BF16)
        return carry

    lax.fori_loop(0, nb, ffn_block, 0)

    @pl.when(nb % 2 == 1)
    def _():
        y_sc[pl.ds(pl.multiple_of(nb * RB, RB), RB), :] = jnp.zeros((RB, y_sc.shape[1]), BF16)

    lane_e = lax.broadcasted_iota(jnp.int32, (1, E), 1) == e
    row_tiles = [slice(ts * TS, (ts + 1) * TS) for ts in range(T // TS)]
    relc = [jnp.sum(jnp.where(lane_e, relc_ref[0, rows, :], 0.0), axis=1, keepdims=True) for rows in row_tiles]
    gate = [jnp.sum(jnp.where(lane_e, affc_ref[0, rows, :], 0.0), axis=1, keepdims=True) for rows in row_tiles]

    def scatter_block(kg, carry):
        koff = pl.multiple_of(kg * 2 * RB, 2 * RB)
        slot = (koff + lax.broadcasted_iota(jnp.int32, (1, 2 * RB), 1)).astype(F32)
        yk = y_sc[pl.ds(koff, 2 * RB), :]
        for ts, rows in enumerate(row_tiles):
            onehot_t = jnp.where(relc[ts] == slot, 1.0, 0.0).astype(BF16)
            f_ref[0, rows, :] += gate[ts] * jnp.dot(onehot_t, yk, preferred_element_type=F32)
        return carry

    lax.fori_loop(0, (nb + 1) // 2, scatter_block, 0)

    @pl.when(e == E - 1)
    def _():
        for rows in row_tiles:
            f_ref[0, rows, :] = _layer_norm(ALPHA * x1_ref[0, rows, :] + f_ref[0, rows, :], g_ref[...], b_ref[...])


def _moe(xb, relt, relc, affc, nblk, wg, wu, wd, x1, g, b, T):
    B, S, D = xb.shape
    E, _, F = wg.shape
    nsc = S // T
    RB = MOE_SLOTS
    cap = EC_FACTOR * S // N_EXPERTS
    max_blocks = -(-min(cap, T) // RB)
    y_rows = (max_blocks + 1) * RB
    TS = min(512, T)
    return pl.pallas_call(
        functools.partial(_moe_kernel, T=T, RB=RB, E=E, nsc=nsc, TS=TS),
        grid_spec=pltpu.PrefetchScalarGridSpec(
            num_scalar_prefetch=1,
            grid=(B, nsc, E),
            in_specs=[pl.BlockSpec((1, T, D), lambda b, s, e, n: (b, s, 0)),
                      pl.BlockSpec((1, 1, 1, T), lambda b, s, e, n: (b, e, 0, s)),
                      pl.BlockSpec((1, T, E), lambda b, s, e, n: (b, s, 0)),
                      pl.BlockSpec((1, T, E), lambda b, s, e, n: (b, s, 0)),
                      pl.BlockSpec((1, D, F), lambda b, s, e, n: (e, 0, 0)),
                      pl.BlockSpec((1, D, F), lambda b, s, e, n: (e, 0, 0)),
                      pl.BlockSpec((1, F, D), lambda b, s, e, n: (e, 0, 0)),
                      pl.BlockSpec((1, T, D), lambda b, s, e, n: (b, s, 0)),
                      pl.BlockSpec((1, D), lambda b, s, e, n: (0, 0)),
                      pl.BlockSpec((1, D), lambda b, s, e, n: (0, 0))],
            out_specs=pl.BlockSpec((1, T, D), lambda b, s, e, n: (b, s, 0)),
            scratch_shapes=[pltpu.VMEM((y_rows, D), BF16)]),
        out_shape=jax.ShapeDtypeStruct((B, S, D), F32),
        compiler_params=_cparams(("parallel", "parallel", "arbitrary")),
        name="moe_ffn",
    )(nblk, xb, relt, relc, affc, wg, wu, wd, x1, g, b)


def _proj_mla_kernel(x_ref, wd_ref, gq_ref, gkv_ref, wq_ref, wk_ref, wv_ref,
                     cq_ref, slq_ref, shq_ref, ck_ref, slk_ref, shk_ref, qt_ref, k_ref, vt_ref):
    xb = x_ref[0].astype(BF16)
    d = jnp.dot(xb, wd_ref[...], preferred_element_type=F32)
    cq = _rms(d[:, :Q_LORA], gq_ref[...], Q_LORA).astype(BF16)
    ckv = _rms(d[:, Q_LORA:Q_LORA + KV_LORA], gkv_ref[...], KV_LORA).astype(BF16)
    k_rope = _rope(d[:, Q_LORA + KV_LORA:], ck_ref[...], slk_ref[...], shk_ref[...], QK_ROPE // 2)
    cqt, slq, shq = cq_ref[...], slq_ref[...], shq_ref[...]
    for c in range(MLA_HEADS // 2):
        cols = slice(c * 2 * LANES, (c + 1) * 2 * LANES)
        q2 = jnp.dot(cq, wq_ref[:, cols], preferred_element_type=F32)
        k2 = jnp.dot(ckv, wk_ref[:, cols], preferred_element_type=F32)
        v2 = jnp.dot(ckv, wv_ref[:, cols], preferred_element_type=F32)
        for half in range(2):
            h = 2 * c + half
            lanes = slice(half * LANES, (half + 1) * LANES)
            qt_ref[0, h] = _rope(q2[:, lanes], cqt, slq, shq, QK_ROPE // 2).T.astype(BF16)
            k_ref[0, h] = (k2[:, lanes] + k_rope).astype(BF16)
            vt_ref[0, h] = _with_ones_lane(v2[:, lanes], V_DIM).T.astype(BF16)


def _proj_mla(x, wd_p, gq, gkv, wq_p, wk_p, wv_p, tabs_q, tabs_k):
    B, S, D = x.shape
    tm = min(PROJ_ROWS, S)
    H = MLA_HEADS
    const = lambda b, i: (0, 0)
    tab = pl.BlockSpec((tm, LANES), lambda b, i: (i, 0))
    full = lambda a: pl.BlockSpec(a.shape, const)
    return pl.pallas_call(
        _proj_mla_kernel,
        grid=(B, S // tm),
        in_specs=[pl.BlockSpec((1, tm, D), lambda b, i: (b, i, 0)),
                  full(wd_p), full(gq), full(gkv), full(wq_p), full(wk_p), full(wv_p),
                  tab, tab, tab, tab, tab, tab],
        out_specs=[pl.BlockSpec((1, H, LANES, tm), lambda b, i: (b, 0, 0, i)),
                   pl.BlockSpec((1, H, tm, LANES), lambda b, i: (b, 0, i, 0)),
                   pl.BlockSpec((1, H, LANES, tm), lambda b, i: (b, 0, 0, i))],
        out_shape=[jax.ShapeDtypeStruct((B, H, LANES, S), BF16),
                   jax.ShapeDtypeStruct((B, H, S, LANES), BF16),
                   jax.ShapeDtypeStruct((B, H, LANES, S), BF16)],
        compiler_params=_cparams(("parallel", "parallel")),
        name="proj_mla",
    )(x, wd_p, gq, gkv, wq_p, wk_p, wv_p, *tabs_q, *tabs_k)


def _pad_lanes(a, width=LANES):
    return jnp.pad(a, [(0, 0)] * (a.ndim - 1) + [(0, width - a.shape[-1])])


def _head_cols(w, n_heads, dim):
    return _pad_lanes(w.reshape(w.shape[0], n_heads, dim)).reshape(w.shape[0], n_heads * LANES)


def _angles(pos, dim):
    freqs = ROPE_THETA ** (-(jnp.arange(0, dim, 2, dtype=F32) / dim))
    return pos[:, None] * freqs[None, :]


def _place(S, pieces):
    parts, lane = [], 0
    for off, val in pieces:
        if off > lane:
            parts.append(jnp.zeros((S, off - lane), F32))
        parts.append(val)
        lane = off + val.shape[1]
    if lane < LANES:
        parts.append(jnp.zeros((S, LANES - lane), F32))
    return jnp.concatenate(parts, axis=1)


def _tables(S):
    t = jnp.arange(S)
    ar = _angles((t // GRID_W).astype(F32), HEAD_DIM // 2)
    ac = _angles((t % GRID_W).astype(F32), HEAD_DIM // 2)
    q4 = HEAD_DIM // 4
    tabs_a = (_place(S, [(0, jnp.cos(ar)), (q4, jnp.cos(ar)), (2 * q4, jnp.cos(ac)), (3 * q4, jnp.cos(ac))]),
              _place(S, [(0, -jnp.sin(ar)), (2 * q4, -jnp.sin(ac))]),
              _place(S, [(q4, jnp.sin(ar)), (3 * q4, jnp.sin(ac))]))
    asq = _angles(t.astype(F32), HEAD_DIM)
    h2 = HEAD_DIM // 2
    tabs_b = (_place(S, [(0, jnp.cos(asq)), (h2, jnp.cos(asq))]),
              _place(S, [(0, -jnp.sin(asq))]),
              _place(S, [(h2, jnp.sin(asq))]))
    am = _angles(t.astype(F32), QK_ROPE)
    r2 = QK_ROPE // 2
    ones = jnp.ones((S, QK_NOPE), F32)
    tabs_mk = (_place(S, [(0, ones), (QK_NOPE, jnp.cos(am)), (QK_NOPE + r2, jnp.cos(am))]),
               _place(S, [(QK_NOPE, -jnp.sin(am))]),
               _place(S, [(QK_NOPE + r2, jnp.sin(am))]))
    scale = (QK_NOPE + QK_ROPE) ** -0.5 * LOG2E
    tabs_mq = tuple(tb * scale for tb in tabs_mk)
    return tabs_a, tabs_b, tabs_mq, tabs_mk


def _split_bf16(w):
    hi = w.astype(BF16)
    return hi, (w - hi.astype(F32)).astype(BF16)


def _moe_layer(x1, x1b, aff, w_gate, w_up, w_down, g, b):
    B, S, D = x1.shape
    T = min(MOE_TOKENS, S)
    rel, cnt = _topk(aff, T)
    nblk = ((cnt.astype(jnp.int32) + MOE_SLOTS - 1) // MOE_SLOTS).transpose(0, 2, 1).reshape(-1)
    relt = rel.reshape(B, N_EXPERTS, 1, S)
    relc = rel.transpose(0, 2, 1)
    affc = aff.transpose(0, 2, 1)
    return _moe(x1b, relt, relc, affc, nblk, w_gate.astype(BF16), w_up.astype(BF16), w_down.astype(BF16), x1, g, b, T)


def kernel(x, ab_w_in, ab_q_norm, ab_k_norm, ab_sink, ab_w_out, mla_w_down, mla_q_norm, mla_kv_norm,
           mla_w_uq, mla_w_ukv, mla_w_out, ln_mix_g, ln_mix_b, moe_router, moe_w_gate, moe_w_up, moe_w_down,
           ln_ffn_g, ln_ffn_b):
    B, S, D = x.shape
    tabs_a, tabs_b, tabs_mq, tabs_mk = _tables(S)
    row = lambda v: v.reshape(1, -1)

    n_ab = A_HEADS + 2 * A_KV + B_HEADS + 2 * B_KV
    w_in_p = _head_cols(ab_w_in[0], n_ab, HEAD_DIM).astype(BF16)
    qta, ka, vta, qtb, kb, vtb = _proj_ab(x, w_in_p, _pad_lanes(row(ab_q_norm[0])), _pad_lanes(row(ab_k_norm[0])),
                                        tabs_a, tabs_b)
    oa = _flash(qta, ka, vta, HEAD_DIM)
    ob = _window(qtb, kb, vtb, ab_sink[0], HEAD_DIM)
    n_a = A_HEADS * HEAD_DIM
    w_oa = ab_w_out[0][:n_a].astype(BF16)
    w_ob = ab_w_out[0][n_a:].astype(BF16)
    wr_hi, wr_lo = _split_bf16(moe_router[0].T)
    x1, x1b, aff = _out_ln_router([oa, ob], [w_oa, w_ob], x, row(ln_mix_g[0]), row(ln_mix_b[0]), wr_hi, wr_lo)
    x = _moe_layer(x1, x1b, aff, moe_w_gate[0], moe_w_up[0], moe_w_down[0], row(ln_ffn_g[0]), row(ln_ffn_b[0]))

    wd = mla_w_down[0]
    wd_p = jnp.concatenate(
        [wd[:, :Q_LORA + KV_LORA],
         jnp.zeros((D, QK_NOPE), F32), wd[:, Q_LORA + KV_LORA:], jnp.zeros((D, LANES - QK_NOPE - QK_ROPE), F32)],
        axis=1).astype(BF16)
    wq_p = _head_cols(mla_w_uq[0], MLA_HEADS, QK_NOPE + QK_ROPE).astype(BF16)
    wkv = mla_w_ukv[0].reshape(KV_LORA, MLA_HEADS, QK_NOPE + V_DIM)
    wk_p = _head_cols(wkv[:, :, :QK_NOPE].reshape(KV_LORA, -1), MLA_HEADS, QK_NOPE).astype(BF16)
    wv_p = _head_cols(wkv[:, :, QK_NOPE:].reshape(KV_LORA, -1), MLA_HEADS, V_DIM).astype(BF16)
    qt, k, vt = _proj_mla(x, wd_p, row(mla_q_norm[0]), row(mla_kv_norm[0]), wq_p, wk_p, wv_p, tabs_mq, tabs_mk)
    oc = _flash(qt, k, vt, V_DIM)
    w_oc = mla_w_out[0].astype(BF16)
    wr_hi, wr_lo = _split_bf16(moe_router[1].T)
    x1, x1b, aff = _out_ln_router([oc], [w_oc], x, row(ln_mix_g[1]), row(ln_mix_b[1]), wr_hi, wr_lo)
    x = _moe_layer(x1, x1b, aff, moe_w_gate[1], moe_w_up[1], moe_w_down[1], row(ln_ffn_g[1]), row(ln_ffn_b[1]))
    return x
```

```python
import functools

import jax
import jax.numpy as jnp
from jax import lax
from jax.experimental import pallas as pl
from jax.experimental.pallas import tpu as pltpu

F32 = jnp.float32
BF16 = jnp.bfloat16

GRID_W = 64
ROPE_THETA = 10000.0
HEAD_DIM = 64
A_HEADS, A_KV = 8, 2
B_HEADS, B_KV = 8, 2
WINDOW = 128
MLA_HEADS = 16
Q_LORA, KV_LORA = 256, 128
QK_NOPE, QK_ROPE, V_DIM = 64, 32, 64
N_EXPERTS = 16
EC_FACTOR = 2
DEPTH = 2
ALPHA = (2.0 * DEPTH) ** 0.25
NEG_INF = -1e30
SHIFT_SLACK = 64.0
RMS_EPS = 1e-6
LOG2E = 1.4426950408889634
LN_EPS = 1e-5

LANES = 128
VMEM_LIMIT = 56 * 1024 * 1024

PROJ_ROWS = 512
ATTN_ROWS = 4096
ATTN_KEYS = 2048
ATTN_KEY_SUB = 512
ATTN_COLS = 256
AHEAD = 2
WIN_Q = 1024
MOE_TOKENS = 1024
MOE_SLOTS = 128
CUMSUM_CHUNK = 256


def _cparams(sem):
    return pltpu.CompilerParams(dimension_semantics=sem, vmem_limit_bytes=VMEM_LIMIT)


def _rope(x, c, s_lo, s_hi, shift):
    return x * c + pltpu.roll(x, LANES - shift, 1) * s_lo + pltpu.roll(x, shift, 1) * s_hi


def _rms(x, g, n):
    ms = jnp.sum(x * x, axis=1, keepdims=True) * (1.0 / n)
    return x * lax.rsqrt(ms + RMS_EPS) * g


def _layer_norm(y, g, b):
    mu = jnp.mean(y, axis=1, keepdims=True)
    d = y - mu
    var = jnp.mean(d * d, axis=1, keepdims=True)
    return d * lax.rsqrt(var + LN_EPS) * g + b


def _with_ones_lane(v, n):
    lane = lax.broadcasted_iota(jnp.int32, (1, LANES), 1)
    return jnp.where(lane == n, 1.0, v)


def _pack_head_pair(a, b, vd):
    lane = lax.broadcasted_iota(jnp.int32, (1, LANES), 1)
    return jnp.where(lane < vd, a, pltpu.roll(b, vd, 1))


def _proj_ab_kernel(x_ref, w_ref, gq_ref, gk_ref, ca_ref, sla_ref, sha_ref, cb_ref, slb_ref, shb_ref,
                    qa_ref, ka_ref, va_ref, qb_ref, kb_ref, vb_ref):
    xb = x_ref[0].astype(BF16)
    scale = HEAD_DIM ** -0.5 * LOG2E
    ca, sla, sha = ca_ref[...], sla_ref[...], sha_ref[...]
    cb, slb, shb = cb_ref[...], slb_ref[...], shb_ref[...]
    n_groups = A_HEADS + 2 * A_KV + B_HEADS + 2 * B_KV
    for c in range(n_groups // 2):
        pr = jnp.dot(xb, w_ref[:, c * 2 * LANES:(c + 1) * 2 * LANES], preferred_element_type=F32)
        for half in range(2):
            g = 2 * c + half
            ph = pr[:, half * LANES:(half + 1) * LANES]
            if g < A_HEADS:
                q = _rope(_rms(ph, gq_ref[...], HEAD_DIM), ca, sla, sha, HEAD_DIM // 4) * scale
                qa_ref[0, g] = q.T.astype(BF16)
            elif g < A_HEADS + A_KV:
                k = _rope(_rms(ph, gk_ref[...], HEAD_DIM), ca, sla, sha, HEAD_DIM // 4)
                ka_ref[0, g - A_HEADS] = k.astype(BF16)
            elif g < A_HEADS + 2 * A_KV:
                va_ref[0, g - A_HEADS - A_KV] = _with_ones_lane(ph, HEAD_DIM).T.astype(BF16)
            elif g < A_HEADS + 2 * A_KV + B_HEADS:
                q = _rope(ph, cb, slb, shb, HEAD_DIM // 2) * scale
                qb_ref[0, g - A_HEADS - 2 * A_KV] = q.T.astype(BF16)
            elif g < A_HEADS + 2 * A_KV + B_HEADS + B_KV:
                k = _rope(ph, cb, slb, shb, HEAD_DIM // 2)
                kb_ref[0, g - A_HEADS - 2 * A_KV - B_HEADS] = k.astype(BF16)
            else:
                vb_ref[0, g - A_HEADS - 2 * A_KV - B_HEADS - B_KV] = _with_ones_lane(ph, HEAD_DIM).T.astype(BF16)


def _proj_ab(x, w_p, gq, gk, tabs_a, tabs_b):
    B, S, D = x.shape
    tm = min(PROJ_ROWS, S)
    ns = S // tm
    row = lambda b, i: (b, i, 0)
    tab = pl.BlockSpec((tm, LANES), lambda b, i: (i, 0))
    vec = pl.BlockSpec((1, LANES), lambda b, i: (0, 0))
    hm = lambda h: pl.BlockSpec((1, h, tm, LANES), lambda b, i: (b, 0, i, 0))
    tr = lambda h: pl.BlockSpec((1, h, LANES, tm), lambda b, i: (b, 0, 0, i))
    sd = lambda h: jax.ShapeDtypeStruct((B, h, S, LANES), BF16)
    sdt = lambda h: jax.ShapeDtypeStruct((B, h, LANES, S), BF16)
    return pl.pallas_call(
        _proj_ab_kernel,
        grid=(B, ns),
        in_specs=[pl.BlockSpec((1, tm, D), row),
                  pl.BlockSpec(w_p.shape, lambda b, i: (0, 0)),
                  vec, vec, tab, tab, tab, tab, tab, tab],
        out_specs=[tr(A_HEADS), hm(A_KV), tr(A_KV), tr(B_HEADS), hm(B_KV), tr(B_KV)],
        out_shape=[sdt(A_HEADS), sd(A_KV), sdt(A_KV), sdt(B_HEADS), sd(B_KV), sdt(B_KV)],
        compiler_params=_cparams(("parallel", "parallel")),
        name="proj_ab",
    )(x, w_p, gq, gk, *tabs_a, *tabs_b)


def _flash_kernel(qt_ref, k_ref, vt_ref, o_ref, m_sc, acc_sc, pv_sc, *, NKV, G, tq, tk, ks, nk, cw, vd):
    heads = NKV * G
    M = heads * tq
    n_sub = M // cw

    def q_tile(c):
        h, j = divmod(c * cw, tq)
        return qt_ref[0, h, :, j:j + cw]

    def kv_of(c):
        return (c * cw // tq) // G

    def cols(c):
        return slice(c * cw, (c + 1) * cw)

    for c in range(n_sub):
        s0 = jnp.dot(k_ref[0, kv_of(c), 0:LANES, :], q_tile(c), preferred_element_type=F32)
        m_sc[:, cols(c)] = jnp.max(s0, axis=0, keepdims=True)
    acc_sc[...] = jnp.zeros((LANES, M), F32)

    def body(kb, carry):
        off = pl.multiple_of(kb * tk, tk)

        def keys(c, s):
            return k_ref[0, kv_of(c), pl.ds(pl.multiple_of(off + s * ks, ks), ks), :]

        def values_t(c, s=None):
            if s is None:
                return vt_ref[0, kv_of(c), :, pl.ds(off, tk)]
            return vt_ref[0, kv_of(c), :, pl.ds(pl.multiple_of(off + s * ks, ks), ks)]

        tiles = [(c, s) for c in range(n_sub) for s in range(tk // ks)]

        def scores(t):
            c, s = tiles[t]
            return jnp.dot(keys(c, s), q_tile(c), preferred_element_type=F32)

        excess = None
        ahead = {t: scores(t) for t in range(min(AHEAD, len(tiles)))}
        pts = []
        for t, (c, s) in enumerate(tiles):
            st = ahead.pop(t)
            if t + AHEAD < len(tiles):
                ahead[t + AHEAD] = scores(t + AHEAD)
            m = m_sc[:, cols(c)]
            over = jnp.max(st, axis=0, keepdims=True) - m
            excess = over if excess is None else jnp.maximum(excess, over)
            pts.append(jnp.exp2(st - m).astype(BF16))
            if s == tk // ks - 1:
                pt = pts[0] if len(pts) == 1 else jnp.concatenate(pts, axis=0)
                pv_sc[:, cols(c)] = jnp.dot(values_t(c), pt, preferred_element_type=F32)
                pts = []
        renew = jnp.max(excess) > SHIFT_SLACK

        @pl.when(jnp.logical_not(renew))
        def _():
            acc_sc[...] += pv_sc[...]

        @pl.when(renew)
        def _():
            for c, s in tiles:
                st = jnp.dot(keys(c, s), q_tile(c), preferred_element_type=F32)
                m_prev = m_sc[:, cols(c)]
                m_new = jnp.maximum(m_prev, jnp.max(st, axis=0, keepdims=True))
                alpha = jnp.exp2(m_prev - m_new)
                pt = jnp.exp2(st - m_new).astype(BF16)
                acc_sc[:, cols(c)] = (acc_sc[:, cols(c)] * alpha
                                      + jnp.dot(values_t(c, s), pt, preferred_element_type=F32))
                m_sc[:, cols(c)] = m_new

        return carry

    lax.fori_loop(0, nk, body, 0)
    acc = acc_sc[...]
    o = acc / acc[vd:vd + 1, :]
    for p in range(heads // 2):
        o_ref[0, :, p * LANES:(p + 1) * LANES] = _pack_head_pair(
            o[:, 2 * p * tq:(2 * p + 1) * tq].T, o[:, (2 * p + 1) * tq:(2 * p + 2) * tq].T, vd).astype(BF16)


def _flash(qt, k, vt, vd):
    B, H, _, S = qt.shape
    HK = k.shape[1]
    G = H // HK
    NKV = max(1, 2 // G)
    heads = NKV * G
    tq = min(ATTN_ROWS // heads, S)
    tk = min(ATTN_KEYS, S)
    M = heads * tq
    cw = min(ATTN_COLS, tq)
    ks = min(ATTN_KEY_SUB, tk)
    return pl.pallas_call(
        functools.partial(_flash_kernel, NKV=NKV, G=G, tq=tq, tk=tk, ks=ks, nk=S // tk, cw=cw, vd=vd),
        grid=(B, HK // NKV, S // tq),
        in_specs=[pl.BlockSpec((1, heads, LANES, tq), lambda b, h, i: (b, h, 0, i)),
                  pl.BlockSpec((1, NKV, S, LANES), lambda b, h, i: (b, h, 0, 0)),
                  pl.BlockSpec((1, NKV, LANES, S), lambda b, h, i: (b, h, 0, 0))],
        out_specs=pl.BlockSpec((1, tq, heads * vd), lambda b, h, i: (b, i, h)),
        out_shape=jax.ShapeDtypeStruct((B, S, H * vd), BF16),
        scratch_shapes=[pltpu.VMEM((1, M), F32), pltpu.VMEM((LANES, M), F32), pltpu.VMEM((LANES, M), F32)],
        compiler_params=_cparams(("parallel", "parallel", "parallel")),
        name="flash_attn",
    )(qt, k, vt)


def _window_kernel(sink_ref, qt_ref, k_ref, vt_ref, o_ref, *, G, tq, span, S, vd):
    kvh = pl.program_id(1)
    start = pl.program_id(2) * tq
    for j in range(tq // LANES):
        q0 = start + j * LANES
        kstart = pl.multiple_of(jnp.clip(q0 - WINDOW, 0, S - span), LANES)
        k = k_ref[0, 0, pl.ds(kstart, span), :]
        vt = vt_ref[0, 0, :, pl.ds(kstart, span)]
        kpos = kstart + lax.broadcasted_iota(jnp.int32, (span, 1), 0)
        qpos = q0 + lax.broadcasted_iota(jnp.int32, (1, LANES), 1)
        valid = jnp.abs(qpos - kpos) <= WINDOW
        outs = []
        for g in range(G):
            st = jnp.dot(k, qt_ref[0, g, :, j * LANES:(j + 1) * LANES], preferred_element_type=F32)
            st = jnp.where(valid, st, NEG_INF)
            sink = sink_ref[kvh * G + g] * LOG2E
            m = jnp.maximum(jnp.max(st, axis=0, keepdims=True), sink)
            pt = jnp.exp2(st - m).astype(BF16)
            acc = jnp.dot(vt, pt, preferred_element_type=F32)
            outs.append((acc / (acc[vd:vd + 1, :] + jnp.exp2(sink - m))).T)
        for p in range(G // 2):
            o_ref[0, j * LANES:(j + 1) * LANES, p * LANES:(p + 1) * LANES] = _pack_head_pair(
                outs[2 * p], outs[2 * p + 1], vd).astype(BF16)


def _window(qt, k, vt, sink, vd):
    B, H, _, S = qt.shape
    HK = k.shape[1]
    G = H // HK
    tq = min(WIN_Q, S)
    span = min(LANES + 2 * WINDOW, S)
    return pl.pallas_call(
        functools.partial(_window_kernel, G=G, tq=tq, span=span, S=S, vd=vd),
        grid_spec=pltpu.PrefetchScalarGridSpec(
            num_scalar_prefetch=1,
            grid=(B, HK, S // tq),
            in_specs=[pl.BlockSpec((1, G, LANES, tq), lambda b, h, i, sk: (b, h, 0, i)),
                      pl.BlockSpec((1, 1, S, LANES), lambda b, h, i, sk: (b, h, 0, 0)),
                      pl.BlockSpec((1, 1, LANES, S), lambda b, h, i, sk: (b, h, 0, 0))],
            out_specs=pl.BlockSpec((1, tq, G * vd), lambda b, h, i, sk: (b, i, h))),
        out_shape=jax.ShapeDtypeStruct((B, S, H * vd), BF16),
        compiler_params=_cparams(("parallel", "parallel", "parallel")),
        name="window_attn",
    )(sink, qt, k, vt)


def _out_ln_router_kernel(*refs, n_in):
    o_refs = refs[:n_in]
    w_refs = refs[n_in:2 * n_in]
    x_ref, g_ref, b_ref, wrh_ref, wrl_ref, x1_ref, x1b_ref, aff_ref = refs[2 * n_in:]
    h = jnp.dot(o_refs[0][0], w_refs[0][...], preferred_element_type=F32)
    for i in range(1, n_in):
        h = h + jnp.dot(o_refs[i][0], w_refs[i][...], preferred_element_type=F32)
    x1 = _layer_norm(ALPHA * x_ref[0] + h, g_ref[...], b_ref[...])
    x1_ref[0] = x1
    x_hi = x1.astype(BF16)
    x1b_ref[0] = x_hi
    x_lo = (x1 - x_hi.astype(F32)).astype(BF16)
    nt = (((1,), (1,)), ((), ()))
    logits = (lax.dot_general(wrh_ref[...], x_hi, nt, preferred_element_type=F32)
              + lax.dot_general(wrh_ref[...], x_lo, nt, preferred_element_type=F32)
              + lax.dot_general(wrl_ref[...], x_hi, nt, preferred_element_type=F32))
    z = jnp.exp(logits - jnp.max(logits, axis=0, keepdims=True))
    aff_ref[0] = z / jnp.sum(z, axis=0, keepdims=True)


def _out_ln_router(os_, ws_, x, g, b, wr_hi, wr_lo):
    B, S, D = x.shape
    E = wr_hi.shape[0]
    tm = min(PROJ_ROWS, S)
    n_in = len(os_)
    row = lambda bb, i: (bb, i, 0)
    const = lambda bb, i: (0, 0)
    in_specs = ([pl.BlockSpec((1, tm, o.shape[2]), row) for o in os_]
                + [pl.BlockSpec(w.shape, const) for w in ws_]
                + [pl.BlockSpec((1, tm, D), row), pl.BlockSpec((1, D), const), pl.BlockSpec((1, D), const),
                   pl.BlockSpec((E, D), const), pl.BlockSpec((E, D), const)])
    return pl.pallas_call(
        functools.partial(_out_ln_router_kernel, n_in=n_in),
        grid=(B, S // tm),
        in_specs=in_specs,
        out_specs=[pl.BlockSpec((1, tm, D), row), pl.BlockSpec((1, tm, D), row),
                   pl.BlockSpec((1, E, tm), lambda bb, i: (bb, 0, i))],
        out_shape=[jax.ShapeDtypeStruct((B, S, D), F32), jax.ShapeDtypeStruct((B, S, D), BF16),
                   jax.ShapeDtypeStruct((B, E, S), F32)],
        compiler_params=_cparams(("parallel", "parallel")),
        name="out_ln_router",
    )(*os_, *ws_, x, g, b, wr_hi, wr_lo)


def _topk_kernel(aff_ref, tri_ref, rel_ref, cnt_ref, *, S, E, cap, T, CH):
    aff = aff_ref[0]
    bits = pltpu.bitcast(aff, jnp.int32)
    capf = jnp.float32(cap)

    def count(mask):
        return jnp.sum(jnp.where(mask, 1.0, 0.0), axis=1, keepdims=True)

    def thr_body(i, t):
        cand = t | jnp.left_shift(jnp.int32(1), 30 - i)
        return jnp.where(count(bits >= cand) >= capf, cand, t)

    thr = lax.fori_loop(0, 31, thr_body, jnp.zeros((E, 1), jnp.int32))
    gt = bits > thr
    ties = bits == thr
    need = capf - count(gt)
    idx = lax.broadcasted_iota(jnp.int32, (E, S), 1)
    nbits = max(1, (S - 1).bit_length())

    def cut_body(i, c):
        cand = c | jnp.left_shift(jnp.int32(1), nbits - 1 - i)
        return jnp.where(count(ties & (idx < cand)) < need, cand, c)

    cut = lax.fori_loop(0, nbits, cut_body, jnp.zeros((E, 1), jnp.int32))
    sel = gt | (ties & (idx <= cut))

    tri = tri_ref[...]
    nsc = S // T
    lane_sc = lax.broadcasted_iota(jnp.int32, (E, nsc), 1)
    cnt = jnp.zeros((E, nsc), F32)
    for sc in range(nsc):
        run = jnp.zeros((E, 1), F32)
        for ch in range(T // CH):
            lo = sc * T + ch * CH
            selc = jnp.where(sel[:, lo:lo + CH], 1.0, 0.0)
            incl = jnp.dot(selc.astype(BF16), tri, preferred_element_type=F32)
            rel = jnp.where(selc > 0.0, incl - 1.0 + run, -1.0)
            rel_ref[0, :, lo:lo + CH] = rel
            run = run + incl[:, CH - 1:CH]
        cnt = jnp.where(lane_sc == sc, run, cnt)
    cnt_ref[0] = cnt


def _topk(aff, T):
    B, E, S = aff.shape
    cap = EC_FACTOR * S // N_EXPERTS
    CH = min(CUMSUM_CHUNK, T)
    r = lax.broadcasted_iota(jnp.int32, (CH, CH), 0)
    c = lax.broadcasted_iota(jnp.int32, (CH, CH), 1)
    tri = jnp.where(r <= c, 1.0, 0.0).astype(BF16)
    nsc = S // T
    return pl.pallas_call(
        functools.partial(_topk_kernel, S=S, E=E, cap=cap, T=T, CH=CH),
        grid=(B,),
        in_specs=[pl.BlockSpec((1, E, S), lambda b: (b, 0, 0)),
                  pl.BlockSpec((CH, CH), lambda b: (0, 0))],
        out_specs=[pl.BlockSpec((1, E, S), lambda b: (b, 0, 0)),
                   pl.BlockSpec((1, E, nsc), lambda b: (b, 0, 0))],
        out_shape=[jax.ShapeDtypeStruct((B, E, S), F32), jax.ShapeDtypeStruct((B, E, nsc), F32)],
        compiler_params=_cparams(("parallel",)),
        name="topk_select",
    )(aff, tri)


def _moe_kernel(nblk_ref, xb_ref, relt_ref, afft_ref, wg_ref, wu_ref, wd_ref, x1_ref, g_ref, b_ref,
                f_ref, y_sc, *, T, RB, E, nsc, TS):
    b, sc, e = pl.program_id(0), pl.program_id(1), pl.program_id(2)

    @pl.when(e == 0)
    def _():
        f_ref[...] = jnp.zeros(f_ref.shape, F32)

    nb = nblk_ref[(b * nsc + sc) * E + e]
    relt = relt_ref[0, 0]
    afft = afft_ref[0, 0]

    def ffn_block(j, carry):
        slot = (j * RB + lax.broadcasted_iota(jnp.int32, (RB, 1), 0)).astype(F32)
        hit = relt == slot
        gate = jnp.sum(jnp.where(hit, afft, 0.0), axis=1, keepdims=True)
        xg = jnp.dot(jnp.where(hit, 1.0, 0.0).astype(BF16), xb_ref[0], preferred_element_type=F32).astype(BF16)
        hg = jnp.dot(xg, wg_ref[0], preferred_element_type=F32)
        hu = jnp.dot(xg, wu_ref[0], preferred_element_type=F32)
        h = (hg / (1.0 + jnp.exp(-hg)) * hu).astype(BF16)
        y = jnp.dot(h, wd_ref[0], preferred_element_type=F32) * gate
        y_sc[pl.ds(pl.multiple_of(j * RB, RB), RB), :] = y.astype(BF16)
        return carry

    lax.fori_loop(0, nb, ffn_block, 0)

    @pl.when(nb % 2 == 1)
    def _():
        y_sc[pl.ds(pl.multiple_of(nb * RB, RB), RB), :] = jnp.zeros((RB, y_sc.shape[1]), BF16)

    row_tiles = [slice(ts * TS, (ts + 1) * TS) for ts in range(T // TS)]
    contract_slots = (((0,), (0,)), ((), ()))

    def scatter_block(kg, carry):
        koff = pl.multiple_of(kg * 2 * RB, 2 * RB)
        slot = (koff + lax.broadcasted_iota(jnp.int32, (2 * RB, 1), 0)).astype(F32)
        yk = y_sc[pl.ds(koff, 2 * RB), :]
        for rows in row_tiles:
            onehot = jnp.where(relt[:, rows] == slot, 1.0, 0.0).astype(BF16)
            f_ref[0, rows, :] += lax.dot_general(onehot, yk, contract_slots, preferred_element_type=F32)
        return carry

    lax.fori_loop(0, (nb + 1) // 2, scatter_block, 0)

    @pl.when(e == E - 1)
    def _():
        for rows in row_tiles:
            f_ref[0, rows, :] = _layer_norm(ALPHA * x1_ref[0, rows, :] + f_ref[0, rows, :], g_ref[...], b_ref[...])


def _moe(xb, relt, afft, nblk, wg, wu, wd, x1, g, b, T):
    B, S, D = xb.shape
    E, _, F = wg.shape
    nsc = S // T
    RB = MOE_SLOTS
    cap = EC_FACTOR * S // N_EXPERTS
    max_blocks = -(-min(cap, T) // RB)
    y_rows = (max_blocks + 1) * RB
    TS = min(512, T)
    return pl.pallas_call(
        functools.partial(_moe_kernel, T=T, RB=RB, E=E, nsc=nsc, TS=TS),
        grid_spec=pltpu.PrefetchScalarGridSpec(
            num_scalar_prefetch=1,
            grid=(B, nsc, E),
            in_specs=[pl.BlockSpec((1, T, D), lambda b, s, e, n: (b, s, 0)),
                      pl.BlockSpec((1, 1, 1, T), lambda b, s, e, n: (b, e, 0, s)),
                      pl.BlockSpec((1, 1, 1, T), lambda b, s, e, n: (b, e, 0, s)),
                      pl.BlockSpec((1, D, F), lambda b, s, e, n: (e, 0, 0)),
                      pl.BlockSpec((1, D, F), lambda b, s, e, n: (e, 0, 0)),
                      pl.BlockSpec((1, F, D), lambda b, s, e, n: (e, 0, 0)),
                      pl.BlockSpec((1, T, D), lambda b, s, e, n: (b, s, 0)),
                      pl.BlockSpec((1, D), lambda b, s, e, n: (0, 0)),
                      pl.BlockSpec((1, D), lambda b, s, e, n: (0, 0))],
            out_specs=pl.BlockSpec((1, T, D), lambda b, s, e, n: (b, s, 0)),
            scratch_shapes=[pltpu.VMEM((y_rows, D), BF16)]),
        out_shape=jax.ShapeDtypeStruct((B, S, D), F32),
        compiler_params=_cparams(("parallel", "parallel", "arbitrary")),
        name="moe_ffn",
    )(nblk, xb, relt, afft, wg, wu, wd, x1, g, b)


def _proj_mla_kernel(x_ref, wd_ref, gq_ref, gkv_ref, wq_ref, wk_ref, wv_ref,
                     cq_ref, slq_ref, shq_ref, ck_ref, slk_ref, shk_ref, qt_ref, k_ref, vt_ref):
    xb = x_ref[0].astype(BF16)
    d = jnp.dot(xb, wd_ref[...], preferred_element_type=F32)
    cq = _rms(d[:, :Q_LORA], gq_ref[...], Q_LORA).astype(BF16)
    ckv = _rms(d[:, Q_LORA:Q_LORA + KV_LORA], gkv_ref[...], KV_LORA).astype(BF16)
    k_rope = _rope(d[:, Q_LORA + KV_LORA:], ck_ref[...], slk_ref[...], shk_ref[...], QK_ROPE // 2)
    cqt, slq, shq = cq_ref[...], slq_ref[...], shq_ref[...]
    for c in range(MLA_HEADS // 2):
        cols = slice(c * 2 * LANES, (c + 1) * 2 * LANES)
        q2 = jnp.dot(cq, wq_ref[:, cols], preferred_element_type=F32)
        k2 = jnp.dot(ckv, wk_ref[:, cols], preferred_element_type=F32)
        v2 = jnp.dot(ckv, wv_ref[:, cols], preferred_element_type=F32)
        for half in range(2):
            h = 2 * c + half
            lanes = slice(half * LANES, (half + 1) * LANES)
            qt_ref[0, h] = _rope(q2[:, lanes], cqt, slq, shq, QK_ROPE // 2).T.astype(BF16)
            k_ref[0, h] = (k2[:, lanes] + k_rope).astype(BF16)
            vt_ref[0, h] = _with_ones_lane(v2[:, lanes], V_DIM).T.astype(BF16)


def _proj_mla(x, wd_p, gq, gkv, wq_p, wk_p, wv_p, tabs_q, tabs_k):
    B, S, D = x.shape
    tm = min(PROJ_ROWS, S)
    H = MLA_HEADS
    const = lambda b, i: (0, 0)
    tab = pl.BlockSpec((tm, LANES), lambda b, i: (i, 0))
    full = lambda a: pl.BlockSpec(a.shape, const)
    return pl.pallas_call(
        _proj_mla_kernel,
        grid=(B, S // tm),
        in_specs=[pl.BlockSpec((1, tm, D), lambda b, i: (b, i, 0)),
                  full(wd_p), full(gq), full(gkv), full(wq_p), full(wk_p), full(wv_p),
                  tab, tab, tab, tab, tab, tab],
        out_specs=[pl.BlockSpec((1, H, LANES, tm), lambda b, i: (b, 0, 0, i)),
                   pl.BlockSpec((1, H, tm, LANES), lambda b, i: (b, 0, i, 0)),
                   pl.BlockSpec((1, H, LANES, tm), lambda b, i: (b, 0, 0, i))],
        out_shape=[jax.ShapeDtypeStruct((B, H, LANES, S), BF16),
                   jax.ShapeDtypeStruct((B, H, S, LANES), BF16),
                   jax.ShapeDtypeStruct((B, H, LANES, S), BF16)],
        compiler_params=_cparams(("parallel", "parallel")),
        name="proj_mla",
    )(x, wd_p, gq, gkv, wq_p, wk_p, wv_p, *tabs_q, *tabs_k)


def _pad_lanes(a, width=LANES):
    return jnp.pad(a, [(0, 0)] * (a.ndim - 1) + [(0, width - a.shape[-1])])


def _head_cols(w, n_heads, dim):
    return _pad_lanes(w.reshape(w.shape[0], n_heads, dim)).reshape(w.shape[0], n_heads * LANES)


def _angles(pos, dim):
    freqs = ROPE_THETA ** (-(jnp.arange(0, dim, 2, dtype=F32) / dim))
    return pos[:, None] * freqs[None, :]


def _place(S, pieces):
    parts, lane = [], 0
    for off, val in pieces:
        if off > lane:
            parts.append(jnp.zeros((S, off - lane), F32))
        parts.append(val)
        lane = off + val.shape[1]
    if lane < LANES:
        parts.append(jnp.zeros((S, LANES - lane), F32))
    return jnp.concatenate(parts, axis=1)


def _tables(S):
    t = jnp.arange(S)
    ar = _angles((t // GRID_W).astype(F32), HEAD_DIM // 2)
    ac = _angles((t % GRID_W).astype(F32), HEAD_DIM // 2)
    q4 = HEAD_DIM // 4
    tabs_a = (_place(S, [(0, jnp.cos(ar)), (q4, jnp.cos(ar)), (2 * q4, jnp.cos(ac)), (3 * q4, jnp.cos(ac))]),
              _place(S, [(0, -jnp.sin(ar)), (2 * q4, -jnp.sin(ac))]),
              _place(S, [(q4, jnp.sin(ar)), (3 * q4, jnp.sin(ac))]))
    asq = _angles(t.astype(F32), HEAD_DIM)
    h2 = HEAD_DIM // 2
    tabs_b = (_place(S, [(0, jnp.cos(asq)), (h2, jnp.cos(asq))]),
              _place(S, [(0, -jnp.sin(asq))]),
              _place(S, [(h2, jnp.sin(asq))]))
    am = _angles(t.astype(F32), QK_ROPE)
    r2 = QK_ROPE // 2
    ones = jnp.ones((S, QK_NOPE), F32)
    tabs_mk = (_place(S, [(0, ones), (QK_NOPE, jnp.cos(am)), (QK_NOPE + r2, jnp.cos(am))]),
               _place(S, [(QK_NOPE, -jnp.sin(am))]),
               _place(S, [(QK_NOPE + r2, jnp.sin(am))]))
    scale = (QK_NOPE + QK_ROPE) ** -0.5 * LOG2E
    tabs_mq = tuple(tb * scale for tb in tabs_mk)
    return tabs_a, tabs_b, tabs_mq, tabs_mk


def _split_bf16(w):
    hi = w.astype(BF16)
    return hi, (w - hi.astype(F32)).astype(BF16)


def _moe_layer(x1, x1b, aff, w_gate, w_up, w_down, g, b):
    B, S, D = x1.shape
    T = min(MOE_TOKENS, S)
    rel, cnt = _topk(aff, T)
    nblk = ((cnt.astype(jnp.int32) + MOE_SLOTS - 1) // MOE_SLOTS).transpose(0, 2, 1).reshape(-1)
    relt = rel.reshape(B, N_EXPERTS, 1, S)
    afft = aff.reshape(B, N_EXPERTS, 1, S)
    return _moe(x1b, relt, afft, nblk, w_gate.astype(BF16), w_up.astype(BF16), w_down.astype(BF16), x1, g, b, T)


def kernel(x, ab_w_in, ab_q_norm, ab_k_norm, ab_sink, ab_w_out, mla_w_down, mla_q_norm, mla_kv_norm,
           mla_w_uq, mla_w_ukv, mla_w_out, ln_mix_g, ln_mix_b, moe_router, moe_w_gate, moe_w_up, moe_w_down,
           ln_ffn_g, ln_ffn_b):
    B, S, D = x.shape
    tabs_a, tabs_b, tabs_mq, tabs_mk = _tables(S)
    row = lambda v: v.reshape(1, -1)

    n_ab = A_HEADS + 2 * A_KV + B_HEADS + 2 * B_KV
    w_in_p = _head_cols(ab_w_in[0], n_ab, HEAD_DIM).astype(BF16)
    qta, ka, vta, qtb, kb, vtb = _proj_ab(x, w_in_p, _pad_lanes(row(ab_q_norm[0])), _pad_lanes(row(ab_k_norm[0])),
                                        tabs_a, tabs_b)
    oa = _flash(qta, ka, vta, HEAD_DIM)
    ob = _window(qtb, kb, vtb, ab_sink[0], HEAD_DIM)
    n_a = A_HEADS * HEAD_DIM
    w_oa = ab_w_out[0][:n_a].astype(BF16)
    w_ob = ab_w_out[0][n_a:].astype(BF16)
    wr_hi, wr_lo = _split_bf16(moe_router[0].T)
    x1, x1b, aff = _out_ln_router([oa, ob], [w_oa, w_ob], x, row(ln_mix_g[0]), row(ln_mix_b[0]), wr_hi, wr_lo)
    x = _moe_layer(x1, x1b, aff, moe_w_gate[0], moe_w_up[0], moe_w_down[0], row(ln_ffn_g[0]), row(ln_ffn_b[0]))

    wd = mla_w_down[0]
    wd_p = jnp.concatenate(
        [wd[:, :Q_LORA + KV_LORA],
         jnp.zeros((D, QK_NOPE), F32), wd[:, Q_LORA + KV_LORA:], jnp.zeros((D, LANES - QK_NOPE - QK_ROPE), F32)],
        axis=1).astype(BF16)
    wq_p = _head_cols(mla_w_uq[0], MLA_HEADS, QK_NOPE + QK_ROPE).astype(BF16)
    wkv = mla_w_ukv[0].reshape(KV_LORA, MLA_HEADS, QK_NOPE + V_DIM)
    wk_p = _head_cols(wkv[:, :, :QK_NOPE].reshape(KV_LORA, -1), MLA_HEADS, QK_NOPE).astype(BF16)
    wv_p = _head_cols(wkv[:, :, QK_NOPE:].reshape(KV_LORA, -1), MLA_HEADS, V_DIM).astype(BF16)
    qt, k, vt = _proj_mla(x, wd_p, row(mla_q_norm[0]), row(mla_kv_norm[0]), wq_p, wk_p, wv_p, tabs_mq, tabs_mk)
    oc = _flash(qt, k, vt, V_DIM)
    w_oc = mla_w_out[0].astype(BF16)
    wr_hi, wr_lo = _split_bf16(moe_router[1].T)
    x1, x1b, aff = _out_ln_router([oc], [w_oc], x, row(ln_mix_g[1]), row(ln_mix_b[1]), wr_hi, wr_lo)
    x = _moe_layer(x1, x1b, aff, moe_w_gate[1], moe_w_up[1], moe_w_down[1], row(ln_ffn_g[1]), row(ln_ffn_b[1]))
    return x
```

```python
import functools

import jax
import jax.numpy as jnp
from jax import lax
from jax.experimental import pallas as pl
from jax.experimental.pallas import tpu as pltpu

F32 = jnp.float32
BF16 = jnp.bfloat16

GRID_W = 64
ROPE_THETA = 10000.0
HEAD_DIM = 64
A_HEADS, A_KV = 8, 2
B_HEADS, B_KV = 8, 2
WINDOW = 128
MLA_HEADS = 16
Q_LORA, KV_LORA = 256, 128
QK_NOPE, QK_ROPE, V_DIM = 64, 32, 64
N_EXPERTS = 16
EC_FACTOR = 2
DEPTH = 2
ALPHA = (2.0 * DEPTH) ** 0.25
NEG_INF = -1e30
SHIFT_SLACK = 64.0
RMS_EPS = 1e-6
LOG2E = 1.4426950408889634
LN_EPS = 1e-5

LANES = 128
VMEM_LIMIT = 56 * 1024 * 1024

PROJ_ROWS = 512
ATTN_ROWS = 4096
ATTN_KEYS = 2048
ATTN_KEY_SUB = 512
ATTN_COLS = 256
AHEAD = 2
WIN_Q = 1024
MOE_TOKENS = 1024
MOE_SLOTS = 128
CUMSUM_CHUNK = 256

NT_DIMS = (((1,), (1,)), ((), ()))
TN_DIMS = (((0,), (0,)), ((), ()))


def _cparams(sem):
    return pltpu.CompilerParams(dimension_semantics=sem, vmem_limit_bytes=VMEM_LIMIT)


def _rope_t(x1, x2, c, s):
    return x1 * c - x2 * s, x2 * c + x1 * s


def _rms_t(x, g, n):
    ms = jnp.sum(x * x, axis=0, keepdims=True) * (1.0 / n)
    return x * lax.rsqrt(ms + RMS_EPS) * g


def _layer_norm(y, g, b):
    mu = jnp.mean(y, axis=1, keepdims=True)
    d = y - mu
    var = jnp.mean(d * d, axis=1, keepdims=True)
    return d * lax.rsqrt(var + LN_EPS) * g + b


def _pack_head_pair(a, b, vd):
    lane = lax.broadcasted_iota(jnp.int32, (1, LANES), 1)
    return jnp.where(lane < vd, a, pltpu.roll(b, vd, 1))


def _store_qt(ref, h, q):
    d = q.shape[0]
    ref[0, h, 0:d, :] = q.astype(BF16)
    ref[0, h, d:LANES, :] = jnp.zeros((LANES - d, q.shape[1]), BF16)


def _store_k(ref, h, kt):
    d, t = kt.shape
    ref[0, h] = jnp.concatenate([kt, jnp.zeros((LANES - d, t), F32)], axis=0).T.astype(BF16)


def _store_vt(ref, h, vt):
    d, t = vt.shape
    ref[0, h, 0:d, :] = vt.astype(BF16)
    row = lax.broadcasted_iota(jnp.int32, (LANES - d, t), 0)
    ref[0, h, d:LANES, :] = jnp.where(row == 0, 1.0, 0.0).astype(BF16)


def _proj_ab_kernel(x_ref, wt_ref, gq_ref, gk_ref, ca_ref, sa_ref, cb_ref, sb_ref,
                    qa_ref, ka_ref, va_ref, qb_ref, kb_ref, vb_ref):
    xb = x_ref[0].astype(BF16)
    pt = lax.dot_general(wt_ref[...], xb, NT_DIMS, preferred_element_type=F32)
    scale = HEAD_DIM ** -0.5 * LOG2E
    q4, h2 = HEAD_DIM // 4, HEAD_DIM // 2
    ca, sa, cb, sb = ca_ref[...], sa_ref[...], cb_ref[...], sb_ref[...]

    def rope_a(p):
        r1, r2 = _rope_t(p[0:q4], p[q4:2 * q4], ca[0:q4], sa[0:q4])
        c1, c2 = _rope_t(p[2 * q4:3 * q4], p[3 * q4:], ca[q4:], sa[q4:])
        return jnp.concatenate([r1, r2, c1, c2], axis=0)

    def rope_b(p):
        x1, x2 = _rope_t(p[0:h2], p[h2:], cb, sb)
        return jnp.concatenate([x1, x2], axis=0)

    heads = iter(pt[g * HEAD_DIM:(g + 1) * HEAD_DIM] for g in range(pt.shape[0] // HEAD_DIM))
    for h in range(A_HEADS):
        _store_qt(qa_ref, h, rope_a(_rms_t(next(heads), gq_ref[...], HEAD_DIM)) * scale)
    for h in range(A_KV):
        _store_k(ka_ref, h, rope_a(_rms_t(next(heads), gk_ref[...], HEAD_DIM)))
    for h in range(A_KV):
        _store_vt(va_ref, h, next(heads))
    for h in range(B_HEADS):
        _store_qt(qb_ref, h, rope_b(next(heads)) * scale)
    for h in range(B_KV):
        _store_k(kb_ref, h, rope_b(next(heads)))
    for h in range(B_KV):
        _store_vt(vb_ref, h, next(heads))


def _proj_ab(x, wt, gq, gk, tabs_a, tabs_b):
    B, S, D = x.shape
    tm = min(PROJ_ROWS, S)
    const = lambda b, i: (0, 0)
    tab = lambda t: pl.BlockSpec((t.shape[0], tm), lambda b, i: (0, i))
    hm = lambda h: pl.BlockSpec((1, h, tm, LANES), lambda b, i: (b, 0, i, 0))
    tr = lambda h: pl.BlockSpec((1, h, LANES, tm), lambda b, i: (b, 0, 0, i))
    sd = lambda h: jax.ShapeDtypeStruct((B, h, S, LANES), BF16)
    sdt = lambda h: jax.ShapeDtypeStruct((B, h, LANES, S), BF16)
    gq_b = jnp.broadcast_to(gq.reshape(-1, 1), (HEAD_DIM, tm))
    gk_b = jnp.broadcast_to(gk.reshape(-1, 1), (HEAD_DIM, tm))
    return pl.pallas_call(
        _proj_ab_kernel,
        grid=(B, S // tm),
        in_specs=[pl.BlockSpec((1, tm, D), lambda b, i: (b, i, 0)),
                  pl.BlockSpec(wt.shape, const),
                  pl.BlockSpec(gq_b.shape, const), pl.BlockSpec(gk_b.shape, const),
                  tab(tabs_a[0]), tab(tabs_a[1]), tab(tabs_b[0]), tab(tabs_b[1])],
        out_specs=[tr(A_HEADS), hm(A_KV), tr(A_KV), tr(B_HEADS), hm(B_KV), tr(B_KV)],
        out_shape=[sdt(A_HEADS), sd(A_KV), sdt(A_KV), sdt(B_HEADS), sd(B_KV), sdt(B_KV)],
        compiler_params=_cparams(("parallel", "parallel")),
        name="proj_ab",
    )(x, wt, gq_b, gk_b, *tabs_a, *tabs_b)


def _flash_kernel(qt_ref, k_ref, vt_ref, o_ref, m_sc, acc_sc, pv_sc, *, NKV, G, tq, tk, ks, nk, cw, vd):
    heads = NKV * G
    M = heads * tq
    n_sub = M // cw

    def q_tile(c):
        h, j = divmod(c * cw, tq)
        return qt_ref[0, h, :, j:j + cw]

    def kv_of(c):
        return (c * cw // tq) // G

    def cols(c):
        return slice(c * cw, (c + 1) * cw)

    for c in range(n_sub):
        s0 = jnp.dot(k_ref[0, kv_of(c), 0:LANES, :], q_tile(c), preferred_element_type=F32)
        m_sc[:, cols(c)] = jnp.max(s0, axis=0, keepdims=True)
    acc_sc[...] = jnp.zeros((LANES, M), F32)

    def body(kb, carry):
        off = pl.multiple_of(kb * tk, tk)

        def keys(c, s):
            return k_ref[0, kv_of(c), pl.ds(pl.multiple_of(off + s * ks, ks), ks), :]

        def values_t(c, s=None):
            if s is None:
                return vt_ref[0, kv_of(c), :, pl.ds(off, tk)]
            return vt_ref[0, kv_of(c), :, pl.ds(pl.multiple_of(off + s * ks, ks), ks)]

        tiles = [(c, s) for c in range(n_sub) for s in range(tk // ks)]

        def scores(t):
            c, s = tiles[t]
            return jnp.dot(keys(c, s), q_tile(c), preferred_element_type=F32)

        excess = None
        ahead = {t: scores(t) for t in range(min(AHEAD, len(tiles)))}
        pts = []
        for t, (c, s) in enumerate(tiles):
            st = ahead.pop(t)
            if t + AHEAD < len(tiles):
                ahead[t + AHEAD] = scores(t + AHEAD)
            m = m_sc[:, cols(c)]
            over = jnp.max(st, axis=0, keepdims=True) - m
            excess = over if excess is None else jnp.maximum(excess, over)
            pts.append(jnp.exp2(st - m).astype(BF16))
            if s == tk // ks - 1:
                pt = pts[0] if len(pts) == 1 else jnp.concatenate(pts, axis=0)
                pv_sc[:, cols(c)] = jnp.dot(values_t(c), pt, preferred_element_type=F32)
                pts = []
        renew = jnp.max(excess) > SHIFT_SLACK

        @pl.when(jnp.logical_not(renew))
        def _():
            acc_sc[...] += pv_sc[...]

        @pl.when(renew)
        def _():
            for c, s in tiles:
                st = jnp.dot(keys(c, s), q_tile(c), preferred_element_type=F32)
                m_prev = m_sc[:, cols(c)]
                m_new = jnp.maximum(m_prev, jnp.max(st, axis=0, keepdims=True))
                alpha = jnp.exp2(m_prev - m_new)
                pt = jnp.exp2(st - m_new).astype(BF16)
                acc_sc[:, cols(c)] = (acc_sc[:, cols(c)] * alpha
                                      + jnp.dot(values_t(c, s), pt, preferred_element_type=F32))
                m_sc[:, cols(c)] = m_new

        return carry

    lax.fori_loop(0, nk, body, 0)
    acc = acc_sc[...]
    o = acc / acc[vd:vd + 1, :]
    for p in range(heads // 2):
        o_ref[0, :, p * LANES:(p + 1) * LANES] = _pack_head_pair(
            o[:, 2 * p * tq:(2 * p + 1) * tq].T, o[:, (2 * p + 1) * tq:(2 * p + 2) * tq].T, vd).astype(BF16)


def _flash(qt, k, vt, vd):
    B, H, _, S = qt.shape
    HK = k.shape[1]
    G = H // HK
    NKV = max(1, 2 // G)
    heads = NKV * G
    tq = min(ATTN_ROWS // heads, S)
    tk = min(ATTN_KEYS, S)
    M = heads * tq
    cw = min(ATTN_COLS, tq)
    ks = min(ATTN_KEY_SUB, tk)
    return pl.pallas_call(
        functools.partial(_flash_kernel, NKV=NKV, G=G, tq=tq, tk=tk, ks=ks, nk=S // tk, cw=cw, vd=vd),
        grid=(B, HK // NKV, S // tq),
        in_specs=[pl.BlockSpec((1, heads, LANES, tq), lambda b, h, i: (b, h, 0, i)),
                  pl.BlockSpec((1, NKV, S, LANES), lambda b, h, i: (b, h, 0, 0)),
                  pl.BlockSpec((1, NKV, LANES, S), lambda b, h, i: (b, h, 0, 0))],
        out_specs=pl.BlockSpec((1, tq, heads * vd), lambda b, h, i: (b, i, h)),
        out_shape=jax.ShapeDtypeStruct((B, S, H * vd), BF16),
        scratch_shapes=[pltpu.VMEM((1, M), F32), pltpu.VMEM((LANES, M), F32), pltpu.VMEM((LANES, M), F32)],
        compiler_params=_cparams(("parallel", "parallel", "parallel")),
        name="flash_attn",
    )(qt, k, vt)


def _window_kernel(sink_ref, qt_ref, k_ref, vt_ref, o_ref, *, G, tq, span, S, vd):
    kvh = pl.program_id(1)
    start = pl.program_id(2) * tq
    for j in range(tq // LANES):
        q0 = start + j * LANES
        kstart = pl.multiple_of(jnp.clip(q0 - WINDOW, 0, S - span), LANES)
        k = k_ref[0, 0, pl.ds(kstart, span), :]
        vt = vt_ref[0, 0, :, pl.ds(kstart, span)]
        kpos = kstart + lax.broadcasted_iota(jnp.int32, (span, 1), 0)
        qpos = q0 + lax.broadcasted_iota(jnp.int32, (1, LANES), 1)
        valid = jnp.abs(qpos - kpos) <= WINDOW
        outs = []
        for g in range(G):
            st = jnp.dot(k, qt_ref[0, g, :, j * LANES:(j + 1) * LANES], preferred_element_type=F32)
            st = jnp.where(valid, st, NEG_INF)
            sink = sink_ref[kvh * G + g] * LOG2E
            m = jnp.maximum(jnp.max(st, axis=0, keepdims=True), sink)
            pt = jnp.exp2(st - m).astype(BF16)
            acc = jnp.dot(vt, pt, preferred_element_type=F32)
            outs.append((acc / (acc[vd:vd + 1, :] + jnp.exp2(sink - m))).T)
        for p in range(G // 2):
            o_ref[0, j * LANES:(j + 1) * LANES, p * LANES:(p + 1) * LANES] = _pack_head_pair(
                outs[2 * p], outs[2 * p + 1], vd).astype(BF16)


def _window(qt, k, vt, sink, vd):
    B, H, _, S = qt.shape
    HK = k.shape[1]
    G = H // HK
    tq = min(WIN_Q, S)
    span = min(LANES + 2 * WINDOW, S)
    return pl.pallas_call(
        functools.partial(_window_kernel, G=G, tq=tq, span=span, S=S, vd=vd),
        grid_spec=pltpu.PrefetchScalarGridSpec(
            num_scalar_prefetch=1,
            grid=(B, HK, S // tq),
            in_specs=[pl.BlockSpec((1, G, LANES, tq), lambda b, h, i, sk: (b, h, 0, i)),
                      pl.BlockSpec((1, 1, S, LANES), lambda b, h, i, sk: (b, h, 0, 0)),
                      pl.BlockSpec((1, 1, LANES, S), lambda b, h, i, sk: (b, h, 0, 0))],
            out_specs=pl.BlockSpec((1, tq, G * vd), lambda b, h, i, sk: (b, i, h))),
        out_shape=jax.ShapeDtypeStruct((B, S, H * vd), BF16),
        compiler_params=_cparams(("parallel", "parallel", "parallel")),
        name="window_attn",
    )(sink, qt, k, vt)


def _out_ln_router_kernel(*refs, n_in):
    o_refs = refs[:n_in]
    w_refs = refs[n_in:2 * n_in]
    x_ref, g_ref, b_ref, wrh_ref, wrl_ref, x1_ref, x1b_ref, aff_ref = refs[2 * n_in:]
    h = jnp.dot(o_refs[0][0], w_refs[0][...], preferred_element_type=F32)
    for i in range(1, n_in):
        h = h + jnp.dot(o_refs[i][0], w_refs[i][...], preferred_element_type=F32)
    x1 = _layer_norm(ALPHA * x_ref[0] + h, g_ref[...], b_ref[...])
    x1_ref[0] = x1
    x_hi = x1.astype(BF16)
    x1b_ref[0] = x_hi
    x_lo = (x1 - x_hi.astype(F32)).astype(BF16)
    logits = (lax.dot_general(wrh_ref[...], x_hi, NT_DIMS, preferred_element_type=F32)
              + lax.dot_general(wrh_ref[...], x_lo, NT_DIMS, preferred_element_type=F32)
              + lax.dot_general(wrl_ref[...], x_hi, NT_DIMS, preferred_element_type=F32))
    z = jnp.exp(logits - jnp.max(logits, axis=0, keepdims=True))
    aff_ref[0] = z / jnp.sum(z, axis=0, keepdims=True)


def _out_ln_router(os_, ws_, x, g, b, wr_hi, wr_lo):
    B, S, D = x.shape
    E = wr_hi.shape[0]
    tm = min(PROJ_ROWS, S)
    n_in = len(os_)
    row = lambda bb, i: (bb, i, 0)
    const = lambda bb, i: (0, 0)
    in_specs = ([pl.BlockSpec((1, tm, o.shape[2]), row) for o in os_]
                + [pl.BlockSpec(w.shape, const) for w in ws_]
                + [pl.BlockSpec((1, tm, D), row), pl.BlockSpec((1, D), const), pl.BlockSpec((1, D), const),
                   pl.BlockSpec((E, D), const), pl.BlockSpec((E, D), const)])
    return pl.pallas_call(
        functools.partial(_out_ln_router_kernel, n_in=n_in),
        grid=(B, S // tm),
        in_specs=in_specs,
        out_specs=[pl.BlockSpec((1, tm, D), row), pl.BlockSpec((1, tm, D), row),
                   pl.BlockSpec((1, E, tm), lambda bb, i: (bb, 0, i))],
        out_shape=[jax.ShapeDtypeStruct((B, S, D), F32), jax.ShapeDtypeStruct((B, S, D), BF16),
                   jax.ShapeDtypeStruct((B, E, S), F32)],
        compiler_params=_cparams(("parallel", "parallel")),
        name="out_ln_router",
    )(*os_, *ws_, x, g, b, wr_hi, wr_lo)


def _topk_kernel(aff_ref, tri_ref, rel_ref, cnt_ref, *, S, E, cap, T, CH):
    aff = aff_ref[0]
    bits = pltpu.bitcast(aff, jnp.int32)
    capf = jnp.float32(cap)

    def count(mask):
        return jnp.sum(jnp.where(mask, 1.0, 0.0), axis=1, keepdims=True)

    def thr_body(i, t):
        cand = t | jnp.left_shift(jnp.int32(1), 30 - i)
        return jnp.where(count(bits >= cand) >= capf, cand, t)

    thr = lax.fori_loop(0, 31, thr_body, jnp.zeros((E, 1), jnp.int32))
    gt = bits > thr
    ties = bits == thr
    need = capf - count(gt)
    idx = lax.broadcasted_iota(jnp.int32, (E, S), 1)
    nbits = max(1, (S - 1).bit_length())

    def cut_body(i, c):
        cand = c | jnp.left_shift(jnp.int32(1), nbits - 1 - i)
        return jnp.where(count(ties & (idx < cand)) < need, cand, c)

    cut = lax.fori_loop(0, nbits, cut_body, jnp.zeros((E, 1), jnp.int32))
    sel = gt | (ties & (idx <= cut))

    tri = tri_ref[...]
    nsc = S // T
    lane_sc = lax.broadcasted_iota(jnp.int32, (E, nsc), 1)
    cnt = jnp.zeros((E, nsc), F32)
    for sc in range(nsc):
        run = jnp.zeros((E, 1), F32)
        for ch in range(T // CH):
            lo = sc * T + ch * CH
            selc = jnp.where(sel[:, lo:lo + CH], 1.0, 0.0)
            incl = jnp.dot(selc.astype(BF16), tri, preferred_element_type=F32)
            rel = jnp.where(selc > 0.0, incl - 1.0 + run, -1.0)
            rel_ref[0, :, lo:lo + CH] = rel
            run = run + incl[:, CH - 1:CH]
        cnt = jnp.where(lane_sc == sc, run, cnt)
    cnt_ref[0] = cnt


def _topk(aff, T):
    B, E, S = aff.shape
    cap = EC_FACTOR * S // N_EXPERTS
    CH = min(CUMSUM_CHUNK, T)
    r = lax.broadcasted_iota(jnp.int32, (CH, CH), 0)
    c = lax.broadcasted_iota(jnp.int32, (CH, CH), 1)
    tri = jnp.where(r <= c, 1.0, 0.0).astype(BF16)
    nsc = S // T
    return pl.pallas_call(
        functools.partial(_topk_kernel, S=S, E=E, cap=cap, T=T, CH=CH),
        grid=(B,),
        in_specs=[pl.BlockSpec((1, E, S), lambda b: (b, 0, 0)),
                  pl.BlockSpec((CH, CH), lambda b: (0, 0))],
        out_specs=[pl.BlockSpec((1, E, S), lambda b: (b, 0, 0)),
                   pl.BlockSpec((1, E, nsc), lambda b: (b, 0, 0))],
        out_shape=[jax.ShapeDtypeStruct((B, E, S), F32), jax.ShapeDtypeStruct((B, E, nsc), F32)],
        compiler_params=_cparams(("parallel",)),
        name="topk_select",
    )(aff, tri)


def _moe_kernel(nblk_ref, xb_ref, relt_ref, afft_ref, wg_ref, wu_ref, wd_ref, x1_ref, g_ref, b_ref,
                f_ref, y_sc, *, T, RB, E, nsc, TS):
    b, sc, e = pl.program_id(0), pl.program_id(1), pl.program_id(2)

    @pl.when(e == 0)
    def _():
        f_ref[...] = jnp.zeros(f_ref.shape, F32)

    nb = nblk_ref[(b * nsc + sc) * E + e]
    relt = relt_ref[0, 0]
    afft = afft_ref[0, 0]

    def ffn_block(j, carry):
        slot = (j * RB + lax.broadcasted_iota(jnp.int32, (RB, 1), 0)).astype(F32)
        hit = relt == slot
        gate = jnp.sum(jnp.where(hit, afft, 0.0), axis=1, keepdims=True)
        xg = jnp.dot(jnp.where(hit, 1.0, 0.0).astype(BF16), xb_ref[0], preferred_element_type=F32).astype(BF16)
        hg = jnp.dot(xg, wg_ref[0], preferred_element_type=F32)
        hu = jnp.dot(xg, wu_ref[0], preferred_element_type=F32)
        h = (hg / (1.0 + jnp.exp(-hg)) * hu).astype(BF16)
        y = jnp.dot(h, wd_ref[0], preferred_element_type=F32) * gate
        y_sc[pl.ds(pl.multiple_of(j * RB, RB), RB), :] = y.astype(BF16)
        return carry

    lax.fori_loop(0, nb, ffn_block, 0)

    @pl.when(nb % 2 == 1)
    def _():
        y_sc[pl.ds(pl.multiple_of(nb * RB, RB), RB), :] = jnp.zeros((RB, y_sc.shape[1]), BF16)

    row_tiles = [slice(ts * TS, (ts + 1) * TS) for ts in range(T // TS)]

    def scatter_block(kg, carry):
        koff = pl.multiple_of(kg * 2 * RB, 2 * RB)
        slot = (koff + lax.broadcasted_iota(jnp.int32, (2 * RB, 1), 0)).astype(F32)
        yk = y_sc[pl.ds(koff, 2 * RB), :]
        for rows in row_tiles:
            onehot = jnp.where(relt[:, rows] == slot, 1.0, 0.0).astype(BF16)
            f_ref[0, rows, :] += lax.dot_general(onehot, yk, TN_DIMS, preferred_element_type=F32)
        return carry

    lax.fori_loop(0, (nb + 1) // 2, scatter_block, 0)

    @pl.when(e == E - 1)
    def _():
        for rows in row_tiles:
            f_ref[0, rows, :] = _layer_norm(ALPHA * x1_ref[0, rows, :] + f_ref[0, rows, :], g_ref[...], b_ref[...])


def _moe(xb, relt, afft, nblk, wg, wu, wd, x1, g, b, T):
    B, S, D = xb.shape
    E, _, F = wg.shape
    nsc = S // T
    RB = MOE_SLOTS
    cap = EC_FACTOR * S // N_EXPERTS
    max_blocks = -(-min(cap, T) // RB)
    y_rows = (max_blocks + 1) * RB
    TS = min(512, T)
    return pl.pallas_call(
        functools.partial(_moe_kernel, T=T, RB=RB, E=E, nsc=nsc, TS=TS),
        grid_spec=pltpu.PrefetchScalarGridSpec(
            num_scalar_prefetch=1,
            grid=(B, nsc, E),
            in_specs=[pl.BlockSpec((1, T, D), lambda b, s, e, n: (b, s, 0)),
                      pl.BlockSpec((1, 1, 1, T), lambda b, s, e, n: (b, e, 0, s)),
                      pl.BlockSpec((1, 1, 1, T), lambda b, s, e, n: (b, e, 0, s)),
                      pl.BlockSpec((1, D, F), lambda b, s, e, n: (e, 0, 0)),
                      pl.BlockSpec((1, D, F), lambda b, s, e, n: (e, 0, 0)),
                      pl.BlockSpec((1, F, D), lambda b, s, e, n: (e, 0, 0)),
                      pl.BlockSpec((1, T, D), lambda b, s, e, n: (b, s, 0)),
                      pl.BlockSpec((1, D), lambda b, s, e, n: (0, 0)),
                      pl.BlockSpec((1, D), lambda b, s, e, n: (0, 0))],
            out_specs=pl.BlockSpec((1, T, D), lambda b, s, e, n: (b, s, 0)),
            scratch_shapes=[pltpu.VMEM((y_rows, D), BF16)]),
        out_shape=jax.ShapeDtypeStruct((B, S, D), F32),
        compiler_params=_cparams(("parallel", "parallel", "arbitrary")),
        name="moe_ffn",
    )(nblk, xb, relt, afft, wg, wu, wd, x1, g, b)


def _proj_mla_kernel(x_ref, wdt_ref, gq_ref, gkv_ref, wqt_ref, wkvt_ref, c_ref, s_ref, qt_ref, k_ref, vt_ref):
    xb = x_ref[0].astype(BF16)
    dt = lax.dot_general(wdt_ref[...], xb, NT_DIMS, preferred_element_type=F32)
    cq = _rms_t(dt[0:Q_LORA], gq_ref[...], Q_LORA).astype(BF16)
    ckv = _rms_t(dt[Q_LORA:Q_LORA + KV_LORA], gkv_ref[...], KV_LORA).astype(BF16)
    r2 = QK_ROPE // 2
    c, s = c_ref[...], s_ref[...]
    kr = dt[Q_LORA + KV_LORA:Q_LORA + KV_LORA + QK_ROPE]
    kr1, kr2 = _rope_t(kr[0:r2], kr[r2:], c, s)
    qt = jnp.dot(wqt_ref[...], cq, preferred_element_type=F32)
    kvt = jnp.dot(wkvt_ref[...], ckv, preferred_element_type=F32)
    scale = (QK_NOPE + QK_ROPE) ** -0.5 * LOG2E
    dq, dkv = QK_NOPE + QK_ROPE, QK_NOPE + V_DIM
    for h in range(MLA_HEADS):
        q = qt[h * dq:(h + 1) * dq]
        q1, q2 = _rope_t(q[QK_NOPE:QK_NOPE + r2], q[QK_NOPE + r2:], c, s)
        _store_qt(qt_ref, h, jnp.concatenate([q[0:QK_NOPE], q1, q2], axis=0) * scale)
        kv = kvt[h * dkv:(h + 1) * dkv]
        _store_k(k_ref, h, jnp.concatenate([kv[0:QK_NOPE], kr1, kr2], axis=0))
        _store_vt(vt_ref, h, kv[QK_NOPE:])


def _proj_mla(x, wdt, gq, gkv, wqt, wkvt, tabs):
    B, S, D = x.shape
    tm = min(PROJ_ROWS, S)
    H = MLA_HEADS
    const = lambda b, i: (0, 0)
    full = lambda a: pl.BlockSpec(a.shape, const)
    tab = lambda t: pl.BlockSpec((t.shape[0], tm), lambda b, i: (0, i))
    gq_b = jnp.broadcast_to(gq.reshape(-1, 1), (Q_LORA, tm))
    gkv_b = jnp.broadcast_to(gkv.reshape(-1, 1), (KV_LORA, tm))
    return pl.pallas_call(
        _proj_mla_kernel,
        grid=(B, S // tm),
        in_specs=[pl.BlockSpec((1, tm, D), lambda b, i: (b, i, 0)),
                  full(wdt), full(gq_b), full(gkv_b), full(wqt), full(wkvt), tab(tabs[0]), tab(tabs[1])],
        out_specs=[pl.BlockSpec((1, H, LANES, tm), lambda b, i: (b, 0, 0, i)),
                   pl.BlockSpec((1, H, tm, LANES), lambda b, i: (b, 0, i, 0)),
                   pl.BlockSpec((1, H, LANES, tm), lambda b, i: (b, 0, 0, i))],
        out_shape=[jax.ShapeDtypeStruct((B, H, LANES, S), BF16),
                   jax.ShapeDtypeStruct((B, H, S, LANES), BF16),
                   jax.ShapeDtypeStruct((B, H, LANES, S), BF16)],
        compiler_params=_cparams(("parallel", "parallel")),
        name="proj_mla",
    )(x, wdt, gq_b, gkv_b, wqt, wkvt, *tabs)


def _angles(pos, dim):
    freqs = ROPE_THETA ** (-(jnp.arange(0, dim, 2, dtype=F32) / dim))
    return pos[:, None] * freqs[None, :]


def _tables(S):
    t = jnp.arange(S)
    ar = _angles((t // GRID_W).astype(F32), HEAD_DIM // 2)
    ac = _angles((t % GRID_W).astype(F32), HEAD_DIM // 2)
    axial = jnp.concatenate([ar, ac], axis=1).T
    seq = _angles(t.astype(F32), HEAD_DIM).T
    latent = _angles(t.astype(F32), QK_ROPE).T
    return tuple((jnp.cos(a), jnp.sin(a)) for a in (axial, seq, latent))


def _split_bf16(w):
    hi = w.astype(BF16)
    return hi, (w - hi.astype(F32)).astype(BF16)


def _moe_layer(x1, x1b, aff, w_gate, w_up, w_down, g, b):
    B, S, D = x1.shape
    T = min(MOE_TOKENS, S)
    rel, cnt = _topk(aff, T)
    nblk = ((cnt.astype(jnp.int32) + MOE_SLOTS - 1) // MOE_SLOTS).transpose(0, 2, 1).reshape(-1)
    relt = rel.reshape(B, N_EXPERTS, 1, S)
    afft = aff.reshape(B, N_EXPERTS, 1, S)
    return _moe(x1b, relt, afft, nblk, w_gate.astype(BF16), w_up.astype(BF16), w_down.astype(BF16), x1, g, b, T)


def kernel(x, ab_w_in, ab_q_norm, ab_k_norm, ab_sink, ab_w_out, mla_w_down, mla_q_norm, mla_kv_norm,
           mla_w_uq, mla_w_ukv, mla_w_out, ln_mix_g, ln_mix_b, moe_router, moe_w_gate, moe_w_up, moe_w_down,
           ln_ffn_g, ln_ffn_b):
    B, S, D = x.shape
    tabs_a, tabs_b, tabs_m = _tables(S)
    row = lambda v: v.reshape(1, -1)

    qta, ka, vta, qtb, kb, vtb = _proj_ab(x, ab_w_in[0].T.astype(BF16), ab_q_norm[0], ab_k_norm[0], tabs_a, tabs_b)
    oa = _flash(qta, ka, vta, HEAD_DIM)
    ob = _window(qtb, kb, vtb, ab_sink[0], HEAD_DIM)
    n_a = A_HEADS * HEAD_DIM
    w_oa = ab_w_out[0][:n_a].astype(BF16)
    w_ob = ab_w_out[0][n_a:].astype(BF16)
    wr_hi, wr_lo = _split_bf16(moe_router[0].T)
    x1, x1b, aff = _out_ln_router([oa, ob], [w_oa, w_ob], x, row(ln_mix_g[0]), row(ln_mix_b[0]), wr_hi, wr_lo)
    x = _moe_layer(x1, x1b, aff, moe_w_gate[0], moe_w_up[0], moe_w_down[0], row(ln_ffn_g[0]), row(ln_ffn_b[0]))

    qt, k, vt = _proj_mla(x, mla_w_down[0].T.astype(BF16), mla_q_norm[0], mla_kv_norm[0],
                          mla_w_uq[0].T.astype(BF16), mla_w_ukv[0].T.astype(BF16), tabs_m)
    oc = _flash(qt, k, vt, V_DIM)
    w_oc = mla_w_out[0].astype(BF16)
    wr_hi, wr_lo = _split_bf16(moe_router[1].T)
    x1, x1b, aff = _out_ln_router([oc], [w_oc], x, row(ln_mix_g[1]), row(ln_mix_b[1]), wr_hi, wr_lo)
    x = _moe_layer(x1, x1b, aff, moe_w_gate[1], moe_w_up[1], moe_w_down[1], row(ln_ffn_g[1]), row(ln_ffn_b[1]))
    return x
```

```python
import functools

import jax
import jax.numpy as jnp
from jax import lax
from jax.experimental import pallas as pl
from jax.experimental.pallas import tpu as pltpu

F32 = jnp.float32
BF16 = jnp.bfloat16

GRID_W = 64
ROPE_THETA = 10000.0
HEAD_DIM = 64
A_HEADS, A_KV = 8, 2
B_HEADS, B_KV = 8, 2
WINDOW = 128
MLA_HEADS = 16
Q_LORA, KV_LORA = 256, 128
QK_NOPE, QK_ROPE, V_DIM = 64, 32, 64
N_EXPERTS = 16
EC_FACTOR = 2
DEPTH = 2
ALPHA = (2.0 * DEPTH) ** 0.25
NEG_INF = -1e30
SHIFT_SLACK = 64.0
RMS_EPS = 1e-6
LOG2E = 1.4426950408889634
LN_EPS = 1e-5

LANES = 128
VMEM_LIMIT = 56 * 1024 * 1024

PROJ_ROWS = 512
ATTN_ROWS = 4096
ATTN_KEYS = 2048
ATTN_KEY_SUB = 512
ATTN_COLS = 256
AHEAD = 2
WIN_Q = 1024
MOE_TOKENS = 1024
MOE_SLOTS = 144
MOE_SLOT_STRIDE = 256
CUMSUM_CHUNK = 256

NT_DIMS = (((1,), (1,)), ((), ()))
TN_DIMS = (((0,), (0,)), ((), ()))


def _cparams(sem):
    return pltpu.CompilerParams(dimension_semantics=sem, vmem_limit_bytes=VMEM_LIMIT)


def _rope_t(x1, x2, c, s):
    return x1 * c - x2 * s, x2 * c + x1 * s


def _rms_t(x, g, n):
    ms = jnp.sum(x * x, axis=0, keepdims=True) * (1.0 / n)
    return x * lax.rsqrt(ms + RMS_EPS) * g


def _layer_norm(y, g, b):
    mu = jnp.mean(y, axis=1, keepdims=True)
    d = y - mu
    var = jnp.mean(d * d, axis=1, keepdims=True)
    return d * lax.rsqrt(var + LN_EPS) * g + b


def _pack_head_pair(a, b, vd):
    lane = lax.broadcasted_iota(jnp.int32, (1, LANES), 1)
    return jnp.where(lane < vd, a, pltpu.roll(b, vd, 1))


def _store_qt(ref, h, q):
    d = q.shape[0]
    ref[0, h, 0:d, :] = q.astype(BF16)
    ref[0, h, d:LANES, :] = jnp.zeros((LANES - d, q.shape[1]), BF16)


def _store_k(ref, h, kt):
    d, t = kt.shape
    ref[0, h] = jnp.concatenate([kt, jnp.zeros((LANES - d, t), F32)], axis=0).T.astype(BF16)


def _store_vt(ref, h, vt):
    d, t = vt.shape
    ref[0, h, 0:d, :] = vt.astype(BF16)
    row = lax.broadcasted_iota(jnp.int32, (LANES - d, t), 0)
    ref[0, h, d:LANES, :] = jnp.where(row == 0, 1.0, 0.0).astype(BF16)


def _proj_ab_kernel(x_ref, wt_ref, gq_ref, gk_ref, ca_ref, sa_ref, cb_ref, sb_ref,
                    qa_ref, ka_ref, va_ref, qb_ref, kb_ref, vb_ref):
    xb = x_ref[0].astype(BF16)
    pt = lax.dot_general(wt_ref[...], xb, NT_DIMS, preferred_element_type=F32)
    scale = HEAD_DIM ** -0.5 * LOG2E
    q4, h2 = HEAD_DIM // 4, HEAD_DIM // 2
    ca, sa, cb, sb = ca_ref[...], sa_ref[...], cb_ref[...], sb_ref[...]

    def rope_a(p):
        r1, r2 = _rope_t(p[0:q4], p[q4:2 * q4], ca[0:q4], sa[0:q4])
        c1, c2 = _rope_t(p[2 * q4:3 * q4], p[3 * q4:], ca[q4:], sa[q4:])
        return jnp.concatenate([r1, r2, c1, c2], axis=0)

    def rope_b(p):
        x1, x2 = _rope_t(p[0:h2], p[h2:], cb, sb)
        return jnp.concatenate([x1, x2], axis=0)

    heads = iter(pt[g * HEAD_DIM:(g + 1) * HEAD_DIM] for g in range(pt.shape[0] // HEAD_DIM))
    for h in range(A_HEADS):
        _store_qt(qa_ref, h, rope_a(_rms_t(next(heads), gq_ref[...], HEAD_DIM)) * scale)
    for h in range(A_KV):
        _store_k(ka_ref, h, rope_a(_rms_t(next(heads), gk_ref[...], HEAD_DIM)))
    for h in range(A_KV):
        _store_vt(va_ref, h, next(heads))
    for h in range(B_HEADS):
        _store_qt(qb_ref, h, rope_b(next(heads)) * scale)
    for h in range(B_KV):
        _store_k(kb_ref, h, rope_b(next(heads)))
    for h in range(B_KV):
        _store_vt(vb_ref, h, next(heads))


def _proj_ab(x, wt, gq, gk, tabs_a, tabs_b):
    B, S, D = x.shape
    tm = min(PROJ_ROWS, S)
    const = lambda b, i: (0, 0)
    tab = lambda t: pl.BlockSpec((t.shape[0], tm), lambda b, i: (0, i))
    hm = lambda h: pl.BlockSpec((1, h, tm, LANES), lambda b, i: (b, 0, i, 0))
    tr = lambda h: pl.BlockSpec((1, h, LANES, tm), lambda b, i: (b, 0, 0, i))
    sd = lambda h: jax.ShapeDtypeStruct((B, h, S, LANES), BF16)
    sdt = lambda h: jax.ShapeDtypeStruct((B, h, LANES, S), BF16)
    gq_b = jnp.broadcast_to(gq.reshape(-1, 1), (HEAD_DIM, tm))
    gk_b = jnp.broadcast_to(gk.reshape(-1, 1), (HEAD_DIM, tm))
    return pl.pallas_call(
        _proj_ab_kernel,
        grid=(B, S // tm),
        in_specs=[pl.BlockSpec((1, tm, D), lambda b, i: (b, i, 0)),
                  pl.BlockSpec(wt.shape, const),
                  pl.BlockSpec(gq_b.shape, const), pl.BlockSpec(gk_b.shape, const),
                  tab(tabs_a[0]), tab(tabs_a[1]), tab(tabs_b[0]), tab(tabs_b[1])],
        out_specs=[tr(A_HEADS), hm(A_KV), tr(A_KV), tr(B_HEADS), hm(B_KV), tr(B_KV)],
        out_shape=[sdt(A_HEADS), sd(A_KV), sdt(A_KV), sdt(B_HEADS), sd(B_KV), sdt(B_KV)],
        compiler_params=_cparams(("parallel", "parallel")),
        name="proj_ab",
    )(x, wt, gq_b, gk_b, *tabs_a, *tabs_b)


def _flash_kernel(qt_ref, k_ref, vt_ref, o_ref, m_sc, acc_sc, pv_sc, *, NKV, G, tq, tk, ks, nk, cw, vd):
    heads = NKV * G
    M = heads * tq
    n_sub = M // cw

    def q_tile(c):
        h, j = divmod(c * cw, tq)
        return qt_ref[0, h, :, j:j + cw]

    def kv_of(c):
        return (c * cw // tq) // G

    def cols(c):
        return slice(c * cw, (c + 1) * cw)

    for c in range(n_sub):
        s0 = jnp.dot(k_ref[0, kv_of(c), 0:LANES, :], q_tile(c), preferred_element_type=F32)
        m_sc[:, cols(c)] = jnp.max(s0, axis=0, keepdims=True)
    acc_sc[...] = jnp.zeros((LANES, M), F32)

    def body(kb, carry):
        off = pl.multiple_of(kb * tk, tk)

        def keys(c, s):
            return k_ref[0, kv_of(c), pl.ds(pl.multiple_of(off + s * ks, ks), ks), :]

        def values_t(c, s=None):
            if s is None:
                return vt_ref[0, kv_of(c), :, pl.ds(off, tk)]
            return vt_ref[0, kv_of(c), :, pl.ds(pl.multiple_of(off + s * ks, ks), ks)]

        tiles = [(c, s) for c in range(n_sub) for s in range(tk // ks)]

        def scores(t):
            c, s = tiles[t]
            return jnp.dot(keys(c, s), q_tile(c), preferred_element_type=F32)

        excess = None
        ahead = {t: scores(t) for t in range(min(AHEAD, len(tiles)))}
        pts = []
        for t, (c, s) in enumerate(tiles):
            st = ahead.pop(t)
            if t + AHEAD < len(tiles):
                ahead[t + AHEAD] = scores(t + AHEAD)
            m = m_sc[:, cols(c)]
            over = jnp.max(st, axis=0, keepdims=True) - m
            excess = over if excess is None else jnp.maximum(excess, over)
            pts.append(jnp.exp2(st - m).astype(BF16))
            if s == tk // ks - 1:
                pt = pts[0] if len(pts) == 1 else jnp.concatenate(pts, axis=0)
                pv_sc[:, cols(c)] = jnp.dot(values_t(c), pt, preferred_element_type=F32)
                pts = []
        renew = jnp.max(excess) > SHIFT_SLACK

        @pl.when(jnp.logical_not(renew))
        def _():
            acc_sc[...] += pv_sc[...]

        @pl.when(renew)
        def _():
            for c, s in tiles:
                st = jnp.dot(keys(c, s), q_tile(c), preferred_element_type=F32)
                m_prev = m_sc[:, cols(c)]
                m_new = jnp.maximum(m_prev, jnp.max(st, axis=0, keepdims=True))
                alpha = jnp.exp2(m_prev - m_new)
                pt = jnp.exp2(st - m_new).astype(BF16)
                acc_sc[:, cols(c)] = (acc_sc[:, cols(c)] * alpha
                                      + jnp.dot(values_t(c, s), pt, preferred_element_type=F32))
                m_sc[:, cols(c)] = m_new

        return carry

    lax.fori_loop(0, nk, body, 0)
    acc = acc_sc[...]
    o = acc / acc[vd:vd + 1, :]
    for p in range(heads // 2):
        o_ref[0, :, p * LANES:(p + 1) * LANES] = _pack_head_pair(
            o[:, 2 * p * tq:(2 * p + 1) * tq].T, o[:, (2 * p + 1) * tq:(2 * p + 2) * tq].T, vd).astype(BF16)


def _flash(qt, k, vt, vd):
    B, H, _, S = qt.shape
    HK = k.shape[1]
    G = H // HK
    NKV = max(1, 2 // G)
    heads = NKV * G
    tq = min(ATTN_ROWS // heads, S)
    tk = min(ATTN_KEYS, S)
    M = heads * tq
    cw = min(ATTN_COLS, tq)
    ks = min(ATTN_KEY_SUB, tk)
    return pl.pallas_call(
        functools.partial(_flash_kernel, NKV=NKV, G=G, tq=tq, tk=tk, ks=ks, nk=S // tk, cw=cw, vd=vd),
        grid=(B, HK // NKV, S // tq),
        in_specs=[pl.BlockSpec((1, heads, LANES, tq), lambda b, h, i: (b, h, 0, i)),
                  pl.BlockSpec((1, NKV, S, LANES), lambda b, h, i: (b, h, 0, 0)),
                  pl.BlockSpec((1, NKV, LANES, S), lambda b, h, i: (b, h, 0, 0))],
        out_specs=pl.BlockSpec((1, tq, heads * vd), lambda b, h, i: (b, i, h)),
        out_shape=jax.ShapeDtypeStruct((B, S, H * vd), BF16),
        scratch_shapes=[pltpu.VMEM((1, M), F32), pltpu.VMEM((LANES, M), F32), pltpu.VMEM((LANES, M), F32)],
        compiler_params=_cparams(("parallel", "parallel", "parallel")),
        name="flash_attn",
    )(qt, k, vt)


def _window_kernel(sink_ref, qt_ref, k_ref, vt_ref, o_ref, *, G, tq, span, S, vd):
    kvh = pl.program_id(1)
    start = pl.program_id(2) * tq
    for j in range(tq // LANES):
        q0 = start + j * LANES
        kstart = pl.multiple_of(jnp.clip(q0 - WINDOW, 0, S - span), LANES)
        k = k_ref[0, 0, pl.ds(kstart, span), :]
        vt = vt_ref[0, 0, :, pl.ds(kstart, span)]
        kpos = kstart + lax.broadcasted_iota(jnp.int32, (span, 1), 0)
        qpos = q0 + lax.broadcasted_iota(jnp.int32, (1, LANES), 1)
        valid = jnp.abs(qpos - kpos) <= WINDOW
        outs = []
        for g in range(G):
            st = jnp.dot(k, qt_ref[0, g, :, j * LANES:(j + 1) * LANES], preferred_element_type=F32)
            st = jnp.where(valid, st, NEG_INF)
            sink = sink_ref[kvh * G + g] * LOG2E
            m = jnp.maximum(jnp.max(st, axis=0, keepdims=True), sink)
            pt = jnp.exp2(st - m).astype(BF16)
            acc = jnp.dot(vt, pt, preferred_element_type=F32)
            outs.append((acc / (acc[vd:vd + 1, :] + jnp.exp2(sink - m))).T)
        for p in range(G // 2):
            o_ref[0, j * LANES:(j + 1) * LANES, p * LANES:(p + 1) * LANES] = _pack_head_pair(
                outs[2 * p], outs[2 * p + 1], vd).astype(BF16)


def _window(qt, k, vt, sink, vd):
    B, H, _, S = qt.shape
    HK = k.shape[1]
    G = H // HK
    tq = min(WIN_Q, S)
    span = min(LANES + 2 * WINDOW, S)
    return pl.pallas_call(
        functools.partial(_window_kernel, G=G, tq=tq, span=span, S=S, vd=vd),
        grid_spec=pltpu.PrefetchScalarGridSpec(
            num_scalar_prefetch=1,
            grid=(B, HK, S // tq),
            in_specs=[pl.BlockSpec((1, G, LANES, tq), lambda b, h, i, sk: (b, h, 0, i)),
                      pl.BlockSpec((1, 1, S, LANES), lambda b, h, i, sk: (b, h, 0, 0)),
                      pl.BlockSpec((1, 1, LANES, S), lambda b, h, i, sk: (b, h, 0, 0))],
            out_specs=pl.BlockSpec((1, tq, G * vd), lambda b, h, i, sk: (b, i, h))),
        out_shape=jax.ShapeDtypeStruct((B, S, H * vd), BF16),
        compiler_params=_cparams(("parallel", "parallel", "parallel")),
        name="window_attn",
    )(sink, qt, k, vt)


def _out_ln_router_kernel(*refs, n_in):
    o_refs = refs[:n_in]
    w_refs = refs[n_in:2 * n_in]
    x_ref, g_ref, b_ref, wrh_ref, wrl_ref, x1_ref, x1b_ref, aff_ref = refs[2 * n_in:]
    h = jnp.dot(o_refs[0][0], w_refs[0][...], preferred_element_type=F32)
    for i in range(1, n_in):
        h = h + jnp.dot(o_refs[i][0], w_refs[i][...], preferred_element_type=F32)
    x1 = _layer_norm(ALPHA * x_ref[0] + h, g_ref[...], b_ref[...])
    x1_ref[0] = x1
    x_hi = x1.astype(BF16)
    x1b_ref[0] = x_hi
    x_lo = (x1 - x_hi.astype(F32)).astype(BF16)
    logits = (lax.dot_general(wrh_ref[...], x_hi, NT_DIMS, preferred_element_type=F32)
              + lax.dot_general(wrh_ref[...], x_lo, NT_DIMS, preferred_element_type=F32)
              + lax.dot_general(wrl_ref[...], x_hi, NT_DIMS, preferred_element_type=F32))
    z = jnp.exp(logits - jnp.max(logits, axis=0, keepdims=True))
    aff_ref[0] = z / jnp.sum(z, axis=0, keepdims=True)


def _out_ln_router(os_, ws_, x, g, b, wr_hi, wr_lo):
    B, S, D = x.shape
    E = wr_hi.shape[0]
    tm = min(PROJ_ROWS, S)
    n_in = len(os_)
    row = lambda bb, i: (bb, i, 0)
    const = lambda bb, i: (0, 0)
    in_specs = ([pl.BlockSpec((1, tm, o.shape[2]), row) for o in os_]
                + [pl.BlockSpec(w.shape, const) for w in ws_]
                + [pl.BlockSpec((1, tm, D), row), pl.BlockSpec((1, D), const), pl.BlockSpec((1, D), const),
                   pl.BlockSpec((E, D), const), pl.BlockSpec((E, D), const)])
    return pl.pallas_call(
        functools.partial(_out_ln_router_kernel, n_in=n_in),
        grid=(B, S // tm),
        in_specs=in_specs,
        out_specs=[pl.BlockSpec((1, tm, D), row), pl.BlockSpec((1, tm, D), row),
                   pl.BlockSpec((1, E, tm), lambda bb, i: (bb, 0, i))],
        out_shape=[jax.ShapeDtypeStruct((B, S, D), F32), jax.ShapeDtypeStruct((B, S, D), BF16),
                   jax.ShapeDtypeStruct((B, E, S), F32)],
        compiler_params=_cparams(("parallel", "parallel")),
        name="out_ln_router",
    )(*os_, *ws_, x, g, b, wr_hi, wr_lo)


def _topk_kernel(aff_ref, tri_ref, rel_ref, cnt_ref, *, S, E, cap, T, CH):
    aff = aff_ref[0]
    bits = pltpu.bitcast(aff, jnp.int32)
    capf = jnp.float32(cap)

    def count(mask):
        return jnp.sum(jnp.where(mask, 1.0, 0.0), axis=1, keepdims=True)

    def thr_body(i, t):
        cand = t | jnp.left_shift(jnp.int32(1), 30 - i)
        return jnp.where(count(bits >= cand) >= capf, cand, t)

    thr = lax.fori_loop(0, 31, thr_body, jnp.zeros((E, 1), jnp.int32))
    gt = bits > thr
    ties = bits == thr
    need = capf - count(gt)
    idx = lax.broadcasted_iota(jnp.int32, (E, S), 1)
    nbits = max(1, (S - 1).bit_length())

    def cut_body(i, c):
        cand = c | jnp.left_shift(jnp.int32(1), nbits - 1 - i)
        return jnp.where(count(ties & (idx < cand)) < need, cand, c)

    cut = lax.fori_loop(0, nbits, cut_body, jnp.zeros((E, 1), jnp.int32))
    sel = gt | (ties & (idx <= cut))

    tri = tri_ref[...]
    nsc = S // T
    lane_sc = lax.broadcasted_iota(jnp.int32, (E, nsc), 1)
    cnt = jnp.zeros((E, nsc), F32)
    for sc in range(nsc):
        run = jnp.zeros((E, 1), F32)
        for ch in range(T // CH):
            lo = sc * T + ch * CH
            selc = jnp.where(sel[:, lo:lo + CH], 1.0, 0.0)
            incl = jnp.dot(selc.astype(BF16), tri, preferred_element_type=F32)
            rel = jnp.where(selc > 0.0, incl - 1.0 + run, -1.0)
            rel_ref[0, :, lo:lo + CH] = rel
            run = run + incl[:, CH - 1:CH]
        cnt = jnp.where(lane_sc == sc, run, cnt)
    cnt_ref[0] = cnt


def _topk(aff, T):
    B, E, S = aff.shape
    cap = EC_FACTOR * S // N_EXPERTS
    CH = min(CUMSUM_CHUNK, T)
    r = lax.broadcasted_iota(jnp.int32, (CH, CH), 0)
    c = lax.broadcasted_iota(jnp.int32, (CH, CH), 1)
    tri = jnp.where(r <= c, 1.0, 0.0).astype(BF16)
    nsc = S // T
    return pl.pallas_call(
        functools.partial(_topk_kernel, S=S, E=E, cap=cap, T=T, CH=CH),
        grid=(B,),
        in_specs=[pl.BlockSpec((1, E, S), lambda b: (b, 0, 0)),
                  pl.BlockSpec((CH, CH), lambda b: (0, 0))],
        out_specs=[pl.BlockSpec((1, E, S), lambda b: (b, 0, 0)),
                   pl.BlockSpec((1, E, nsc), lambda b: (b, 0, 0))],
        out_shape=[jax.ShapeDtypeStruct((B, E, S), F32), jax.ShapeDtypeStruct((B, E, nsc), F32)],
        compiler_params=_cparams(("parallel",)),
        name="topk_select",
    )(aff, tri)


def _moe_kernel(nblk_ref, xb_ref, relt_ref, afft_ref, wg_ref, wu_ref, wd_ref, x1_ref, g_ref, b_ref,
                f_ref, y_sc, *, T, RB, RS, E, nsc, TS):
    b, sc, e = pl.program_id(0), pl.program_id(1), pl.program_id(2)

    @pl.when(e == 0)
    def _():
        f_ref[...] = jnp.zeros(f_ref.shape, F32)

    nb = nblk_ref[(b * nsc + sc) * E + e]
    relt = relt_ref[0, 0]
    afft = afft_ref[0, 0]

    def ffn_block(j, carry):
        slot = (j * RB + lax.broadcasted_iota(jnp.int32, (RB, 1), 0)).astype(F32)
        hit = relt == slot
        gate = jnp.sum(jnp.where(hit, afft, 0.0), axis=1, keepdims=True)
        xg = jnp.dot(jnp.where(hit, 1.0, 0.0).astype(BF16), xb_ref[0], preferred_element_type=F32).astype(BF16)
        hg = jnp.dot(xg, wg_ref[0], preferred_element_type=F32)
        hu = jnp.dot(xg, wu_ref[0], preferred_element_type=F32)
        h = (hg / (1.0 + jnp.exp(-hg)) * hu).astype(BF16)
        y = jnp.dot(h, wd_ref[0], preferred_element_type=F32) * gate
        base = pl.multiple_of(j * RS, RS)
        y_sc[pl.ds(base, RB), :] = y.astype(BF16)
        y_sc[pl.ds(base + RB, RS - RB), :] = jnp.zeros((RS - RB, y_sc.shape[1]), BF16)
        return carry

    lax.fori_loop(0, nb, ffn_block, 0)

    row_tiles = [slice(ts * TS, (ts + 1) * TS) for ts in range(T // TS)]

    def scatter_block(j, carry):
        slot = (j * RB + lax.broadcasted_iota(jnp.int32, (RS, 1), 0)).astype(F32)
        yk = y_sc[pl.ds(pl.multiple_of(j * RS, RS), RS), :]
        for rows in row_tiles:
            onehot = jnp.where(relt[:, rows] == slot, 1.0, 0.0).astype(BF16)
            f_ref[0, rows, :] += lax.dot_general(onehot, yk, TN_DIMS, preferred_element_type=F32)
        return carry

    lax.fori_loop(0, nb, scatter_block, 0)

    @pl.when(e == E - 1)
    def _():
        for rows in row_tiles:
            f_ref[0, rows, :] = _layer_norm(ALPHA * x1_ref[0, rows, :] + f_ref[0, rows, :], g_ref[...], b_ref[...])


def _moe(xb, relt, afft, nblk, wg, wu, wd, x1, g, b, T):
    B, S, D = xb.shape
    E, _, F = wg.shape
    nsc = S // T
    RB, RS = MOE_SLOTS, MOE_SLOT_STRIDE
    cap = EC_FACTOR * S // N_EXPERTS
    max_blocks = -(-min(cap, T) // RB)
    TS = min(512, T)
    return pl.pallas_call(
        functools.partial(_moe_kernel, T=T, RB=RB, RS=RS, E=E, nsc=nsc, TS=TS),
        grid_spec=pltpu.PrefetchScalarGridSpec(
            num_scalar_prefetch=1,
            grid=(B, nsc, E),
            in_specs=[pl.BlockSpec((1, T, D), lambda b, s, e, n: (b, s, 0)),
                      pl.BlockSpec((1, 1, 1, T), lambda b, s, e, n: (b, e, 0, s)),
                      pl.BlockSpec((1, 1, 1, T), lambda b, s, e, n: (b, e, 0, s)),
                      pl.BlockSpec((1, D, F), lambda b, s, e, n: (e, 0, 0)),
                      pl.BlockSpec((1, D, F), lambda b, s, e, n: (e, 0, 0)),
                      pl.BlockSpec((1, F, D), lambda b, s, e, n: (e, 0, 0)),
                      pl.BlockSpec((1, T, D), lambda b, s, e, n: (b, s, 0)),
                      pl.BlockSpec((1, D), lambda b, s, e, n: (0, 0)),
                      pl.BlockSpec((1, D), lambda b, s, e, n: (0, 0))],
            out_specs=pl.BlockSpec((1, T, D), lambda b, s, e, n: (b, s, 0)),
            scratch_shapes=[pltpu.VMEM((max_blocks * RS, D), BF16)]),
        out_shape=jax.ShapeDtypeStruct((B, S, D), F32),
        compiler_params=_cparams(("parallel", "parallel", "arbitrary")),
        name="moe_ffn",
    )(nblk, xb, relt, afft, wg, wu, wd, x1, g, b)


def _proj_mla_kernel(x_ref, wdt_ref, gq_ref, gkv_ref, wqt_ref, wkvt_ref, c_ref, s_ref, qt_ref, k_ref, vt_ref):
    xb = x_ref[0].astype(BF16)
    dt = lax.dot_general(wdt_ref[...], xb, NT_DIMS, preferred_element_type=F32)
    cq = _rms_t(dt[0:Q_LORA], gq_ref[...], Q_LORA).astype(BF16)
    ckv = _rms_t(dt[Q_LORA:Q_LORA + KV_LORA], gkv_ref[...], KV_LORA).astype(BF16)
    r2 = QK_ROPE // 2
    c, s = c_ref[...], s_ref[...]
    kr = dt[Q_LORA + KV_LORA:Q_LORA + KV_LORA + QK_ROPE]
    kr1, kr2 = _rope_t(kr[0:r2], kr[r2:], c, s)
    qt = jnp.dot(wqt_ref[...], cq, preferred_element_type=F32)
    kvt = jnp.dot(wkvt_ref[...], ckv, preferred_element_type=F32)
    scale = (QK_NOPE + QK_ROPE) ** -0.5 * LOG2E
    dq, dkv = QK_NOPE + QK_ROPE, QK_NOPE + V_DIM
    for h in range(MLA_HEADS):
        q = qt[h * dq:(h + 1) * dq]
        q1, q2 = _rope_t(q[QK_NOPE:QK_NOPE + r2], q[QK_NOPE + r2:], c, s)
        _store_qt(qt_ref, h, jnp.concatenate([q[0:QK_NOPE], q1, q2], axis=0) * scale)
        kv = kvt[h * dkv:(h + 1) * dkv]
        _store_k(k_ref, h, jnp.concatenate([kv[0:QK_NOPE], kr1, kr2], axis=0))
        _store_vt(vt_ref, h, kv[QK_NOPE:])


def _proj_mla(x, wdt, gq, gkv, wqt, wkvt, tabs):
    B, S, D = x.shape
    tm = min(PROJ_ROWS, S)
    H = MLA_HEADS
    const = lambda b, i: (0, 0)
    full = lambda a: pl.BlockSpec(a.shape, const)
    tab = lambda t: pl.BlockSpec((t.shape[0], tm), lambda b, i: (0, i))
    gq_b = jnp.broadcast_to(gq.reshape(-1, 1), (Q_LORA, tm))
    gkv_b = jnp.broadcast_to(gkv.reshape(-1, 1), (KV_LORA, tm))
    return pl.pallas_call(
        _proj_mla_kernel,
        grid=(B, S // tm),
        in_specs=[pl.BlockSpec((1, tm, D), lambda b, i: (b, i, 0)),
                  full(wdt), full(gq_b), full(gkv_b), full(wqt), full(wkvt), tab(tabs[0]), tab(tabs[1])],
        out_specs=[pl.BlockSpec((1, H, LANES, tm), lambda b, i: (b, 0, 0, i)),
                   pl.BlockSpec((1, H, tm, LANES), lambda b, i: (b, 0, i, 0)),
                   pl.BlockSpec((1, H, LANES, tm), lambda b, i: (b, 0, 0, i))],
        out_shape=[jax.ShapeDtypeStruct((B, H, LANES, S), BF16),
                   jax.ShapeDtypeStruct((B, H, S, LANES), BF16),
                   jax.ShapeDtypeStruct((B, H, LANES, S), BF16)],
        compiler_params=_cparams(("parallel", "parallel")),
        name="proj_mla",
    )(x, wdt, gq_b, gkv_b, wqt, wkvt, *tabs)


def _angles(pos, dim):
    freqs = ROPE_THETA ** (-(jnp.arange(0, dim, 2, dtype=F32) / dim))
    return pos[:, None] * freqs[None, :]


def _tables(S):
    t = jnp.arange(S)
    ar = _angles((t // GRID_W).astype(F32), HEAD_DIM // 2)
    ac = _angles((t % GRID_W).astype(F32), HEAD_DIM // 2)
    axial = jnp.concatenate([ar, ac], axis=1).T
    seq = _angles(t.astype(F32), HEAD_DIM).T
    latent = _angles(t.astype(F32), QK_ROPE).T
    return tuple((jnp.cos(a), jnp.sin(a)) for a in (axial, seq, latent))


def _split_bf16(w):
    hi = w.astype(BF16)
    return hi, (w - hi.astype(F32)).astype(BF16)


def _moe_layer(x1, x1b, aff, w_gate, w_up, w_down, g, b):
    B, S, D = x1.shape
    T = min(MOE_TOKENS, S)
    rel, cnt = _topk(aff, T)
    nblk = ((cnt.astype(jnp.int32) + MOE_SLOTS - 1) // MOE_SLOTS).transpose(0, 2, 1).reshape(-1)
    relt = rel.reshape(B, N_EXPERTS, 1, S)
    afft = aff.reshape(B, N_EXPERTS, 1, S)
    return _moe(x1b, relt, afft, nblk, w_gate.astype(BF16), w_up.astype(BF16), w_down.astype(BF16), x1, g, b, T)


def kernel(x, ab_w_in, ab_q_norm, ab_k_norm, ab_sink, ab_w_out, mla_w_down, mla_q_norm, mla_kv_norm,
           mla_w_uq, mla_w_ukv, mla_w_out, ln_mix_g, ln_mix_b, moe_router, moe_w_gate, moe_w_up, moe_w_down,
           ln_ffn_g, ln_ffn_b):
    B, S, D = x.shape
    tabs_a, tabs_b, tabs_m = _tables(S)
    row = lambda v: v.reshape(1, -1)

    qta, ka, vta, qtb, kb, vtb = _proj_ab(x, ab_w_in[0].T.astype(BF16), ab_q_norm[0], ab_k_norm[0], tabs_a, tabs_b)
    oa = _flash(qta, ka, vta, HEAD_DIM)
    ob = _window(qtb, kb, vtb, ab_sink[0], HEAD_DIM)
    n_a = A_HEADS * HEAD_DIM
    w_oa = ab_w_out[0][:n_a].astype(BF16)
    w_ob = ab_w_out[0][n_a:].astype(BF16)
    wr_hi, wr_lo = _split_bf16(moe_router[0].T)
    x1, x1b, aff = _out_ln_router([oa, ob], [w_oa, w_ob], x, row(ln_mix_g[0]), row(ln_mix_b[0]), wr_hi, wr_lo)
    x = _moe_layer(x1, x1b, aff, moe_w_gate[0], moe_w_up[0], moe_w_down[0], row(ln_ffn_g[0]), row(ln_ffn_b[0]))

    qt, k, vt = _proj_mla(x, mla_w_down[0].T.astype(BF16), mla_q_norm[0], mla_kv_norm[0],
                          mla_w_uq[0].T.astype(BF16), mla_w_ukv[0].T.astype(BF16), tabs_m)
    oc = _flash(qt, k, vt, V_DIM)
    w_oc = mla_w_out[0].astype(BF16)
    wr_hi, wr_lo = _split_bf16(moe_router[1].T)
    x1, x1b, aff = _out_ln_router([oc], [w_oc], x, row(ln_mix_g[1]), row(ln_mix_b[1]), wr_hi, wr_lo)
    x = _moe_layer(x1, x1b, aff, moe_w_gate[1], moe_w_up[1], moe_w_down[1], row(ln_ffn_g[1]), row(ln_ffn_b[1]))
    return x
```

```python
import functools

import jax
import jax.numpy as jnp
from jax import lax
from jax.experimental import pallas as pl
from jax.experimental.pallas import tpu as pltpu

F32 = jnp.float32
BF16 = jnp.bfloat16

GRID_W = 64
ROPE_THETA = 10000.0
HEAD_DIM = 64
A_HEADS, A_KV = 8, 2
B_HEADS, B_KV = 8, 2
WINDOW = 128
MLA_HEADS = 16
Q_LORA, KV_LORA = 256, 128
QK_NOPE, QK_ROPE, V_DIM = 64, 32, 64
N_EXPERTS = 16
EC_FACTOR = 2
DEPTH = 2
ALPHA = (2.0 * DEPTH) ** 0.25
NEG_INF = -1e30
SHIFT_SLACK = 64.0
RMS_EPS = 1e-6
LOG2E = 1.4426950408889634
LN_EPS = 1e-5

LANES = 128
VMEM_LIMIT = 56 * 1024 * 1024

PROJ_ROWS = 512
ATTN_ROWS = 4096
ATTN_KEYS = 2048
ATTN_KEY_SUB = 512
ATTN_COLS = 256
AHEAD = 2
WIN_Q = 1024
MOE_TOKENS = 1024
MOE_SLOTS = 144
CUMSUM_CHUNK = 256

NT_DIMS = (((1,), (1,)), ((), ()))
TN_DIMS = (((0,), (0,)), ((), ()))


def _cparams(sem):
    return pltpu.CompilerParams(dimension_semantics=sem, vmem_limit_bytes=VMEM_LIMIT)


def _rope_t(x1, x2, c, s):
    return x1 * c - x2 * s, x2 * c + x1 * s


def _rms_t(x, g, n):
    ms = jnp.sum(x * x, axis=0, keepdims=True) * (1.0 / n)
    return x * lax.rsqrt(ms + RMS_EPS) * g


def _layer_norm(y, g, b):
    mu = jnp.mean(y, axis=1, keepdims=True)
    d = y - mu
    var = jnp.mean(d * d, axis=1, keepdims=True)
    return d * lax.rsqrt(var + LN_EPS) * g + b


def _pack_head_pair(a, b, vd):
    lane = lax.broadcasted_iota(jnp.int32, (1, LANES), 1)
    return jnp.where(lane < vd, a, pltpu.roll(b, vd, 1))


def _store_qt(ref, h, q):
    d = q.shape[0]
    ref[0, h, 0:d, :] = q.astype(BF16)
    ref[0, h, d:LANES, :] = jnp.zeros((LANES - d, q.shape[1]), BF16)


def _store_k(ref, h, kt):
    d, t = kt.shape
    ref[0, h] = jnp.concatenate([kt, jnp.zeros((LANES - d, t), F32)], axis=0).T.astype(BF16)


def _store_vt(ref, h, vt):
    d, t = vt.shape
    ref[0, h, 0:d, :] = vt.astype(BF16)
    row = lax.broadcasted_iota(jnp.int32, (LANES - d, t), 0)
    ref[0, h, d:LANES, :] = jnp.where(row == 0, 1.0, 0.0).astype(BF16)


def _proj_ab_kernel(x_ref, wt_ref, gq_ref, gk_ref, ca_ref, sa_ref, cb_ref, sb_ref,
                    qa_ref, ka_ref, va_ref, qb_ref, kb_ref, vb_ref):
    xb = x_ref[0].astype(BF16)
    pt = lax.dot_general(wt_ref[...], xb, NT_DIMS, preferred_element_type=F32)
    scale = HEAD_DIM ** -0.5 * LOG2E
    q4, h2 = HEAD_DIM // 4, HEAD_DIM // 2
    ca, sa, cb, sb = ca_ref[...], sa_ref[...], cb_ref[...], sb_ref[...]

    def rope_a(p):
        r1, r2 = _rope_t(p[0:q4], p[q4:2 * q4], ca[0:q4], sa[0:q4])
        c1, c2 = _rope_t(p[2 * q4:3 * q4], p[3 * q4:], ca[q4:], sa[q4:])
        return jnp.concatenate([r1, r2, c1, c2], axis=0)

    def rope_b(p):
        x1, x2 = _rope_t(p[0:h2], p[h2:], cb, sb)
        return jnp.concatenate([x1, x2], axis=0)

    heads = iter(pt[g * HEAD_DIM:(g + 1) * HEAD_DIM] for g in range(pt.shape[0] // HEAD_DIM))
    for h in range(A_HEADS):
        _store_qt(qa_ref, h, rope_a(_rms_t(next(heads), gq_ref[...], HEAD_DIM)) * scale)
    for h in range(A_KV):
        _store_k(ka_ref, h, rope_a(_rms_t(next(heads), gk_ref[...], HEAD_DIM)))
    for h in range(A_KV):
        _store_vt(va_ref, h, next(heads))
    for h in range(B_HEADS):
        _store_qt(qb_ref, h, rope_b(next(heads)) * scale)
    for h in range(B_KV):
        _store_k(kb_ref, h, rope_b(next(heads)))
    for h in range(B_KV):
        _store_vt(vb_ref, h, next(heads))


def _proj_ab(x, wt, gq, gk, tabs_a, tabs_b):
    B, S, D = x.shape
    tm = min(PROJ_ROWS, S)
    const = lambda b, i: (0, 0)
    tab = lambda t: pl.BlockSpec((t.shape[0], tm), lambda b, i: (0, i))
    hm = lambda h: pl.BlockSpec((1, h, tm, LANES), lambda b, i: (b, 0, i, 0))
    tr = lambda h: pl.BlockSpec((1, h, LANES, tm), lambda b, i: (b, 0, 0, i))
    sd = lambda h: jax.ShapeDtypeStruct((B, h, S, LANES), BF16)
    sdt = lambda h: jax.ShapeDtypeStruct((B, h, LANES, S), BF16)
    gq_b = jnp.broadcast_to(gq.reshape(-1, 1), (HEAD_DIM, tm))
    gk_b = jnp.broadcast_to(gk.reshape(-1, 1), (HEAD_DIM, tm))
    return pl.pallas_call(
        _proj_ab_kernel,
        grid=(B, S // tm),
        in_specs=[pl.BlockSpec((1, tm, D), lambda b, i: (b, i, 0)),
                  pl.BlockSpec(wt.shape, const),
                  pl.BlockSpec(gq_b.shape, const), pl.BlockSpec(gk_b.shape, const),
                  tab(tabs_a[0]), tab(tabs_a[1]), tab(tabs_b[0]), tab(tabs_b[1])],
        out_specs=[tr(A_HEADS), hm(A_KV), tr(A_KV), tr(B_HEADS), hm(B_KV), tr(B_KV)],
        out_shape=[sdt(A_HEADS), sd(A_KV), sdt(A_KV), sdt(B_HEADS), sd(B_KV), sdt(B_KV)],
        compiler_params=_cparams(("parallel", "parallel")),
        name="proj_ab",
    )(x, wt, gq_b, gk_b, *tabs_a, *tabs_b)


def _flash_kernel(qt_ref, k_ref, vt_ref, o_ref, m_sc, acc_sc, pv_sc, *, NKV, G, tq, tk, ks, nk, cw, vd):
    heads = NKV * G
    M = heads * tq
    n_sub = M // cw

    def q_tile(c):
        h, j = divmod(c * cw, tq)
        return qt_ref[0, h, :, j:j + cw]

    def kv_of(c):
        return (c * cw // tq) // G

    def cols(c):
        return slice(c * cw, (c + 1) * cw)

    for c in range(n_sub):
        s0 = jnp.dot(k_ref[0, kv_of(c), 0:LANES, :], q_tile(c), preferred_element_type=F32)
        m_sc[:, cols(c)] = jnp.max(s0, axis=0, keepdims=True)
    acc_sc[...] = jnp.zeros((LANES, M), F32)
    pv_sc[...] = jnp.zeros((LANES, M), F32)

    def body(kb, carry):
        off = pl.multiple_of(kb * tk, tk)
        acc_sc[...] += pv_sc[...]

        def keys(c, s):
            return k_ref[0, kv_of(c), pl.ds(pl.multiple_of(off + s * ks, ks), ks), :]

        def values_t(c, s=None):
            if s is None:
                return vt_ref[0, kv_of(c), :, pl.ds(off, tk)]
            return vt_ref[0, kv_of(c), :, pl.ds(pl.multiple_of(off + s * ks, ks), ks)]

        tiles = [(c, s) for c in range(n_sub) for s in range(tk // ks)]

        def scores(t):
            c, s = tiles[t]
            return jnp.dot(keys(c, s), q_tile(c), preferred_element_type=F32)

        excess = None
        ahead = {t: scores(t) for t in range(min(AHEAD, len(tiles)))}
        pts = []
        for t, (c, s) in enumerate(tiles):
            st = ahead.pop(t)
            if t + AHEAD < len(tiles):
                ahead[t + AHEAD] = scores(t + AHEAD)
            m = m_sc[:, cols(c)]
            over = jnp.max(st, axis=0, keepdims=True) - m
            excess = over if excess is None else jnp.maximum(excess, over)
            pts.append(jnp.exp2(st - m).astype(BF16))
            if s == tk // ks - 1:
                pt = pts[0] if len(pts) == 1 else jnp.concatenate(pts, axis=0)
                pv_sc[:, cols(c)] = jnp.dot(values_t(c), pt, preferred_element_type=F32)
                pts = []
        renew = jnp.max(excess) > SHIFT_SLACK

        @pl.when(renew)
        def _():
            pv_sc[...] = jnp.zeros((LANES, M), F32)
            for c, s in tiles:
                st = jnp.dot(keys(c, s), q_tile(c), preferred_element_type=F32)
                m_prev = m_sc[:, cols(c)]
                m_new = jnp.maximum(m_prev, jnp.max(st, axis=0, keepdims=True))
                alpha = jnp.exp2(m_prev - m_new)
                pt = jnp.exp2(st - m_new).astype(BF16)
                acc_sc[:, cols(c)] = (acc_sc[:, cols(c)] * alpha
                                      + jnp.dot(values_t(c, s), pt, preferred_element_type=F32))
                m_sc[:, cols(c)] = m_new

        return carry

    lax.fori_loop(0, nk, body, 0)
    acc = acc_sc[...] + pv_sc[...]
    o = acc / acc[vd:vd + 1, :]
    for p in range(heads // 2):
        o_ref[0, :, p * LANES:(p + 1) * LANES] = _pack_head_pair(
            o[:, 2 * p * tq:(2 * p + 1) * tq].T, o[:, (2 * p + 1) * tq:(2 * p + 2) * tq].T, vd).astype(BF16)


def _flash(qt, k, vt, vd):
    B, H, _, S = qt.shape
    HK = k.shape[1]
    G = H // HK
    NKV = max(1, 2 // G)
    heads = NKV * G
    tq = min(ATTN_ROWS // heads, S)
    tk = min(ATTN_KEYS, S)
    M = heads * tq
    cw = min(ATTN_COLS, tq)
    ks = min(ATTN_KEY_SUB, tk)
    return pl.pallas_call(
        functools.partial(_flash_kernel, NKV=NKV, G=G, tq=tq, tk=tk, ks=ks, nk=S // tk, cw=cw, vd=vd),
        grid=(B, HK // NKV, S // tq),
        in_specs=[pl.BlockSpec((1, heads, LANES, tq), lambda b, h, i: (b, h, 0, i)),
                  pl.BlockSpec((1, NKV, S, LANES), lambda b, h, i: (b, h, 0, 0)),
                  pl.BlockSpec((1, NKV, LANES, S), lambda b, h, i: (b, h, 0, 0))],
        out_specs=pl.BlockSpec((1, tq, heads * vd), lambda b, h, i: (b, i, h)),
        out_shape=jax.ShapeDtypeStruct((B, S, H * vd), BF16),
        scratch_shapes=[pltpu.VMEM((1, M), F32), pltpu.VMEM((LANES, M), F32), pltpu.VMEM((LANES, M), F32)],
        compiler_params=_cparams(("parallel", "parallel", "parallel")),
        name="flash_attn",
    )(qt, k, vt)


def _window_kernel(sink_ref, qt_ref, k_ref, vt_ref, o_ref, *, G, tq, span, S, vd):
    kvh = pl.program_id(1)
    start = pl.program_id(2) * tq
    for j in range(tq // LANES):
        q0 = start + j * LANES
        kstart = pl.multiple_of(jnp.clip(q0 - WINDOW, 0, S - span), LANES)
        k = k_ref[0, 0, pl.ds(kstart, span), :]
        vt = vt_ref[0, 0, :, pl.ds(kstart, span)]
        kpos = kstart + lax.broadcasted_iota(jnp.int32, (span, 1), 0)
        qpos = q0 + lax.broadcasted_iota(jnp.int32, (1, LANES), 1)
        valid = jnp.abs(qpos - kpos) <= WINDOW
        outs = []
        for g in range(G):
            st = jnp.dot(k, qt_ref[0, g, :, j * LANES:(j + 1) * LANES], preferred_element_type=F32)
            st = jnp.where(valid, st, NEG_INF)
            sink = sink_ref[kvh * G + g] * LOG2E
            m = jnp.maximum(jnp.max(st, axis=0, keepdims=True), sink)
            pt = jnp.exp2(st - m).astype(BF16)
            acc = jnp.dot(vt, pt, preferred_element_type=F32)
            outs.append((acc / (acc[vd:vd + 1, :] + jnp.exp2(sink - m))).T)
        for p in range(G // 2):
            o_ref[0, j * LANES:(j + 1) * LANES, p * LANES:(p + 1) * LANES] = _pack_head_pair(
                outs[2 * p], outs[2 * p + 1], vd).astype(BF16)


def _window(qt, k, vt, sink, vd):
    B, H, _, S = qt.shape
    HK = k.shape[1]
    G = H // HK
    tq = min(WIN_Q, S)
    span = min(LANES + 2 * WINDOW, S)
    return pl.pallas_call(
        functools.partial(_window_kernel, G=G, tq=tq, span=span, S=S, vd=vd),
        grid_spec=pltpu.PrefetchScalarGridSpec(
            num_scalar_prefetch=1,
            grid=(B, HK, S // tq),
            in_specs=[pl.BlockSpec((1, G, LANES, tq), lambda b, h, i, sk: (b, h, 0, i)),
                      pl.BlockSpec((1, 1, S, LANES), lambda b, h, i, sk: (b, h, 0, 0)),
                      pl.BlockSpec((1, 1, LANES, S), lambda b, h, i, sk: (b, h, 0, 0))],
            out_specs=pl.BlockSpec((1, tq, G * vd), lambda b, h, i, sk: (b, i, h))),
        out_shape=jax.ShapeDtypeStruct((B, S, H * vd), BF16),
        compiler_params=_cparams(("parallel", "parallel", "parallel")),
        name="window_attn",
    )(sink, qt, k, vt)


def _out_ln_router_kernel(*refs, n_in):
    o_refs = refs[:n_in]
    w_refs = refs[n_in:2 * n_in]
    x_ref, g_ref, b_ref, wrh_ref, wrl_ref, x1_ref, x1b_ref, aff_ref = refs[2 * n_in:]
    h = jnp.dot(o_refs[0][0], w_refs[0][...], preferred_element_type=F32)
    for i in range(1, n_in):
        h = h + jnp.dot(o_refs[i][0], w_refs[i][...], preferred_element_type=F32)
    x1 = _layer_norm(ALPHA * x_ref[0] + h, g_ref[...], b_ref[...])
    x1_ref[0] = x1
    x_hi = x1.astype(BF16)
    x1b_ref[0] = x_hi
    x_lo = (x1 - x_hi.astype(F32)).astype(BF16)
    logits = (lax.dot_general(wrh_ref[...], x_hi, NT_DIMS, preferred_element_type=F32)
              + lax.dot_general(wrh_ref[...], x_lo, NT_DIMS, preferred_element_type=F32)
              + lax.dot_general(wrl_ref[...], x_hi, NT_DIMS, preferred_element_type=F32))
    z = jnp.exp(logits - jnp.max(logits, axis=0, keepdims=True))
    aff_ref[0] = z / jnp.sum(z, axis=0, keepdims=True)


def _out_ln_router(os_, ws_, x, g, b, wr_hi, wr_lo):
    B, S, D = x.shape
    E = wr_hi.shape[0]
    tm = min(PROJ_ROWS, S)
    n_in = len(os_)
    row = lambda bb, i: (bb, i, 0)
    const = lambda bb, i: (0, 0)
    in_specs = ([pl.BlockSpec((1, tm, o.shape[2]), row) for o in os_]
                + [pl.BlockSpec(w.shape, const) for w in ws_]
                + [pl.BlockSpec((1, tm, D), row), pl.BlockSpec((1, D), const), pl.BlockSpec((1, D), const),
                   pl.BlockSpec((E, D), const), pl.BlockSpec((E, D), const)])
    return pl.pallas_call(
        functools.partial(_out_ln_router_kernel, n_in=n_in),
        grid=(B, S // tm),
        in_specs=in_specs,
        out_specs=[pl.BlockSpec((1, tm, D), row), pl.BlockSpec((1, tm, D), row),
                   pl.BlockSpec((1, E, tm), lambda bb, i: (bb, 0, i))],
        out_shape=[jax.ShapeDtypeStruct((B, S, D), F32), jax.ShapeDtypeStruct((B, S, D), BF16),
                   jax.ShapeDtypeStruct((B, E, S), F32)],
        compiler_params=_cparams(("parallel", "parallel")),
        name="out_ln_router",
    )(*os_, *ws_, x, g, b, wr_hi, wr_lo)


def _topk_kernel(aff_ref, tri_ref, rel_ref, cnt_ref, *, S, E, cap, T, CH):
    aff = aff_ref[0]
    bits = pltpu.bitcast(aff, jnp.int32)
    capf = jnp.float32(cap)

    def count(mask):
        return jnp.sum(jnp.where(mask, 1.0, 0.0), axis=1, keepdims=True)

    def thr_body(i, t):
        cand = t | jnp.left_shift(jnp.int32(1), 30 - i)
        return jnp.where(count(bits >= cand) >= capf, cand, t)

    thr = lax.fori_loop(0, 31, thr_body, jnp.zeros((E, 1), jnp.int32))
    gt = bits > thr
    ties = bits == thr
    need = capf - count(gt)
    idx = lax.broadcasted_iota(jnp.int32, (E, S), 1)
    nbits = max(1, (S - 1).bit_length())

    def cut_body(i, c):
        cand = c | jnp.left_shift(jnp.int32(1), nbits - 1 - i)
        return jnp.where(count(ties & (idx < cand)) < need, cand, c)

    cut = lax.fori_loop(0, nbits, cut_body, jnp.zeros((E, 1), jnp.int32))
    sel = gt | (ties & (idx <= cut))

    tri = tri_ref[...]
    nsc = S // T
    lane_sc = lax.broadcasted_iota(jnp.int32, (E, nsc), 1)
    cnt = jnp.zeros((E, nsc), F32)
    for sc in range(nsc):
        run = jnp.zeros((E, 1), F32)
        for ch in range(T // CH):
            lo = sc * T + ch * CH
            selc = jnp.where(sel[:, lo:lo + CH], 1.0, 0.0)
            incl = jnp.dot(selc.astype(BF16), tri, preferred_element_type=F32)
            rel = jnp.where(selc > 0.0, incl - 1.0 + run, -1.0)
            rel_ref[0, :, lo:lo + CH] = rel
            run = run + incl[:, CH - 1:CH]
        cnt = jnp.where(lane_sc == sc, run, cnt)
    cnt_ref[0] = cnt


def _topk(aff, T):
    B, E, S = aff.shape
    cap = EC_FACTOR * S // N_EXPERTS
    CH = min(CUMSUM_CHUNK, T)
    r = lax.broadcasted_iota(jnp.int32, (CH, CH), 0)
    c = lax.broadcasted_iota(jnp.int32, (CH, CH), 1)
    tri = jnp.where(r <= c, 1.0, 0.0).astype(BF16)
    nsc = S // T
    return pl.pallas_call(
        functools.partial(_topk_kernel, S=S, E=E, cap=cap, T=T, CH=CH),
        grid=(B,),
        in_specs=[pl.BlockSpec((1, E, S), lambda b: (b, 0, 0)),
                  pl.BlockSpec((CH, CH), lambda b: (0, 0))],
        out_specs=[pl.BlockSpec((1, E, S), lambda b: (b, 0, 0)),
                   pl.BlockSpec((1, E, nsc), lambda b: (b, 0, 0))],
        out_shape=[jax.ShapeDtypeStruct((B, E, S), F32), jax.ShapeDtypeStruct((B, E, nsc), F32)],
        compiler_params=_cparams(("parallel",)),
        name="topk_select",
    )(aff, tri)


def _moe_kernel(nblk_ref, xb_ref, relt_ref, afft_ref, wg_ref, wu_ref, wd_ref, x1_ref, g_ref, b_ref,
                f_ref, *, T, RB, E, nsc, TS):
    b, sc, e = pl.program_id(0), pl.program_id(1), pl.program_id(2)

    @pl.when(e == 0)
    def _():
        f_ref[...] = jnp.zeros(f_ref.shape, F32)

    nb = nblk_ref[(b * nsc + sc) * E + e]
    relt = relt_ref[0, 0]
    afft = afft_ref[0, 0]

    row_tiles = [slice(ts * TS, (ts + 1) * TS) for ts in range(T // TS)]

    def slot_block(j, carry):
        slot = (j * RB + lax.broadcasted_iota(jnp.int32, (RB, 1), 0)).astype(F32)
        hit = relt == slot
        gate = jnp.sum(jnp.where(hit, afft, 0.0), axis=1, keepdims=True)
        onehot = jnp.where(hit, 1.0, 0.0).astype(BF16)
        xg = jnp.dot(onehot, xb_ref[0], preferred_element_type=F32).astype(BF16)
        hg = jnp.dot(xg, wg_ref[0], preferred_element_type=F32)
        hu = jnp.dot(xg, wu_ref[0], preferred_element_type=F32)
        h = (hg / (1.0 + jnp.exp(-hg)) * hu).astype(BF16)
        y = (jnp.dot(h, wd_ref[0], preferred_element_type=F32) * gate).astype(BF16)
        for rows in row_tiles:
            f_ref[0, rows, :] += lax.dot_general(onehot[:, rows], y, TN_DIMS, preferred_element_type=F32)
        return carry

    lax.fori_loop(0, nb, slot_block, 0)

    @pl.when(e == E - 1)
    def _():
        for rows in row_tiles:
            f_ref[0, rows, :] = _layer_norm(ALPHA * x1_ref[0, rows, :] + f_ref[0, rows, :], g_ref[...], b_ref[...])


def _moe(xb, relt, afft, nblk, wg, wu, wd, x1, g, b, T):
    B, S, D = xb.shape
    E, _, F = wg.shape
    nsc = S // T
    RB = MOE_SLOTS
    TS = min(512, T)
    return pl.pallas_call(
        functools.partial(_moe_kernel, T=T, RB=RB, E=E, nsc=nsc, TS=TS),
        grid_spec=pltpu.PrefetchScalarGridSpec(
            num_scalar_prefetch=1,
            grid=(B, nsc, E),
            in_specs=[pl.BlockSpec((1, T, D), lambda b, s, e, n: (b, s, 0)),
                      pl.BlockSpec((1, 1, 1, T), lambda b, s, e, n: (b, e, 0, s)),
                      pl.BlockSpec((1, 1, 1, T), lambda b, s, e, n: (b, e, 0, s)),
                      pl.BlockSpec((1, D, F), lambda b, s, e, n: (e, 0, 0)),
                      pl.BlockSpec((1, D, F), lambda b, s, e, n: (e, 0, 0)),
                      pl.BlockSpec((1, F, D), lambda b, s, e, n: (e, 0, 0)),
                      pl.BlockSpec((1, T, D), lambda b, s, e, n: (b, s, 0)),
                      pl.BlockSpec((1, D), lambda b, s, e, n: (0, 0)),
                      pl.BlockSpec((1, D), lambda b, s, e, n: (0, 0))],
            out_specs=pl.BlockSpec((1, T, D), lambda b, s, e, n: (b, s, 0))),
        out_shape=jax.ShapeDtypeStruct((B, S, D), F32),
        compiler_params=_cparams(("parallel", "parallel", "arbitrary")),
        name="moe_ffn",
    )(nblk, xb, relt, afft, wg, wu, wd, x1, g, b)


def _proj_mla_kernel(x_ref, wdt_ref, gq_ref, gkv_ref, wqt_ref, wkvt_ref, c_ref, s_ref, qt_ref, k_ref, vt_ref):
    xb = x_ref[0].astype(BF16)
    dt = lax.dot_general(wdt_ref[...], xb, NT_DIMS, preferred_element_type=F32)
    cq = _rms_t(dt[0:Q_LORA], gq_ref[...], Q_LORA).astype(BF16)
    ckv = _rms_t(dt[Q_LORA:Q_LORA + KV_LORA], gkv_ref[...], KV_LORA).astype(BF16)
    r2 = QK_ROPE // 2
    c, s = c_ref[...], s_ref[...]
    kr = dt[Q_LORA + KV_LORA:Q_LORA + KV_LORA + QK_ROPE]
    kr1, kr2 = _rope_t(kr[0:r2], kr[r2:], c, s)
    qt = jnp.dot(wqt_ref[...], cq, preferred_element_type=F32)
    kvt = jnp.dot(wkvt_ref[...], ckv, preferred_element_type=F32)
    scale = (QK_NOPE + QK_ROPE) ** -0.5 * LOG2E
    dq, dkv = QK_NOPE + QK_ROPE, QK_NOPE + V_DIM
    for h in range(MLA_HEADS):
        q = qt[h * dq:(h + 1) * dq]
        q1, q2 = _rope_t(q[QK_NOPE:QK_NOPE + r2], q[QK_NOPE + r2:], c, s)
        _store_qt(qt_ref, h, jnp.concatenate([q[0:QK_NOPE], q1, q2], axis=0) * scale)
        kv = kvt[h * dkv:(h + 1) * dkv]
        _store_k(k_ref, h, jnp.concatenate([kv[0:QK_NOPE], kr1, kr2], axis=0))
        _store_vt(vt_ref, h, kv[QK_NOPE:])


def _proj_mla(x, wdt, gq, gkv, wqt, wkvt, tabs):
    B, S, D = x.shape
    tm = min(PROJ_ROWS, S)
    H = MLA_HEADS
    const = lambda b, i: (0, 0)
    full = lambda a: pl.BlockSpec(a.shape, const)
    tab = lambda t: pl.BlockSpec((t.shape[0], tm), lambda b, i: (0, i))
    gq_b = jnp.broadcast_to(gq.reshape(-1, 1), (Q_LORA, tm))
    gkv_b = jnp.broadcast_to(gkv.reshape(-1, 1), (KV_LORA, tm))
    return pl.pallas_call(
        _proj_mla_kernel,
        grid=(B, S // tm),
        in_specs=[pl.BlockSpec((1, tm, D), lambda b, i: (b, i, 0)),
                  full(wdt), full(gq_b), full(gkv_b), full(wqt), full(wkvt), tab(tabs[0]), tab(tabs[1])],
        out_specs=[pl.BlockSpec((1, H, LANES, tm), lambda b, i: (b, 0, 0, i)),
                   pl.BlockSpec((1, H, tm, LANES), lambda b, i: (b, 0, i, 0)),
                   pl.BlockSpec((1, H, LANES, tm), lambda b, i: (b, 0, 0, i))],
        out_shape=[jax.ShapeDtypeStruct((B, H, LANES, S), BF16),
                   jax.ShapeDtypeStruct((B, H, S, LANES), BF16),
                   jax.ShapeDtypeStruct((B, H, LANES, S), BF16)],
        compiler_params=_cparams(("parallel", "parallel")),
        name="proj_mla",
    )(x, wdt, gq_b, gkv_b, wqt, wkvt, *tabs)


def _angles(pos, dim):
    freqs = ROPE_THETA ** (-(jnp.arange(0, dim, 2, dtype=F32) / dim))
    return pos[:, None] * freqs[None, :]


def _tables(S):
    t = jnp.arange(S)
    ar = _angles((t // GRID_W).astype(F32), HEAD_DIM // 2)
    ac = _angles((t % GRID_W).astype(F32), HEAD_DIM // 2)
    axial = jnp.concatenate([ar, ac], axis=1).T
    seq = _angles(t.astype(F32), HEAD_DIM).T
    latent = _angles(t.astype(F32), QK_ROPE).T
    return tuple((jnp.cos(a), jnp.sin(a)) for a in (axial, seq, latent))


def _split_bf16(w):
    hi = w.astype(BF16)
    return hi, (w - hi.astype(F32)).astype(BF16)


def _moe_layer(x1, x1b, aff, w_gate, w_up, w_down, g, b):
    B, S, D = x1.shape
    T = min(MOE_TOKENS, S)
    rel, cnt = _topk(aff, T)
    nblk = ((cnt.astype(jnp.int32) + MOE_SLOTS - 1) // MOE_SLOTS).transpose(0, 2, 1).reshape(-1)
    relt = rel.reshape(B, N_EXPERTS, 1, S)
    afft = aff.reshape(B, N_EXPERTS, 1, S)
    return _moe(x1b, relt, afft, nblk, w_gate.astype(BF16), w_up.astype(BF16), w_down.astype(BF16), x1, g, b, T)


def kernel(x, ab_w_in, ab_q_norm, ab_k_norm, ab_sink, ab_w_out, mla_w_down, mla_q_norm, mla_kv_norm,
           mla_w_uq, mla_w_ukv, mla_w_out, ln_mix_g, ln_mix_b, moe_router, moe_w_gate, moe_w_up, moe_w_down,
           ln_ffn_g, ln_ffn_b):
    B, S, D = x.shape
    tabs_a, tabs_b, tabs_m = _tables(S)
    row = lambda v: v.reshape(1, -1)

    qta, ka, vta, qtb, kb, vtb = _proj_ab(x, ab_w_in[0].T.astype(BF16), ab_q_norm[0], ab_k_norm[0], tabs_a, tabs_b)
    oa = _flash(qta, ka, vta, HEAD_DIM)
    ob = _window(qtb, kb, vtb, ab_sink[0], HEAD_DIM)
    n_a = A_HEADS * HEAD_DIM
    w_oa = ab_w_out[0][:n_a].astype(BF16)
    w_ob = ab_w_out[0][n_a:].astype(BF16)
    wr_hi, wr_lo = _split_bf16(moe_router[0].T)
    x1, x1b, aff = _out_ln_router([oa, ob], [w_oa, w_ob], x, row(ln_mix_g[0]), row(ln_mix_b[0]), wr_hi, wr_lo)
    x = _moe_layer(x1, x1b, aff, moe_w_gate[0], moe_w_up[0], moe_w_down[0], row(ln_ffn_g[0]), row(ln_ffn_b[0]))

    qt, k, vt = _proj_mla(x, mla_w_down[0].T.astype(BF16), mla_q_norm[0], mla_kv_norm[0],
                          mla_w_uq[0].T.astype(BF16), mla_w_ukv[0].T.astype(BF16), tabs_m)
    oc = _flash(qt, k, vt, V_DIM)
    w_oc = mla_w_out[0].astype(BF16)
    wr_hi, wr_lo = _split_bf16(moe_router[1].T)
    x1, x1b, aff = _out_ln_router([oc], [w_oc], x, row(ln_mix_g[1]), row(ln_mix_b[1]), wr_hi, wr_lo)
    x = _moe_layer(x1, x1b, aff, moe_w_gate[1], moe_w_up[1], moe_w_down[1], row(ln_ffn_g[1]), row(ln_ffn_b[1]))
    return x
```

```python
import functools

import jax
import jax.numpy as jnp
from jax import lax
from jax.experimental import pallas as pl
from jax.experimental.pallas import tpu as pltpu

F32 = jnp.float32
BF16 = jnp.bfloat16

GRID_W = 64
ROPE_THETA = 10000.0
HEAD_DIM = 64
A_HEADS, A_KV = 8, 2
B_HEADS, B_KV = 8, 2
WINDOW = 128
MLA_HEADS = 16
Q_LORA, KV_LORA = 256, 128
QK_NOPE, QK_ROPE, V_DIM = 64, 32, 64
N_EXPERTS = 16
EC_FACTOR = 2
DEPTH = 2
ALPHA = (2.0 * DEPTH) ** 0.25
NEG_INF = -1e30
SHIFT_SLACK = 64.0
RMS_EPS = 1e-6
LOG2E = 1.4426950408889634
LN_EPS = 1e-5

LANES = 128
VMEM_LIMIT = 56 * 1024 * 1024

PROJ_ROWS = 512
ATTN_ROWS = 4096
ATTN_KEYS = 2048
ATTN_KEY_SUB = 512
ATTN_COLS = 256
AHEAD = 2
WIN_Q = 1024
MOE_TOKENS = 1024
MOE_SLOTS = 144
CUMSUM_CHUNK = 256

NT_DIMS = (((1,), (1,)), ((), ()))
TN_DIMS = (((0,), (0,)), ((), ()))


def _cparams(sem):
    return pltpu.CompilerParams(dimension_semantics=sem, vmem_limit_bytes=VMEM_LIMIT)


def _rope_t(x1, x2, c, s):
    return x1 * c - x2 * s, x2 * c + x1 * s


def _rms_t(x, g, n):
    ms = jnp.sum(x * x, axis=0, keepdims=True) * (1.0 / n)
    return x * lax.rsqrt(ms + RMS_EPS) * g


def _layer_norm(y, g, b):
    mu = jnp.mean(y, axis=1, keepdims=True)
    d = y - mu
    var = jnp.mean(d * d, axis=1, keepdims=True)
    return d * lax.rsqrt(var + LN_EPS) * g + b


def _pack_head_pair(a, b, vd):
    lane = lax.broadcasted_iota(jnp.int32, (1, LANES), 1)
    return jnp.where(lane < vd, a, pltpu.roll(b, vd, 1))


def _store_qt(ref, h, q):
    d = q.shape[0]
    ref[0, h, 0:d, :] = q.astype(BF16)
    ref[0, h, d:LANES, :] = jnp.zeros((LANES - d, q.shape[1]), BF16)


def _store_k(ref, h, kt):
    d, t = kt.shape
    ref[0, h] = jnp.concatenate([kt, jnp.zeros((LANES - d, t), F32)], axis=0).T.astype(BF16)


def _store_vt(ref, h, vt):
    d, t = vt.shape
    ref[0, h, 0:d, :] = vt.astype(BF16)
    row = lax.broadcasted_iota(jnp.int32, (LANES - d, t), 0)
    ref[0, h, d:LANES, :] = jnp.where(row == 0, 1.0, 0.0).astype(BF16)


def _proj_ab_kernel(x_ref, wt_ref, gq_ref, gk_ref, ca_ref, sa_ref, cb_ref, sb_ref,
                    qa_ref, ka_ref, va_ref, qb_ref, kb_ref, vb_ref):
    xb = x_ref[0].astype(BF16)
    pt = lax.dot_general(wt_ref[...], xb, NT_DIMS, preferred_element_type=F32)
    scale = HEAD_DIM ** -0.5 * LOG2E
    q4, h2 = HEAD_DIM // 4, HEAD_DIM // 2
    ca, sa, cb, sb = ca_ref[...], sa_ref[...], cb_ref[...], sb_ref[...]

    def rope_a(p):
        r1, r2 = _rope_t(p[0:q4], p[q4:2 * q4], ca[0:q4], sa[0:q4])
        c1, c2 = _rope_t(p[2 * q4:3 * q4], p[3 * q4:], ca[q4:], sa[q4:])
        return jnp.concatenate([r1, r2, c1, c2], axis=0)

    def rope_b(p):
        x1, x2 = _rope_t(p[0:h2], p[h2:], cb, sb)
        return jnp.concatenate([x1, x2], axis=0)

    heads = iter(pt[g * HEAD_DIM:(g + 1) * HEAD_DIM] for g in range(pt.shape[0] // HEAD_DIM))
    for h in range(A_HEADS):
        _store_qt(qa_ref, h, rope_a(_rms_t(next(heads), gq_ref[...], HEAD_DIM)) * scale)
    for h in range(A_KV):
        _store_k(ka_ref, h, rope_a(_rms_t(next(heads), gk_ref[...], HEAD_DIM)))
    for h in range(A_KV):
        _store_vt(va_ref, h, next(heads))
    for h in range(B_HEADS):
        _store_qt(qb_ref, h, rope_b(next(heads)) * scale)
    for h in range(B_KV):
        _store_k(kb_ref, h, rope_b(next(heads)))
    for h in range(B_KV):
        _store_vt(vb_ref, h, next(heads))


def _proj_ab(x, wt, gq, gk, tabs_a, tabs_b):
    B, S, D = x.shape
    tm = min(PROJ_ROWS, S)
    const = lambda b, i: (0, 0)
    tab = lambda t: pl.BlockSpec((t.shape[0], tm), lambda b, i: (0, i))
    hm = lambda h: pl.BlockSpec((1, h, tm, LANES), lambda b, i: (b, 0, i, 0))
    tr = lambda h: pl.BlockSpec((1, h, LANES, tm), lambda b, i: (b, 0, 0, i))
    sd = lambda h: jax.ShapeDtypeStruct((B, h, S, LANES), BF16)
    sdt = lambda h: jax.ShapeDtypeStruct((B, h, LANES, S), BF16)
    gq_b = jnp.broadcast_to(gq.reshape(-1, 1), (HEAD_DIM, tm))
    gk_b = jnp.broadcast_to(gk.reshape(-1, 1), (HEAD_DIM, tm))
    return pl.pallas_call(
        _proj_ab_kernel,
        grid=(B, S // tm),
        in_specs=[pl.BlockSpec((1, tm, D), lambda b, i: (b, i, 0)),
                  pl.BlockSpec(wt.shape, const),
                  pl.BlockSpec(gq_b.shape, const), pl.BlockSpec(gk_b.shape, const),
                  tab(tabs_a[0]), tab(tabs_a[1]), tab(tabs_b[0]), tab(tabs_b[1])],
        out_specs=[tr(A_HEADS), hm(A_KV), tr(A_KV), tr(B_HEADS), hm(B_KV), tr(B_KV)],
        out_shape=[sdt(A_HEADS), sd(A_KV), sdt(A_KV), sdt(B_HEADS), sd(B_KV), sdt(B_KV)],
        compiler_params=_cparams(("parallel", "parallel")),
        name="proj_ab",
    )(x, wt, gq_b, gk_b, *tabs_a, *tabs_b)


def _flash_kernel(qt_ref, k_ref, vt_ref, o_ref, m_sc, acc_sc, pv_sc, *, NKV, G, tq, tk, ks, nk, cw, vd):
    heads = NKV * G
    M = heads * tq
    n_sub = M // cw

    def q_tile(c):
        h, j = divmod(c * cw, tq)
        return qt_ref[0, h, :, j:j + cw]

    def kv_of(c):
        return (c * cw // tq) // G

    def cols(c):
        return slice(c * cw, (c + 1) * cw)

    for c in range(n_sub):
        s0 = jnp.dot(k_ref[0, kv_of(c), 0:LANES, :], q_tile(c), preferred_element_type=F32)
        m_sc[:, cols(c)] = jnp.max(s0, axis=0, keepdims=True)
    acc_sc[...] = jnp.zeros((LANES, M), F32)
    pv_sc[...] = jnp.zeros((LANES, M), F32)

    def body(kb, carry):
        off = pl.multiple_of(kb * tk, tk)
        acc_sc[...] += pv_sc[...]

        def keys(c, s):
            return k_ref[0, kv_of(c), pl.ds(pl.multiple_of(off + s * ks, ks), ks), :]

        def values_t(c, s=None):
            if s is None:
                return vt_ref[0, kv_of(c), :, pl.ds(off, tk)]
            return vt_ref[0, kv_of(c), :, pl.ds(pl.multiple_of(off + s * ks, ks), ks)]

        tiles = [(c, s) for c in range(n_sub) for s in range(tk // ks)]

        def scores(t):
            c, s = tiles[t]
            return jnp.dot(keys(c, s), q_tile(c), preferred_element_type=F32)

        excess = None
        ahead = {t: scores(t) for t in range(min(AHEAD, len(tiles)))}
        pts = []
        for t, (c, s) in enumerate(tiles):
            st = ahead.pop(t)
            if t + AHEAD < len(tiles):
                ahead[t + AHEAD] = scores(t + AHEAD)
            m = m_sc[:, cols(c)]
            over = jnp.max(st, axis=0, keepdims=True) - m
            excess = over if excess is None else jnp.maximum(excess, over)
            pts.append(jnp.exp2(st - m).astype(BF16))
            if s == tk // ks - 1:
                pt = pts[0] if len(pts) == 1 else jnp.concatenate(pts, axis=0)
                pv_sc[:, cols(c)] = jnp.dot(values_t(c), pt, preferred_element_type=F32)
                pts = []
        renew = jnp.max(excess) > SHIFT_SLACK

        @pl.when(renew)
        def _():
            pv_sc[...] = jnp.zeros((LANES, M), F32)
            for c, s in tiles:
                st = jnp.dot(keys(c, s), q_tile(c), preferred_element_type=F32)
                m_prev = m_sc[:, cols(c)]
                m_new = jnp.maximum(m_prev, jnp.max(st, axis=0, keepdims=True))
                alpha = jnp.exp2(m_prev - m_new)
                pt = jnp.exp2(st - m_new).astype(BF16)
                acc_sc[:, cols(c)] = (acc_sc[:, cols(c)] * alpha
                                      + jnp.dot(values_t(c, s), pt, preferred_element_type=F32))
                m_sc[:, cols(c)] = m_new

        return carry

    lax.fori_loop(0, nk, body, 0)
    acc = acc_sc[...] + pv_sc[...]
    o = acc / acc[vd:vd + 1, :]
    for p in range(heads // 2):
        o_ref[0, :, p * LANES:(p + 1) * LANES] = _pack_head_pair(
            o[:, 2 * p * tq:(2 * p + 1) * tq].T, o[:, (2 * p + 1) * tq:(2 * p + 2) * tq].T, vd).astype(BF16)


def _flash(qt, k, vt, vd):
    B, H, _, S = qt.shape
    HK = k.shape[1]
    G = H // HK
    NKV = max(1, 2 // G)
    heads = NKV * G
    tq = min(ATTN_ROWS // heads, S)
    tk = min(ATTN_KEYS, S)
    M = heads * tq
    cw = min(ATTN_COLS, tq)
    ks = min(ATTN_KEY_SUB, tk)
    return pl.pallas_call(
        functools.partial(_flash_kernel, NKV=NKV, G=G, tq=tq, tk=tk, ks=ks, nk=S // tk, cw=cw, vd=vd),
        grid=(B, HK // NKV, S // tq),
        in_specs=[pl.BlockSpec((1, heads, LANES, tq), lambda b, h, i: (b, h, 0, i)),
                  pl.BlockSpec((1, NKV, S, LANES), lambda b, h, i: (b, h, 0, 0)),
                  pl.BlockSpec((1, NKV, LANES, S), lambda b, h, i: (b, h, 0, 0))],
        out_specs=pl.BlockSpec((1, tq, heads * vd), lambda b, h, i: (b, i, h)),
        out_shape=jax.ShapeDtypeStruct((B, S, H * vd), BF16),
        scratch_shapes=[pltpu.VMEM((1, M), F32), pltpu.VMEM((LANES, M), F32), pltpu.VMEM((LANES, M), F32)],
        compiler_params=_cparams(("parallel", "parallel", "parallel")),
        name="flash_attn",
    )(qt, k, vt)


def _window_kernel(sink_ref, qt_ref, k_ref, vt_ref, o_ref, *, G, tq, span, S, vd):
    kvh = pl.program_id(1)
    start = pl.program_id(2) * tq
    for j in range(tq // LANES):
        q0 = start + j * LANES
        kstart = pl.multiple_of(jnp.clip(q0 - WINDOW, 0, S - span), LANES)
        k = k_ref[0, 0, pl.ds(kstart, span), :]
        vt = vt_ref[0, 0, :, pl.ds(kstart, span)]
        kpos = kstart + lax.broadcasted_iota(jnp.int32, (span, 1), 0)
        qpos = q0 + lax.broadcasted_iota(jnp.int32, (1, LANES), 1)
        valid = jnp.abs(qpos - kpos) <= WINDOW
        outs = []
        for g in range(G):
            st = jnp.dot(k, qt_ref[0, g, :, j * LANES:(j + 1) * LANES], preferred_element_type=F32)
            st = jnp.where(valid, st, NEG_INF)
            sink = sink_ref[kvh * G + g] * LOG2E
            m = jnp.maximum(jnp.max(st, axis=0, keepdims=True), sink)
            pt = jnp.exp2(st - m).astype(BF16)
            acc = jnp.dot(vt, pt, preferred_element_type=F32)
            outs.append((acc / (acc[vd:vd + 1, :] + jnp.exp2(sink - m))).T)
        for p in range(G // 2):
            o_ref[0, j * LANES:(j + 1) * LANES, p * LANES:(p + 1) * LANES] = _pack_head_pair(
                outs[2 * p], outs[2 * p + 1], vd).astype(BF16)


def _window(qt, k, vt, sink, vd):
    B, H, _, S = qt.shape
    HK = k.shape[1]
    G = H // HK
    tq = min(WIN_Q, S)
    span = min(LANES + 2 * WINDOW, S)
    return pl.pallas_call(
        functools.partial(_window_kernel, G=G, tq=tq, span=span, S=S, vd=vd),
        grid_spec=pltpu.PrefetchScalarGridSpec(
            num_scalar_prefetch=1,
            grid=(B, HK, S // tq),
            in_specs=[pl.BlockSpec((1, G, LANES, tq), lambda b, h, i, sk: (b, h, 0, i)),
                      pl.BlockSpec((1, 1, S, LANES), lambda b, h, i, sk: (b, h, 0, 0)),
                      pl.BlockSpec((1, 1, LANES, S), lambda b, h, i, sk: (b, h, 0, 0))],
            out_specs=pl.BlockSpec((1, tq, G * vd), lambda b, h, i, sk: (b, i, h))),
        out_shape=jax.ShapeDtypeStruct((B, S, H * vd), BF16),
        compiler_params=_cparams(("parallel", "parallel", "parallel")),
        name="window_attn",
    )(sink, qt, k, vt)


def _out_ln_router_kernel(*refs, n_in):
    o_refs = refs[:n_in]
    w_refs = refs[n_in:2 * n_in]
    x_ref, g_ref, b_ref, wrh_ref, wrl_ref, x1_ref, x1b_ref, aff_ref = refs[2 * n_in:]
    h = jnp.dot(o_refs[0][0], w_refs[0][...], preferred_element_type=F32)
    for i in range(1, n_in):
        h = h + jnp.dot(o_refs[i][0], w_refs[i][...], preferred_element_type=F32)
    x1 = _layer_norm(ALPHA * x_ref[0] + h, g_ref[...], b_ref[...])
    x1_ref[0] = x1
    x_hi = x1.astype(BF16)
    x1b_ref[0] = x_hi
    x_lo = (x1 - x_hi.astype(F32)).astype(BF16)
    logits = (lax.dot_general(wrh_ref[...], x_hi, NT_DIMS, preferred_element_type=F32)
              + lax.dot_general(wrh_ref[...], x_lo, NT_DIMS, preferred_element_type=F32)
              + lax.dot_general(wrl_ref[...], x_hi, NT_DIMS, preferred_element_type=F32))
    z = jnp.exp(logits - jnp.max(logits, axis=0, keepdims=True))
    aff_ref[0] = z / jnp.sum(z, axis=0, keepdims=True)


def _out_ln_router(os_, ws_, x, g, b, wr_hi, wr_lo):
    B, S, D = x.shape
    E = wr_hi.shape[0]
    tm = min(PROJ_ROWS, S)
    n_in = len(os_)
    row = lambda bb, i: (bb, i, 0)
    const = lambda bb, i: (0, 0)
    in_specs = ([pl.BlockSpec((1, tm, o.shape[2]), row) for o in os_]
                + [pl.BlockSpec(w.shape, const) for w in ws_]
                + [pl.BlockSpec((1, tm, D), row), pl.BlockSpec((1, D), const), pl.BlockSpec((1, D), const),
                   pl.BlockSpec((E, D), const), pl.BlockSpec((E, D), const)])
    return pl.pallas_call(
        functools.partial(_out_ln_router_kernel, n_in=n_in),
        grid=(B, S // tm),
        in_specs=in_specs,
        out_specs=[pl.BlockSpec((1, tm, D), row), pl.BlockSpec((1, tm, D), row),
                   pl.BlockSpec((1, E, tm), lambda bb, i: (bb, 0, i))],
        out_shape=[jax.ShapeDtypeStruct((B, S, D), F32), jax.ShapeDtypeStruct((B, S, D), BF16),
                   jax.ShapeDtypeStruct((B, E, S), F32)],
        compiler_params=_cparams(("parallel", "parallel")),
        name="out_ln_router",
    )(*os_, *ws_, x, g, b, wr_hi, wr_lo)


def _topk_kernel(aff_ref, tri_ref, rel_ref, cnt_ref, *, S, E, cap, T, CH):
    aff = aff_ref[0]
    bits = pltpu.bitcast(aff, jnp.int32)
    capf = jnp.float32(cap)

    def count(mask):
        return jnp.sum(jnp.where(mask, 1.0, 0.0), axis=1, keepdims=True)

    def thr_body(i, t):
        cand = t | jnp.left_shift(jnp.int32(1), 30 - i)
        return jnp.where(count(bits >= cand) >= capf, cand, t)

    thr = lax.fori_loop(0, 31, thr_body, jnp.zeros((E, 1), jnp.int32))
    gt = bits > thr
    ties = bits == thr
    need = capf - count(gt)
    idx = lax.broadcasted_iota(jnp.int32, (E, S), 1)
    nbits = max(1, (S - 1).bit_length())

    def cut_body(i, c):
        cand = c | jnp.left_shift(jnp.int32(1), nbits - 1 - i)
        return jnp.where(count(ties & (idx < cand)) < need, cand, c)

    cut = lax.fori_loop(0, nbits, cut_body, jnp.zeros((E, 1), jnp.int32))
    sel = gt | (ties & (idx <= cut))

    tri = tri_ref[...]
    nsc = S // T
    lane_sc = lax.broadcasted_iota(jnp.int32, (E, nsc), 1)
    cnt = jnp.zeros((E, nsc), F32)
    for sc in range(nsc):
        run = jnp.zeros((E, 1), F32)
        for ch in range(T // CH):
            lo = sc * T + ch * CH
            selc = jnp.where(sel[:, lo:lo + CH], 1.0, 0.0)
            incl = jnp.dot(selc.astype(BF16), tri, preferred_element_type=F32)
            rel = jnp.where(selc > 0.0, incl - 1.0 + run, -1.0)
            rel_ref[0, :, lo:lo + CH] = rel
            run = run + incl[:, CH - 1:CH]
        cnt = jnp.where(lane_sc == sc, run, cnt)
    cnt_ref[0] = cnt


def _topk(aff, T):
    B, E, S = aff.shape
    cap = EC_FACTOR * S // N_EXPERTS
    CH = min(CUMSUM_CHUNK, T)
    r = lax.broadcasted_iota(jnp.int32, (CH, CH), 0)
    c = lax.broadcasted_iota(jnp.int32, (CH, CH), 1)
    tri = jnp.where(r <= c, 1.0, 0.0).astype(BF16)
    nsc = S // T
    return pl.pallas_call(
        functools.partial(_topk_kernel, S=S, E=E, cap=cap, T=T, CH=CH),
        grid=(B,),
        in_specs=[pl.BlockSpec((1, E, S), lambda b: (b, 0, 0)),
                  pl.BlockSpec((CH, CH), lambda b: (0, 0))],
        out_specs=[pl.BlockSpec((1, E, S), lambda b: (b, 0, 0)),
                   pl.BlockSpec((1, E, nsc), lambda b: (b, 0, 0))],
        out_shape=[jax.ShapeDtypeStruct((B, E, S), F32), jax.ShapeDtypeStruct((B, E, nsc), F32)],
        compiler_params=_cparams(("parallel",)),
        name="topk_select",
    )(aff, tri)


def _moe_kernel(nblk_ref, xb_ref, relt_ref, afft_ref, rel_all_ref, wg_ref, wu_ref, wd_ref, x1_ref, g_ref, b_ref,
                f_ref, y_all, *, T, RB, E, nsc, TS):
    b, sc, e = pl.program_id(0), pl.program_id(1), pl.program_id(2)

    @pl.when(e == 0)
    def _():
        f_ref[...] = jnp.zeros(f_ref.shape, F32)

    nb = nblk_ref[(b * nsc + sc) * E + e]
    relt = relt_ref[0, 0]
    afft = afft_ref[0, 0]
    row_tiles = [slice(ts * TS, (ts + 1) * TS) for ts in range(T // TS)]
    first_rows = pl.ds(pl.multiple_of(e * RB, RB), RB)

    def expert_rows(j):
        slot = (j * RB + lax.broadcasted_iota(jnp.int32, (RB, 1), 0)).astype(F32)
        hit = relt == slot
        gate = jnp.sum(jnp.where(hit, afft, 0.0), axis=1, keepdims=True)
        onehot = jnp.where(hit, 1.0, 0.0).astype(BF16)
        xg = jnp.dot(onehot, xb_ref[0], preferred_element_type=F32).astype(BF16)
        hg = jnp.dot(xg, wg_ref[0], preferred_element_type=F32)
        hu = jnp.dot(xg, wu_ref[0], preferred_element_type=F32)
        h = (hg / (1.0 + jnp.exp(-hg)) * hu).astype(BF16)
        return onehot, (jnp.dot(h, wd_ref[0], preferred_element_type=F32) * gate).astype(BF16)

    @pl.when(nb == 0)
    def _():
        y_all[first_rows, :] = jnp.zeros((RB, y_all.shape[1]), BF16)

    @pl.when(nb > 0)
    def _():
        y_all[first_rows, :] = expert_rows(0)[1]

    def later_block(j, carry):
        onehot, y = expert_rows(j)
        for rows in row_tiles:
            f_ref[0, rows, :] += lax.dot_general(onehot[:, rows], y, TN_DIMS, preferred_element_type=F32)
        return carry

    lax.fori_loop(1, nb, later_block, 0)

    @pl.when(e == E - 1)
    def _():
        slot = lax.broadcasted_iota(jnp.int32, (RB, 1), 0).astype(F32)
        for rows in row_tiles:
            onehot_all = jnp.concatenate(
                [jnp.where(rel_all_ref[0, ee, :, rows] == slot, 1.0, 0.0).astype(BF16) for ee in range(E)], axis=0)
            f = f_ref[0, rows, :] + lax.dot_general(onehot_all, y_all[...], TN_DIMS, preferred_element_type=F32)
            f_ref[0, rows, :] = _layer_norm(ALPHA * x1_ref[0, rows, :] + f, g_ref[...], b_ref[...])


def _moe(xb, relt, afft, nblk, wg, wu, wd, x1, g, b, T):
    B, S, D = xb.shape
    E, _, F = wg.shape
    nsc = S // T
    RB = MOE_SLOTS
    TS = min(512, T)
    return pl.pallas_call(
        functools.partial(_moe_kernel, T=T, RB=RB, E=E, nsc=nsc, TS=TS),
        grid_spec=pltpu.PrefetchScalarGridSpec(
            num_scalar_prefetch=1,
            grid=(B, nsc, E),
            in_specs=[pl.BlockSpec((1, T, D), lambda b, s, e, n: (b, s, 0)),
                      pl.BlockSpec((1, 1, 1, T), lambda b, s, e, n: (b, e, 0, s)),
                      pl.BlockSpec((1, 1, 1, T), lambda b, s, e, n: (b, e, 0, s)),
                      pl.BlockSpec((1, E, 1, T), lambda b, s, e, n: (b, 0, 0, s)),
                      pl.BlockSpec((1, D, F), lambda b, s, e, n: (e, 0, 0)),
                      pl.BlockSpec((1, D, F), lambda b, s, e, n: (e, 0, 0)),
                      pl.BlockSpec((1, F, D), lambda b, s, e, n: (e, 0, 0)),
                      pl.BlockSpec((1, T, D), lambda b, s, e, n: (b, s, 0)),
                      pl.BlockSpec((1, D), lambda b, s, e, n: (0, 0)),
                      pl.BlockSpec((1, D), lambda b, s, e, n: (0, 0))],
            out_specs=pl.BlockSpec((1, T, D), lambda b, s, e, n: (b, s, 0)),
            scratch_shapes=[pltpu.VMEM((E * RB, D), BF16)]),
        out_shape=jax.ShapeDtypeStruct((B, S, D), F32),
        compiler_params=_cparams(("parallel", "parallel", "arbitrary")),
        name="moe_ffn",
    )(nblk, xb, relt, afft, relt, wg, wu, wd, x1, g, b)


def _proj_mla_kernel(x_ref, wdt_ref, gq_ref, gkv_ref, wqt_ref, wkvt_ref, c_ref, s_ref, qt_ref, k_ref, vt_ref):
    xb = x_ref[0].astype(BF16)
    dt = lax.dot_general(wdt_ref[...], xb, NT_DIMS, preferred_element_type=F32)
    cq = _rms_t(dt[0:Q_LORA], gq_ref[...], Q_LORA).astype(BF16)
    ckv = _rms_t(dt[Q_LORA:Q_LORA + KV_LORA], gkv_ref[...], KV_LORA).astype(BF16)
    r2 = QK_ROPE // 2
    c, s = c_ref[...], s_ref[...]
    kr = dt[Q_LORA + KV_LORA:Q_LORA + KV_LORA + QK_ROPE]
    kr1, kr2 = _rope_t(kr[0:r2], kr[r2:], c, s)
    qt = jnp.dot(wqt_ref[...], cq, preferred_element_type=F32)
    kvt = jnp.dot(wkvt_ref[...], ckv, preferred_element_type=F32)
    scale = (QK_NOPE + QK_ROPE) ** -0.5 * LOG2E
    dq, dkv = QK_NOPE + QK_ROPE, QK_NOPE + V_DIM
    for h in range(MLA_HEADS):
        q = qt[h * dq:(h + 1) * dq]
        q1, q2 = _rope_t(q[QK_NOPE:QK_NOPE + r2], q[QK_NOPE + r2:], c, s)
        _store_qt(qt_ref, h, jnp.concatenate([q[0:QK_NOPE], q1, q2], axis=0) * scale)
        kv = kvt[h * dkv:(h + 1) * dkv]
        _store_k(k_ref, h, jnp.concatenate([kv[0:QK_NOPE], kr1, kr2], axis=0))
        _store_vt(vt_ref, h, kv[QK_NOPE:])


def _proj_mla(x, wdt, gq, gkv, wqt, wkvt, tabs):
    B, S, D = x.shape
    tm = min(PROJ_ROWS, S)
    H = MLA_HEADS
    const = lambda b, i: (0, 0)
    full = lambda a: pl.BlockSpec(a.shape, const)
    tab = lambda t: pl.BlockSpec((t.shape[0], tm), lambda b, i: (0, i))
    gq_b = jnp.broadcast_to(gq.reshape(-1, 1), (Q_LORA, tm))
    gkv_b = jnp.broadcast_to(gkv.reshape(-1, 1), (KV_LORA, tm))
    return pl.pallas_call(
        _proj_mla_kernel,
        grid=(B, S // tm),
        in_specs=[pl.BlockSpec((1, tm, D), lambda b, i: (b, i, 0)),
                  full(wdt), full(gq_b), full(gkv_b), full(wqt), full(wkvt), tab(tabs[0]), tab(tabs[1])],
        out_specs=[pl.BlockSpec((1, H, LANES, tm), lambda b, i: (b, 0, 0, i)),
                   pl.BlockSpec((1, H, tm, LANES), lambda b, i: (b, 0, i, 0)),
                   pl.BlockSpec((1, H, LANES, tm), lambda b, i: (b, 0, 0, i))],
        out_shape=[jax.ShapeDtypeStruct((B, H, LANES, S), BF16),
                   jax.ShapeDtypeStruct((B, H, S, LANES), BF16),
                   jax.ShapeDtypeStruct((B, H, LANES, S), BF16)],
        compiler_params=_cparams(("parallel", "parallel")),
        name="proj_mla",
    )(x, wdt, gq_b, gkv_b, wqt, wkvt, *tabs)


def _angles(pos, dim):
    freqs = ROPE_THETA ** (-(jnp.arange(0, dim, 2, dtype=F32) / dim))
    return pos[:, None] * freqs[None, :]


def _tables(S):
    t = jnp.arange(S)
    ar = _angles((t // GRID_W).astype(F32), HEAD_DIM // 2)
    ac = _angles((t % GRID_W).astype(F32), HEAD_DIM // 2)
    axial = jnp.concatenate([ar, ac], axis=1).T
    seq = _angles(t.astype(F32), HEAD_DIM).T
    latent = _angles(t.astype(F32), QK_ROPE).T
    return tuple((jnp.cos(a), jnp.sin(a)) for a in (axial, seq, latent))


def _split_bf16(w):
    hi = w.astype(BF16)
    return hi, (w - hi.astype(F32)).astype(BF16)


def _moe_layer(x1, x1b, aff, w_gate, w_up, w_down, g, b):
    B, S, D = x1.shape
    T = min(MOE_TOKENS, S)
    rel, cnt = _topk(aff, T)
    nblk = ((cnt.astype(jnp.int32) + MOE_SLOTS - 1) // MOE_SLOTS).transpose(0, 2, 1).reshape(-1)
    relt = rel.reshape(B, N_EXPERTS, 1, S)
    afft = aff.reshape(B, N_EXPERTS, 1, S)
    return _moe(x1b, relt, afft, nblk, w_gate.astype(BF16), w_up.astype(BF16), w_down.astype(BF16), x1, g, b, T)


def kernel(x, ab_w_in, ab_q_norm, ab_k_norm, ab_sink, ab_w_out, mla_w_down, mla_q_norm, mla_kv_norm,
           mla_w_uq, mla_w_ukv, mla_w_out, ln_mix_g, ln_mix_b, moe_router, moe_w_gate, moe_w_up, moe_w_down,
           ln_ffn_g, ln_ffn_b):
    B, S, D = x.shape
    tabs_a, tabs_b, tabs_m = _tables(S)
    row = lambda v: v.reshape(1, -1)

    qta, ka, vta, qtb, kb, vtb = _proj_ab(x, ab_w_in[0].T.astype(BF16), ab_q_norm[0], ab_k_norm[0], tabs_a, tabs_b)
    oa = _flash(qta, ka, vta, HEAD_DIM)
    ob = _window(qtb, kb, vtb, ab_sink[0], HEAD_DIM)
    n_a = A_HEADS * HEAD_DIM
    w_oa = ab_w_out[0][:n_a].astype(BF16)
    w_ob = ab_w_out[0][n_a:].astype(BF16)
    wr_hi, wr_lo = _split_bf16(moe_router[0].T)
    x1, x1b, aff = _out_ln_router([oa, ob], [w_oa, w_ob], x, row(ln_mix_g[0]), row(ln_mix_b[0]), wr_hi, wr_lo)
    x = _moe_layer(x1, x1b, aff, moe_w_gate[0], moe_w_up[0], moe_w_down[0], row(ln_ffn_g[0]), row(ln_ffn_b[0]))

    qt, k, vt = _proj_mla(x, mla_w_down[0].T.astype(BF16), mla_q_norm[0], mla_kv_norm[0],
                          mla_w_uq[0].T.astype(BF16), mla_w_ukv[0].T.astype(BF16), tabs_m)
    oc = _flash(qt, k, vt, V_DIM)
    w_oc = mla_w_out[0].astype(BF16)
    wr_hi, wr_lo = _split_bf16(moe_router[1].T)
    x1, x1b, aff = _out_ln_router([oc], [w_oc], x, row(ln_mix_g[1]), row(ln_mix_b[1]), wr_hi, wr_lo)
    x = _moe_layer(x1, x1b, aff, moe_w_gate[1], moe_w_up[1], moe_w_down[1], row(ln_ffn_g[1]), row(ln_ffn_b[1]))
    return x
```

```python
import functools

import jax
import jax.numpy as jnp
from jax import lax
from jax.experimental import pallas as pl
from jax.experimental.pallas import tpu as pltpu

F32 = jnp.float32
BF16 = jnp.bfloat16

GRID_W = 64
ROPE_THETA = 10000.0
HEAD_DIM = 64
A_HEADS, A_KV = 8, 2
B_HEADS, B_KV = 8, 2
WINDOW = 128
MLA_HEADS = 16
Q_LORA, KV_LORA = 256, 128
QK_NOPE, QK_ROPE, V_DIM = 64, 32, 64
N_EXPERTS = 16
EC_FACTOR = 2
DEPTH = 2
ALPHA = (2.0 * DEPTH) ** 0.25
NEG_INF = -1e30
SHIFT_SLACK = 64.0
RMS_EPS = 1e-6
LOG2E = 1.4426950408889634
LN_EPS = 1e-5

LANES = 128
VMEM_LIMIT = 56 * 1024 * 1024

PROJ_ROWS = 512
ATTN_ROWS = 4096
ATTN_KEYS = 2048
ATTN_KEY_SUB = 512
ATTN_COLS = 256
AHEAD = 2
WIN_Q = 1024
MOE_TOKENS = 1024
MOE_TILES = 2
MOE_SLOTS = 144
CUMSUM_CHUNK = 256

NT_DIMS = (((1,), (1,)), ((), ()))
TN_DIMS = (((0,), (0,)), ((), ()))


def _cparams(sem):
    return pltpu.CompilerParams(dimension_semantics=sem, vmem_limit_bytes=VMEM_LIMIT)


def _rope_t(x1, x2, c, s):
    return x1 * c - x2 * s, x2 * c + x1 * s


def _rms_t(x, g, n):
    ms = jnp.sum(x * x, axis=0, keepdims=True) * (1.0 / n)
    return x * lax.rsqrt(ms + RMS_EPS) * g


def _layer_norm(y, g, b):
    mu = jnp.mean(y, axis=1, keepdims=True)
    d = y - mu
    var = jnp.mean(d * d, axis=1, keepdims=True)
    return d * lax.rsqrt(var + LN_EPS) * g + b


def _pack_head_pair(a, b, vd):
    lane = lax.broadcasted_iota(jnp.int32, (1, LANES), 1)
    return jnp.where(lane < vd, a, pltpu.roll(b, vd, 1))


def _store_qt(ref, h, q):
    d = q.shape[0]
    ref[0, h, 0:d, :] = q.astype(BF16)
    ref[0, h, d:LANES, :] = jnp.zeros((LANES - d, q.shape[1]), BF16)


def _store_k(ref, h, kt):
    d, t = kt.shape
    ref[0, h] = jnp.concatenate([kt, jnp.zeros((LANES - d, t), F32)], axis=0).T.astype(BF16)


def _store_vt(ref, h, vt):
    d, t = vt.shape
    ref[0, h, 0:d, :] = vt.astype(BF16)
    row = lax.broadcasted_iota(jnp.int32, (LANES - d, t), 0)
    ref[0, h, d:LANES, :] = jnp.where(row == 0, 1.0, 0.0).astype(BF16)


def _proj_ab_kernel(x_ref, wt_ref, gq_ref, gk_ref, ca_ref, sa_ref, cb_ref, sb_ref,
                    qa_ref, ka_ref, va_ref, qb_ref, kb_ref, vb_ref):
    xb = x_ref[0].astype(BF16)
    pt = lax.dot_general(wt_ref[...], xb, NT_DIMS, preferred_element_type=F32)
    scale = HEAD_DIM ** -0.5 * LOG2E
    q4, h2 = HEAD_DIM // 4, HEAD_DIM // 2
    ca, sa, cb, sb = ca_ref[...], sa_ref[...], cb_ref[...], sb_ref[...]

    def rope_a(p):
        r1, r2 = _rope_t(p[0:q4], p[q4:2 * q4], ca[0:q4], sa[0:q4])
        c1, c2 = _rope_t(p[2 * q4:3 * q4], p[3 * q4:], ca[q4:], sa[q4:])
        return jnp.concatenate([r1, r2, c1, c2], axis=0)

    def rope_b(p):
        x1, x2 = _rope_t(p[0:h2], p[h2:], cb, sb)
        return jnp.concatenate([x1, x2], axis=0)

    heads = iter(pt[g * HEAD_DIM:(g + 1) * HEAD_DIM] for g in range(pt.shape[0] // HEAD_DIM))
    for h in range(A_HEADS):
        _store_qt(qa_ref, h, rope_a(_rms_t(next(heads), gq_ref[...], HEAD_DIM)) * scale)
    for h in range(A_KV):
        _store_k(ka_ref, h, rope_a(_rms_t(next(heads), gk_ref[...], HEAD_DIM)))
    for h in range(A_KV):
        _store_vt(va_ref, h, next(heads))
    for h in range(B_HEADS):
        _store_qt(qb_ref, h, rope_b(next(heads)) * scale)
    for h in range(B_KV):
        _store_k(kb_ref, h, rope_b(next(heads)))
    for h in range(B_KV):
        _store_vt(vb_ref, h, next(heads))


def _proj_ab(x, wt, gq, gk, tabs_a, tabs_b):
    B, S, D = x.shape
    tm = min(PROJ_ROWS, S)
    const = lambda b, i: (0, 0)
    tab = lambda t: pl.BlockSpec((t.shape[0], tm), lambda b, i: (0, i))
    hm = lambda h: pl.BlockSpec((1, h, tm, LANES), lambda b, i: (b, 0, i, 0))
    tr = lambda h: pl.BlockSpec((1, h, LANES, tm), lambda b, i: (b, 0, 0, i))
    sd = lambda h: jax.ShapeDtypeStruct((B, h, S, LANES), BF16)
    sdt = lambda h: jax.ShapeDtypeStruct((B, h, LANES, S), BF16)
    gq_b = jnp.broadcast_to(gq.reshape(-1, 1), (HEAD_DIM, tm))
    gk_b = jnp.broadcast_to(gk.reshape(-1, 1), (HEAD_DIM, tm))
    return pl.pallas_call(
        _proj_ab_kernel,
        grid=(B, S // tm),
        in_specs=[pl.BlockSpec((1, tm, D), lambda b, i: (b, i, 0)),
                  pl.BlockSpec(wt.shape, const),
                  pl.BlockSpec(gq_b.shape, const), pl.BlockSpec(gk_b.shape, const),
                  tab(tabs_a[0]), tab(tabs_a[1]), tab(tabs_b[0]), tab(tabs_b[1])],
        out_specs=[tr(A_HEADS), hm(A_KV), tr(A_KV), tr(B_HEADS), hm(B_KV), tr(B_KV)],
        out_shape=[sdt(A_HEADS), sd(A_KV), sdt(A_KV), sdt(B_HEADS), sd(B_KV), sdt(B_KV)],
        compiler_params=_cparams(("parallel", "parallel")),
        name="proj_ab",
    )(x, wt, gq_b, gk_b, *tabs_a, *tabs_b)


def _flash_kernel(qt_ref, k_ref, vt_ref, o_ref, m_sc, acc_sc, pv_sc, *, NKV, G, tq, tk, ks, nk, cw, vd):
    heads = NKV * G
    M = heads * tq
    n_sub = M // cw

    def q_tile(c):
        h, j = divmod(c * cw, tq)
        return qt_ref[0, h, :, j:j + cw]

    def kv_of(c):
        return (c * cw // tq) // G

    def cols(c):
        return slice(c * cw, (c + 1) * cw)

    for c in range(n_sub):
        s0 = jnp.dot(k_ref[0, kv_of(c), 0:LANES, :], q_tile(c), preferred_element_type=F32)
        m_sc[:, cols(c)] = jnp.max(s0, axis=0, keepdims=True)
    acc_sc[...] = jnp.zeros((LANES, M), F32)
    pv_sc[...] = jnp.zeros((LANES, M), F32)

    def body(kb, carry):
        off = pl.multiple_of(kb * tk, tk)
        acc_sc[...] += pv_sc[...]

        def keys(c, s):
            return k_ref[0, kv_of(c), pl.ds(pl.multiple_of(off + s * ks, ks), ks), :]

        def values_t(c, s=None):
            if s is None:
                return vt_ref[0, kv_of(c), :, pl.ds(off, tk)]
            return vt_ref[0, kv_of(c), :, pl.ds(pl.multiple_of(off + s * ks, ks), ks)]

        tiles = [(c, s) for c in range(n_sub) for s in range(tk // ks)]

        def scores(t):
            c, s = tiles[t]
            return jnp.dot(keys(c, s), q_tile(c), preferred_element_type=F32)

        excess = None
        ahead = {t: scores(t) for t in range(min(AHEAD, len(tiles)))}
        pts = []
        for t, (c, s) in enumerate(tiles):
            st = ahead.pop(t)
            if t + AHEAD < len(tiles):
                ahead[t + AHEAD] = scores(t + AHEAD)
            m = m_sc[:, cols(c)]
            over = jnp.max(st, axis=0, keepdims=True) - m
            excess = over if excess is None else jnp.maximum(excess, over)
            pts.append(jnp.exp2(st - m).astype(BF16))
            if s == tk // ks - 1:
                pt = pts[0] if len(pts) == 1 else jnp.concatenate(pts, axis=0)
                pv_sc[:, cols(c)] = jnp.dot(values_t(c), pt, preferred_element_type=F32)
                pts = []
        renew = jnp.max(excess) > SHIFT_SLACK

        @pl.when(renew)
        def _():
            pv_sc[...] = jnp.zeros((LANES, M), F32)
            for c, s in tiles:
                st = jnp.dot(keys(c, s), q_tile(c), preferred_element_type=F32)
                m_prev = m_sc[:, cols(c)]
                m_new = jnp.maximum(m_prev, jnp.max(st, axis=0, keepdims=True))
                alpha = jnp.exp2(m_prev - m_new)
                pt = jnp.exp2(st - m_new).astype(BF16)
                acc_sc[:, cols(c)] = (acc_sc[:, cols(c)] * alpha
                                      + jnp.dot(values_t(c, s), pt, preferred_element_type=F32))
                m_sc[:, cols(c)] = m_new

        return carry

    lax.fori_loop(0, nk, body, 0)
    acc = acc_sc[...] + pv_sc[...]
    o = acc / acc[vd:vd + 1, :]
    for p in range(heads // 2):
        o_ref[0, :, p * LANES:(p + 1) * LANES] = _pack_head_pair(
            o[:, 2 * p * tq:(2 * p + 1) * tq].T, o[:, (2 * p + 1) * tq:(2 * p + 2) * tq].T, vd).astype(BF16)


def _flash(qt, k, vt, vd):
    B, H, _, S = qt.shape
    HK = k.shape[1]
    G = H // HK
    NKV = max(1, 2 // G)
    heads = NKV * G
    tq = min(ATTN_ROWS // heads, S)
    tk = min(ATTN_KEYS, S)
    M = heads * tq
    cw = min(ATTN_COLS, tq)
    ks = min(ATTN_KEY_SUB, tk)
    return pl.pallas_call(
        functools.partial(_flash_kernel, NKV=NKV, G=G, tq=tq, tk=tk, ks=ks, nk=S // tk, cw=cw, vd=vd),
        grid=(B, HK // NKV, S // tq),
        in_specs=[pl.BlockSpec((1, heads, LANES, tq), lambda b, h, i: (b, h, 0, i)),
                  pl.BlockSpec((1, NKV, S, LANES), lambda b, h, i: (b, h, 0, 0)),
                  pl.BlockSpec((1, NKV, LANES, S), lambda b, h, i: (b, h, 0, 0))],
        out_specs=pl.BlockSpec((1, tq, heads * vd), lambda b, h, i: (b, i, h)),
        out_shape=jax.ShapeDtypeStruct((B, S, H * vd), BF16),
        scratch_shapes=[pltpu.VMEM((1, M), F32), pltpu.VMEM((LANES, M), F32), pltpu.VMEM((LANES, M), F32)],
        compiler_params=_cparams(("parallel", "parallel", "parallel")),
        name="flash_attn",
    )(qt, k, vt)


def _window_kernel(sink_ref, qt_ref, k_ref, vt_ref, o_ref, *, G, tq, span, S, vd):
    kvh = pl.program_id(1)
    start = pl.program_id(2) * tq
    for j in range(tq // LANES):
        q0 = start + j * LANES
        kstart = pl.multiple_of(jnp.clip(q0 - WINDOW, 0, S - span), LANES)
        k = k_ref[0, 0, pl.ds(kstart, span), :]
        vt = vt_ref[0, 0, :, pl.ds(kstart, span)]
        kpos = kstart + lax.broadcasted_iota(jnp.int32, (span, 1), 0)
        qpos = q0 + lax.broadcasted_iota(jnp.int32, (1, LANES), 1)
        valid = jnp.abs(qpos - kpos) <= WINDOW
        outs = []
        for g in range(G):
            st = jnp.dot(k, qt_ref[0, g, :, j * LANES:(j + 1) * LANES], preferred_element_type=F32)
            st = jnp.where(valid, st, NEG_INF)
            sink = sink_ref[kvh * G + g] * LOG2E
            m = jnp.maximum(jnp.max(st, axis=0, keepdims=True), sink)
            pt = jnp.exp2(st - m).astype(BF16)
            acc = jnp.dot(vt, pt, preferred_element_type=F32)
            outs.append((acc / (acc[vd:vd + 1, :] + jnp.exp2(sink - m))).T)
        for p in range(G // 2):
            o_ref[0, j * LANES:(j + 1) * LANES, p * LANES:(p + 1) * LANES] = _pack_head_pair(
                outs[2 * p], outs[2 * p + 1], vd).astype(BF16)


def _window(qt, k, vt, sink, vd):
    B, H, _, S = qt.shape
    HK = k.shape[1]
    G = H // HK
    tq = min(WIN_Q, S)
    span = min(LANES + 2 * WINDOW, S)
    return pl.pallas_call(
        functools.partial(_window_kernel, G=G, tq=tq, span=span, S=S, vd=vd),
        grid_spec=pltpu.PrefetchScalarGridSpec(
            num_scalar_prefetch=1,
            grid=(B, HK, S // tq),
            in_specs=[pl.BlockSpec((1, G, LANES, tq), lambda b, h, i, sk: (b, h, 0, i)),
                      pl.BlockSpec((1, 1, S, LANES), lambda b, h, i, sk: (b, h, 0, 0)),
                      pl.BlockSpec((1, 1, LANES, S), lambda b, h, i, sk: (b, h, 0, 0))],
            out_specs=pl.BlockSpec((1, tq, G * vd), lambda b, h, i, sk: (b, i, h))),
        out_shape=jax.ShapeDtypeStruct((B, S, H * vd), BF16),
        compiler_params=_cparams(("parallel", "parallel", "parallel")),
        name="window_attn",
    )(sink, qt, k, vt)


def _out_ln_router_kernel(*refs, n_in):
    o_refs = refs[:n_in]
    w_refs = refs[n_in:2 * n_in]
    x_ref, g_ref, b_ref, wrh_ref, wrl_ref, x1_ref, x1b_ref, aff_ref = refs[2 * n_in:]
    h = jnp.dot(o_refs[0][0], w_refs[0][...], preferred_element_type=F32)
    for i in range(1, n_in):
        h = h + jnp.dot(o_refs[i][0], w_refs[i][...], preferred_element_type=F32)
    x1 = _layer_norm(ALPHA * x_ref[0] + h, g_ref[...], b_ref[...])
    x1_ref[0] = x1
    x_hi = x1.astype(BF16)
    x1b_ref[0] = x_hi
    x_lo = (x1 - x_hi.astype(F32)).astype(BF16)
    logits = (lax.dot_general(wrh_ref[...], x_hi, NT_DIMS, preferred_element_type=F32)
              + lax.dot_general(wrh_ref[...], x_lo, NT_DIMS, preferred_element_type=F32)
              + lax.dot_general(wrl_ref[...], x_hi, NT_DIMS, preferred_element_type=F32))
    z = jnp.exp(logits - jnp.max(logits, axis=0, keepdims=True))
    aff_ref[0] = z / jnp.sum(z, axis=0, keepdims=True)


def _out_ln_router(os_, ws_, x, g, b, wr_hi, wr_lo):
    B, S, D = x.shape
    E = wr_hi.shape[0]
    tm = min(PROJ_ROWS, S)
    n_in = len(os_)
    row = lambda bb, i: (bb, i, 0)
    const = lambda bb, i: (0, 0)
    in_specs = ([pl.BlockSpec((1, tm, o.shape[2]), row) for o in os_]
                + [pl.BlockSpec(w.shape, const) for w in ws_]
                + [pl.BlockSpec((1, tm, D), row), pl.BlockSpec((1, D), const), pl.BlockSpec((1, D), const),
                   pl.BlockSpec((E, D), const), pl.BlockSpec((E, D), const)])
    return pl.pallas_call(
        functools.partial(_out_ln_router_kernel, n_in=n_in),
        grid=(B, S // tm),
        in_specs=in_specs,
        out_specs=[pl.BlockSpec((1, tm, D), row), pl.BlockSpec((1, tm, D), row),
                   pl.BlockSpec((1, E, tm), lambda bb, i: (bb, 0, i))],
        out_shape=[jax.ShapeDtypeStruct((B, S, D), F32), jax.ShapeDtypeStruct((B, S, D), BF16),
                   jax.ShapeDtypeStruct((B, E, S), F32)],
        compiler_params=_cparams(("parallel", "parallel")),
        name="out_ln_router",
    )(*os_, *ws_, x, g, b, wr_hi, wr_lo)


def _topk_kernel(aff_ref, tri_ref, rel_ref, cnt_ref, *, S, E, cap, T, CH):
    aff = aff_ref[0]
    bits = pltpu.bitcast(aff, jnp.int32)
    capf = jnp.float32(cap)

    def count(mask):
        return jnp.sum(jnp.where(mask, 1.0, 0.0), axis=1, keepdims=True)

    def thr_body(i, t):
        cand = t | jnp.left_shift(jnp.int32(1), 30 - i)
        return jnp.where(count(bits >= cand) >= capf, cand, t)

    thr = lax.fori_loop(0, 31, thr_body, jnp.zeros((E, 1), jnp.int32))
    gt = bits > thr
    ties = bits == thr
    need = capf - count(gt)
    idx = lax.broadcasted_iota(jnp.int32, (E, S), 1)
    nbits = max(1, (S - 1).bit_length())

    def cut_body(i, c):
        cand = c | jnp.left_shift(jnp.int32(1), nbits - 1 - i)
        return jnp.where(count(ties & (idx < cand)) < need, cand, c)

    cut = lax.fori_loop(0, nbits, cut_body, jnp.zeros((E, 1), jnp.int32))
    sel = gt | (ties & (idx <= cut))

    tri = tri_ref[...]
    nsc = S // T
    lane_sc = lax.broadcasted_iota(jnp.int32, (E, nsc), 1)
    cnt = jnp.zeros((E, nsc), F32)
    for sc in range(nsc):
        run = jnp.zeros((E, 1), F32)
        for ch in range(T // CH):
            lo = sc * T + ch * CH
            selc = jnp.where(sel[:, lo:lo + CH], 1.0, 0.0)
            incl = jnp.dot(selc.astype(BF16), tri, preferred_element_type=F32)
            rel = jnp.where(selc > 0.0, incl - 1.0 + run, -1.0)
            rel_ref[0, :, lo:lo + CH] = rel
            run = run + incl[:, CH - 1:CH]
        cnt = jnp.where(lane_sc == sc, run, cnt)
    cnt_ref[0] = cnt


def _topk(aff, T):
    B, E, S = aff.shape
    cap = EC_FACTOR * S // N_EXPERTS
    CH = min(CUMSUM_CHUNK, T)
    r = lax.broadcasted_iota(jnp.int32, (CH, CH), 0)
    c = lax.broadcasted_iota(jnp.int32, (CH, CH), 1)
    tri = jnp.where(r <= c, 1.0, 0.0).astype(BF16)
    nsc = S // T
    return pl.pallas_call(
        functools.partial(_topk_kernel, S=S, E=E, cap=cap, T=T, CH=CH),
        grid=(B,),
        in_specs=[pl.BlockSpec((1, E, S), lambda b: (b, 0, 0)),
                  pl.BlockSpec((CH, CH), lambda b: (0, 0))],
        out_specs=[pl.BlockSpec((1, E, S), lambda b: (b, 0, 0)),
                   pl.BlockSpec((1, E, nsc), lambda b: (b, 0, 0))],
        out_shape=[jax.ShapeDtypeStruct((B, E, S), F32), jax.ShapeDtypeStruct((B, E, nsc), F32)],
        compiler_params=_cparams(("parallel",)),
        name="topk_select",
    )(aff, tri)


def _moe_kernel(nblk_ref, xb_ref, relt_ref, afft_ref, wg_ref, wu_ref, wd_ref, x1_hbm, g_ref, b_ref,
                f_ref, x1_buf, x1_sem, *, T, NT, RB, E, nsc, TS):
    b, sg, e = pl.program_id(0), pl.program_id(1), pl.program_id(2)

    @pl.when(e == 0)
    def _():
        f_ref[...] = jnp.zeros(f_ref.shape, F32)

    def x1_copy():
        return pltpu.make_async_copy(x1_hbm.at[b, pl.ds(sg * NT * T, NT * T), :], x1_buf, x1_sem)

    @pl.when(e == E - 2)
    def _():
        x1_copy().start()

    for t in range(NT):
        tok = slice(t * T, (t + 1) * T)
        nb = nblk_ref[(b * nsc + sg * NT + t) * E + e]
        relt = relt_ref[0, 0, :, tok]
        afft = afft_ref[0, 0, :, tok]
        row_tiles = [slice(t * T + ts * TS, t * T + (ts + 1) * TS) for ts in range(T // TS)]

        def slot_block(j, carry, tok=tok, relt=relt, afft=afft, row_tiles=row_tiles):
            slot = (j * RB + lax.broadcasted_iota(jnp.int32, (RB, 1), 0)).astype(F32)
            hit = relt == slot
            gate = jnp.sum(jnp.where(hit, afft, 0.0), axis=1, keepdims=True)
            onehot = jnp.where(hit, 1.0, 0.0).astype(BF16)
            xg = jnp.dot(onehot, xb_ref[0, tok, :], preferred_element_type=F32).astype(BF16)
            hg = jnp.dot(xg, wg_ref[0], preferred_element_type=F32)
            hu = jnp.dot(xg, wu_ref[0], preferred_element_type=F32)
            h = (hg / (1.0 + jnp.exp(-hg)) * hu).astype(BF16)
            y = (jnp.dot(h, wd_ref[0], preferred_element_type=F32) * gate).astype(BF16)
            for ts, rows in enumerate(row_tiles):
                cols = slice(ts * TS, (ts + 1) * TS)
                f_ref[0, rows, :] += lax.dot_general(onehot[:, cols], y, TN_DIMS, preferred_element_type=F32)
            return carry

        lax.fori_loop(0, nb, slot_block, 0)

    @pl.when(e == E - 1)
    def _():
        x1_copy().wait()
        for r in range(NT * T // TS):
            rows = slice(r * TS, (r + 1) * TS)
            f_ref[0, rows, :] = _layer_norm(ALPHA * x1_buf[rows, :] + f_ref[0, rows, :], g_ref[...], b_ref[...])


def _moe(xb, relt, afft, nblk, wg, wu, wd, x1, g, b, T):
    B, S, D = xb.shape
    E, _, F = wg.shape
    nsc = S // T
    NT = min(MOE_TILES, nsc)
    assert nsc % NT == 0 and E >= 2
    RB = MOE_SLOTS
    TS = min(512, T)
    return pl.pallas_call(
        functools.partial(_moe_kernel, T=T, NT=NT, RB=RB, E=E, nsc=nsc, TS=TS),
        grid_spec=pltpu.PrefetchScalarGridSpec(
            num_scalar_prefetch=1,
            grid=(B, nsc // NT, E),
            in_specs=[pl.BlockSpec((1, NT * T, D), lambda b, s, e, n: (b, s, 0)),
                      pl.BlockSpec((1, 1, 1, NT * T), lambda b, s, e, n: (b, e, 0, s)),
                      pl.BlockSpec((1, 1, 1, NT * T), lambda b, s, e, n: (b, e, 0, s)),
                      pl.BlockSpec((1, D, F), lambda b, s, e, n: (e, 0, 0)),
                      pl.BlockSpec((1, D, F), lambda b, s, e, n: (e, 0, 0)),
                      pl.BlockSpec((1, F, D), lambda b, s, e, n: (e, 0, 0)),
                      pl.BlockSpec(memory_space=pl.ANY),
                      pl.BlockSpec((1, D), lambda b, s, e, n: (0, 0)),
                      pl.BlockSpec((1, D), lambda b, s, e, n: (0, 0))],
            out_specs=pl.BlockSpec((1, NT * T, D), lambda b, s, e, n: (b, s, 0)),
            scratch_shapes=[pltpu.VMEM((NT * T, D), F32), pltpu.SemaphoreType.DMA(())]),
        out_shape=jax.ShapeDtypeStruct((B, S, D), F32),
        compiler_params=_cparams(("parallel", "parallel", "arbitrary")),
        name="moe_ffn",
    )(nblk, xb, relt, afft, wg, wu, wd, x1, g, b)


def _proj_mla_kernel(x_ref, wdt_ref, gq_ref, gkv_ref, wqt_ref, wkvt_ref, c_ref, s_ref, qt_ref, k_ref, vt_ref):
    xb = x_ref[0].astype(BF16)
    dt = lax.dot_general(wdt_ref[...], xb, NT_DIMS, preferred_element_type=F32)
    cq = _rms_t(dt[0:Q_LORA], gq_ref[...], Q_LORA).astype(BF16)
    ckv = _rms_t(dt[Q_LORA:Q_LORA + KV_LORA], gkv_ref[...], KV_LORA).astype(BF16)
    r2 = QK_ROPE // 2
    c, s = c_ref[...], s_ref[...]
    kr = dt[Q_LORA + KV_LORA:Q_LORA + KV_LORA + QK_ROPE]
    kr1, kr2 = _rope_t(kr[0:r2], kr[r2:], c, s)
    qt = jnp.dot(wqt_ref[...], cq, preferred_element_type=F32)
    kvt = jnp.dot(wkvt_ref[...], ckv, preferred_element_type=F32)
    scale = (QK_NOPE + QK_ROPE) ** -0.5 * LOG2E
    dq, dkv = QK_NOPE + QK_ROPE, QK_NOPE + V_DIM
    for h in range(MLA_HEADS):
        q = qt[h * dq:(h + 1) * dq]
        q1, q2 = _rope_t(q[QK_NOPE:QK_NOPE + r2], q[QK_NOPE + r2:], c, s)
        _store_qt(qt_ref, h, jnp.concatenate([q[0:QK_NOPE], q1, q2], axis=0) * scale)
        kv = kvt[h * dkv:(h + 1) * dkv]
        _store_k(k_ref, h, jnp.concatenate([kv[0:QK_NOPE], kr1, kr2], axis=0))
        _store_vt(vt_ref, h, kv[QK_NOPE:])


def _proj_mla(x, wdt, gq, gkv, wqt, wkvt, tabs):
    B, S, D = x.shape
    tm = min(PROJ_ROWS, S)
    H = MLA_HEADS
    const = lambda b, i: (0, 0)
    full = lambda a: pl.BlockSpec(a.shape, const)
    tab = lambda t: pl.BlockSpec((t.shape[0], tm), lambda b, i: (0, i))
    gq_b = jnp.broadcast_to(gq.reshape(-1, 1), (Q_LORA, tm))
    gkv_b = jnp.broadcast_to(gkv.reshape(-1, 1), (KV_LORA, tm))
    return pl.pallas_call(
        _proj_mla_kernel,
        grid=(B, S // tm),
        in_specs=[pl.BlockSpec((1, tm, D), lambda b, i: (b, i, 0)),
                  full(wdt), full(gq_b), full(gkv_b), full(wqt), full(wkvt), tab(tabs[0]), tab(tabs[1])],
        out_specs=[pl.BlockSpec((1, H, LANES, tm), lambda b, i: (b, 0, 0, i)),
                   pl.BlockSpec((1, H, tm, LANES), lambda b, i: (b, 0, i, 0)),
                   pl.BlockSpec((1, H, LANES, tm), lambda b, i: (b, 0, 0, i))],
        out_shape=[jax.ShapeDtypeStruct((B, H, LANES, S), BF16),
                   jax.ShapeDtypeStruct((B, H, S, LANES), BF16),
                   jax.ShapeDtypeStruct((B, H, LANES, S), BF16)],
        compiler_params=_cparams(("parallel", "parallel")),
        name="proj_mla",
    )(x, wdt, gq_b, gkv_b, wqt, wkvt, *tabs)


def _angles(pos, dim):
    freqs = ROPE_THETA ** (-(jnp.arange(0, dim, 2, dtype=F32) / dim))
    return pos[:, None] * freqs[None, :]


def _tables(S):
    t = jnp.arange(S)
    ar = _angles((t // GRID_W).astype(F32), HEAD_DIM // 2)
    ac = _angles((t % GRID_W).astype(F32), HEAD_DIM // 2)
    axial = jnp.concatenate([ar, ac], axis=1).T
    seq = _angles(t.astype(F32), HEAD_DIM).T
    latent = _angles(t.astype(F32), QK_ROPE).T
    return tuple((jnp.cos(a), jnp.sin(a)) for a in (axial, seq, latent))


def _split_bf16(w):
    hi = w.astype(BF16)
    return hi, (w - hi.astype(F32)).astype(BF16)


def _moe_layer(x1, x1b, aff, w_gate, w_up, w_down, g, b):
    B, S, D = x1.shape
    T = min(MOE_TOKENS, S)
    rel, cnt = _topk(aff, T)
    nblk = ((cnt.astype(jnp.int32) + MOE_SLOTS - 1) // MOE_SLOTS).transpose(0, 2, 1).reshape(-1)
    relt = rel.reshape(B, N_EXPERTS, 1, S)
    afft = aff.reshape(B, N_EXPERTS, 1, S)
    return _moe(x1b, relt, afft, nblk, w_gate.astype(BF16), w_up.astype(BF16), w_down.astype(BF16), x1, g, b, T)


def kernel(x, ab_w_in, ab_q_norm, ab_k_norm, ab_sink, ab_w_out, mla_w_down, mla_q_norm, mla_kv_norm,
           mla_w_uq, mla_w_ukv, mla_w_out, ln_mix_g, ln_mix_b, moe_router, moe_w_gate, moe_w_up, moe_w_down,
           ln_ffn_g, ln_ffn_b):
    B, S, D = x.shape
    tabs_a, tabs_b, tabs_m = _tables(S)
    row = lambda v: v.reshape(1, -1)

    qta, ka, vta, qtb, kb, vtb = _proj_ab(x, ab_w_in[0].T.astype(BF16), ab_q_norm[0], ab_k_norm[0], tabs_a, tabs_b)
    oa = _flash(qta, ka, vta, HEAD_DIM)
    ob = _window(qtb, kb, vtb, ab_sink[0], HEAD_DIM)
    n_a = A_HEADS * HEAD_DIM
    w_oa = ab_w_out[0][:n_a].astype(BF16)
    w_ob = ab_w_out[0][n_a:].astype(BF16)
    wr_hi, wr_lo = _split_bf16(moe_router[0].T)
    x1, x1b, aff = _out_ln_router([oa, ob], [w_oa, w_ob], x, row(ln_mix_g[0]), row(ln_mix_b[0]), wr_hi, wr_lo)
    x = _moe_layer(x1, x1b, aff, moe_w_gate[0], moe_w_up[0], moe_w_down[0], row(ln_ffn_g[0]), row(ln_ffn_b[0]))

    qt, k, vt = _proj_mla(x, mla_w_down[0].T.astype(BF16), mla_q_norm[0], mla_kv_norm[0],
                          mla_w_uq[0].T.astype(BF16), mla_w_ukv[0].T.astype(BF16), tabs_m)
    oc = _flash(qt, k, vt, V_DIM)
    w_oc = mla_w_out[0].astype(BF16)
    wr_hi, wr_lo = _split_bf16(moe_router[1].T)
    x1, x1b, aff = _out_ln_router([oc], [w_oc], x, row(ln_mix_g[1]), row(ln_mix_b[1]), wr_hi, wr_lo)
    x = _moe_layer(x1, x1b, aff, moe_w_gate[1], moe_w_up[1], moe_w_down[1], row(ln_ffn_g[1]), row(ln_ffn_b[1]))
    return x
```

```python
import functools

import jax
import jax.numpy as jnp
from jax import lax
from jax.experimental import pallas as pl
from jax.experimental.pallas import tpu as pltpu

F32 = jnp.float32
BF16 = jnp.bfloat16

GRID_W = 64
ROPE_THETA = 10000.0
HEAD_DIM = 64
A_HEADS, A_KV = 8, 2
B_HEADS, B_KV = 8, 2
WINDOW = 128
MLA_HEADS = 16
Q_LORA, KV_LORA = 256, 128
QK_NOPE, QK_ROPE, V_DIM = 64, 32, 64
N_EXPERTS = 16
EC_FACTOR = 2
DEPTH = 2
ALPHA = (2.0 * DEPTH) ** 0.25
NEG_INF = -1e30
SHIFT_SLACK = 64.0
RMS_EPS = 1e-6
LOG2E = 1.4426950408889634
LN_EPS = 1e-5

LANES = 128
VMEM_LIMIT = 56 * 1024 * 1024

PROJ_ROWS = 512
ATTN_ROWS = 4096
ATTN_KEYS = 2048
ATTN_KEY_SUB = 512
ATTN_COLS = 256
AHEAD = 2
WIN_Q = 1024
MOE_TOKENS = 1024
MOE_TILES = 2
MOE_SLOTS = 144
CUMSUM_CHUNK = 256

NT_DIMS = (((1,), (1,)), ((), ()))
TN_DIMS = (((0,), (0,)), ((), ()))


def _cparams(sem):
    return pltpu.CompilerParams(dimension_semantics=sem, vmem_limit_bytes=VMEM_LIMIT)


def _rope_t(x1, x2, c, s):
    return x1 * c - x2 * s, x2 * c + x1 * s


def _rms_t(x, g, n):
    ms = jnp.sum(x * x, axis=0, keepdims=True) * (1.0 / n)
    return x * lax.rsqrt(ms + RMS_EPS) * g


def _layer_norm(y, g, b):
    mu = jnp.mean(y, axis=1, keepdims=True)
    d = y - mu
    var = jnp.mean(d * d, axis=1, keepdims=True)
    return d * lax.rsqrt(var + LN_EPS) * g + b


def _store_qt(ref, h, q):
    d = q.shape[0]
    ref[0, h, 0:d, :] = q.astype(BF16)
    ref[0, h, d:LANES, :] = jnp.zeros((LANES - d, q.shape[1]), BF16)


def _store_k(ref, h, kt):
    d, t = kt.shape
    ref[0, h] = jnp.concatenate([kt, jnp.zeros((LANES - d, t), F32)], axis=0).T.astype(BF16)


def _store_vt(ref, h, vt):
    d, t = vt.shape
    ref[0, h, 0:d, :] = vt.astype(BF16)
    row = lax.broadcasted_iota(jnp.int32, (LANES - d, t), 0)
    ref[0, h, d:LANES, :] = jnp.where(row == 0, 1.0, 0.0).astype(BF16)


def _proj_ab_kernel(x_ref, wt_ref, gq_ref, gk_ref, ca_ref, sa_ref, cb_ref, sb_ref,
                    qa_ref, ka_ref, va_ref, qb_ref, kb_ref, vb_ref):
    xb = x_ref[0].astype(BF16)
    pt = lax.dot_general(wt_ref[...], xb, NT_DIMS, preferred_element_type=F32)
    scale = HEAD_DIM ** -0.5 * LOG2E
    q4, h2 = HEAD_DIM // 4, HEAD_DIM // 2
    ca, sa, cb, sb = ca_ref[...], sa_ref[...], cb_ref[...], sb_ref[...]

    def rope_a(p):
        r1, r2 = _rope_t(p[0:q4], p[q4:2 * q4], ca[0:q4], sa[0:q4])
        c1, c2 = _rope_t(p[2 * q4:3 * q4], p[3 * q4:], ca[q4:], sa[q4:])
        return jnp.concatenate([r1, r2, c1, c2], axis=0)

    def rope_b(p):
        x1, x2 = _rope_t(p[0:h2], p[h2:], cb, sb)
        return jnp.concatenate([x1, x2], axis=0)

    heads = iter(pt[g * HEAD_DIM:(g + 1) * HEAD_DIM] for g in range(pt.shape[0] // HEAD_DIM))
    for h in range(A_HEADS):
        _store_qt(qa_ref, h, rope_a(_rms_t(next(heads), gq_ref[...], HEAD_DIM)) * scale)
    for h in range(A_KV):
        _store_k(ka_ref, h, rope_a(_rms_t(next(heads), gk_ref[...], HEAD_DIM)))
    for h in range(A_KV):
        _store_vt(va_ref, h, next(heads))
    for h in range(B_HEADS):
        _store_qt(qb_ref, h, rope_b(next(heads)) * scale)
    for h in range(B_KV):
        _store_k(kb_ref, h, rope_b(next(heads)))
    for h in range(B_KV):
        _store_vt(vb_ref, h, next(heads))


def _proj_ab(x, wt, gq, gk, tabs_a, tabs_b):
    B, S, D = x.shape
    tm = min(PROJ_ROWS, S)
    const = lambda b, i: (0, 0)
    tab = lambda t: pl.BlockSpec((t.shape[0], tm), lambda b, i: (0, i))
    hm = lambda h: pl.BlockSpec((1, h, tm, LANES), lambda b, i: (b, 0, i, 0))
    tr = lambda h: pl.BlockSpec((1, h, LANES, tm), lambda b, i: (b, 0, 0, i))
    sd = lambda h: jax.ShapeDtypeStruct((B, h, S, LANES), BF16)
    sdt = lambda h: jax.ShapeDtypeStruct((B, h, LANES, S), BF16)
    gq_b = jnp.broadcast_to(gq.reshape(-1, 1), (HEAD_DIM, tm))
    gk_b = jnp.broadcast_to(gk.reshape(-1, 1), (HEAD_DIM, tm))
    return pl.pallas_call(
        _proj_ab_kernel,
        grid=(B, S // tm),
        in_specs=[pl.BlockSpec((1, tm, D), lambda b, i: (b, i, 0)),
                  pl.BlockSpec(wt.shape, const),
                  pl.BlockSpec(gq_b.shape, const), pl.BlockSpec(gk_b.shape, const),
                  tab(tabs_a[0]), tab(tabs_a[1]), tab(tabs_b[0]), tab(tabs_b[1])],
        out_specs=[tr(A_HEADS), hm(A_KV), tr(A_KV), tr(B_HEADS), hm(B_KV), tr(B_KV)],
        out_shape=[sdt(A_HEADS), sd(A_KV), sdt(A_KV), sdt(B_HEADS), sd(B_KV), sdt(B_KV)],
        compiler_params=_cparams(("parallel", "parallel")),
        name="proj_ab",
    )(x, wt, gq_b, gk_b, *tabs_a, *tabs_b)


def _flash_kernel(qt_ref, k_ref, vt_ref, o_ref, m_sc, acc_sc, pv_sc, *, NKV, G, tq, tk, ks, nk, cw, vd):
    heads = NKV * G
    M = heads * tq
    n_sub = M // cw

    def q_tile(c):
        h, j = divmod(c * cw, tq)
        return qt_ref[0, h, :, j:j + cw]

    def kv_of(c):
        return (c * cw // tq) // G

    def cols(c):
        return slice(c * cw, (c + 1) * cw)

    for c in range(n_sub):
        s0 = jnp.dot(k_ref[0, kv_of(c), 0:LANES, :], q_tile(c), preferred_element_type=F32)
        m_sc[:, cols(c)] = jnp.max(s0, axis=0, keepdims=True)
    acc_sc[...] = jnp.zeros((LANES, M), F32)
    pv_sc[...] = jnp.zeros((LANES, M), F32)

    def body(kb, carry):
        off = pl.multiple_of(kb * tk, tk)
        acc_sc[...] += pv_sc[...]

        def keys(c, s):
            return k_ref[0, kv_of(c), pl.ds(pl.multiple_of(off + s * ks, ks), ks), :]

        def values_t(c, s=None):
            if s is None:
                return vt_ref[0, kv_of(c), :, pl.ds(off, tk)]
            return vt_ref[0, kv_of(c), :, pl.ds(pl.multiple_of(off + s * ks, ks), ks)]

        tiles = [(c, s) for c in range(n_sub) for s in range(tk // ks)]

        def scores(t):
            c, s = tiles[t]
            return jnp.dot(keys(c, s), q_tile(c), preferred_element_type=F32)

        excess = None
        ahead = {t: scores(t) for t in range(min(AHEAD, len(tiles)))}
        pts = []
        for t, (c, s) in enumerate(tiles):
            st = ahead.pop(t)
            if t + AHEAD < len(tiles):
                ahead[t + AHEAD] = scores(t + AHEAD)
            m = m_sc[:, cols(c)]
            over = jnp.max(st, axis=0, keepdims=True) - m
            excess = over if excess is None else jnp.maximum(excess, over)
            pts.append(jnp.exp2(st - m).astype(BF16))
            if s == tk // ks - 1:
                pt = pts[0] if len(pts) == 1 else jnp.concatenate(pts, axis=0)
                pv_sc[:, cols(c)] = jnp.dot(values_t(c), pt, preferred_element_type=F32)
                pts = []
        renew = jnp.max(excess) > SHIFT_SLACK

        @pl.when(renew)
        def _():
            pv_sc[...] = jnp.zeros((LANES, M), F32)
            for c, s in tiles:
                st = jnp.dot(keys(c, s), q_tile(c), preferred_element_type=F32)
                m_prev = m_sc[:, cols(c)]
                m_new = jnp.maximum(m_prev, jnp.max(st, axis=0, keepdims=True))
                alpha = jnp.exp2(m_prev - m_new)
                pt = jnp.exp2(st - m_new).astype(BF16)
                acc_sc[:, cols(c)] = (acc_sc[:, cols(c)] * alpha
                                      + jnp.dot(values_t(c, s), pt, preferred_element_type=F32))
                m_sc[:, cols(c)] = m_new

        return carry

    lax.fori_loop(0, nk, body, 0)
    acc = acc_sc[0:2 * vd, :] + pv_sc[0:2 * vd, :]
    o = acc[0:vd, :] / acc[vd:vd + 1, :]
    for p in range(heads // 2):
        pair = jnp.concatenate([o[:, 2 * p * tq:(2 * p + 1) * tq], o[:, (2 * p + 1) * tq:(2 * p + 2) * tq]], axis=0)
        o_ref[0, :, p * LANES:(p + 1) * LANES] = pair.T.astype(BF16)


def _flash(qt, k, vt, vd):
    B, H, _, S = qt.shape
    HK = k.shape[1]
    G = H // HK
    assert 2 * vd == LANES
    NKV = max(1, 2 // G)
    heads = NKV * G
    tq = min(ATTN_ROWS // heads, S)
    tk = min(ATTN_KEYS, S)
    M = heads * tq
    cw = min(ATTN_COLS, tq)
    ks = min(ATTN_KEY_SUB, tk)
    return pl.pallas_call(
        functools.partial(_flash_kernel, NKV=NKV, G=G, tq=tq, tk=tk, ks=ks, nk=S // tk, cw=cw, vd=vd),
        grid=(B, HK // NKV, S // tq),
        in_specs=[pl.BlockSpec((1, heads, LANES, tq), lambda b, h, i: (b, h, 0, i)),
                  pl.BlockSpec((1, NKV, S, LANES), lambda b, h, i: (b, h, 0, 0)),
                  pl.BlockSpec((1, NKV, LANES, S), lambda b, h, i: (b, h, 0, 0))],
        out_specs=pl.BlockSpec((1, tq, heads * vd), lambda b, h, i: (b, i, h)),
        out_shape=jax.ShapeDtypeStruct((B, S, H * vd), BF16),
        scratch_shapes=[pltpu.VMEM((1, M), F32), pltpu.VMEM((LANES, M), F32), pltpu.VMEM((LANES, M), F32)],
        compiler_params=_cparams(("parallel", "parallel", "parallel")),
        name="flash_attn",
    )(qt, k, vt)


def _window_kernel(sink_ref, qt_ref, k_ref, vt_ref, o_ref, *, G, tq, span, S, vd):
    kvh = pl.program_id(1)
    start = pl.program_id(2) * tq
    for j in range(tq // LANES):
        q0 = start + j * LANES
        kstart = pl.multiple_of(jnp.clip(q0 - WINDOW, 0, S - span), LANES)
        k = k_ref[0, 0, pl.ds(kstart, span), :]
        vt = vt_ref[0, 0, :, pl.ds(kstart, span)]
        kpos = kstart + lax.broadcasted_iota(jnp.int32, (span, 1), 0)
        qpos = q0 + lax.broadcasted_iota(jnp.int32, (1, LANES), 1)
        valid = jnp.abs(qpos - kpos) <= WINDOW
        outs = []
        for g in range(G):
            st = jnp.dot(k, qt_ref[0, g, :, j * LANES:(j + 1) * LANES], preferred_element_type=F32)
            st = jnp.where(valid, st, NEG_INF)
            sink = sink_ref[kvh * G + g] * LOG2E
            m = jnp.maximum(jnp.max(st, axis=0, keepdims=True), sink)
            pt = jnp.exp2(st - m).astype(BF16)
            acc = jnp.dot(vt, pt, preferred_element_type=F32)
            outs.append(acc[0:vd, :] / (acc[vd:vd + 1, :] + jnp.exp2(sink - m)))
        for p in range(G // 2):
            pair = jnp.concatenate([outs[2 * p], outs[2 * p + 1]], axis=0)
            o_ref[0, j * LANES:(j + 1) * LANES, p * LANES:(p + 1) * LANES] = pair.T.astype(BF16)


def _window(qt, k, vt, sink, vd):
    B, H, _, S = qt.shape
    HK = k.shape[1]
    G = H // HK
    tq = min(WIN_Q, S)
    span = min(LANES + 2 * WINDOW, S)
    return pl.pallas_call(
        functools.partial(_window_kernel, G=G, tq=tq, span=span, S=S, vd=vd),
        grid_spec=pltpu.PrefetchScalarGridSpec(
            num_scalar_prefetch=1,
            grid=(B, HK, S // tq),
            in_specs=[pl.BlockSpec((1, G, LANES, tq), lambda b, h, i, sk: (b, h, 0, i)),
                      pl.BlockSpec((1, 1, S, LANES), lambda b, h, i, sk: (b, h, 0, 0)),
                      pl.BlockSpec((1, 1, LANES, S), lambda b, h, i, sk: (b, h, 0, 0))],
            out_specs=pl.BlockSpec((1, tq, G * vd), lambda b, h, i, sk: (b, i, h))),
        out_shape=jax.ShapeDtypeStruct((B, S, H * vd), BF16),
        compiler_params=_cparams(("parallel", "parallel", "parallel")),
        name="window_attn",
    )(sink, qt, k, vt)


def _out_ln_router_kernel(*refs, n_in):
    o_refs = refs[:n_in]
    w_refs = refs[n_in:2 * n_in]
    x_ref, g_ref, b_ref, wrh_ref, wrl_ref, x1_ref, x1b_ref, aff_ref = refs[2 * n_in:]
    h = jnp.dot(o_refs[0][0], w_refs[0][...], preferred_element_type=F32)
    for i in range(1, n_in):
        h = h + jnp.dot(o_refs[i][0], w_refs[i][...], preferred_element_type=F32)
    x1 = _layer_norm(ALPHA * x_ref[0] + h, g_ref[...], b_ref[...])
    x1_ref[0] = x1
    x_hi = x1.astype(BF16)
    x1b_ref[0] = x_hi
    x_lo = (x1 - x_hi.astype(F32)).astype(BF16)
    logits = (lax.dot_general(wrh_ref[...], x_hi, NT_DIMS, preferred_element_type=F32)
              + lax.dot_general(wrh_ref[...], x_lo, NT_DIMS, preferred_element_type=F32)
              + lax.dot_general(wrl_ref[...], x_hi, NT_DIMS, preferred_element_type=F32))
    z = jnp.exp(logits - jnp.max(logits, axis=0, keepdims=True))
    aff_ref[0] = z / jnp.sum(z, axis=0, keepdims=True)


def _out_ln_router(os_, ws_, x, g, b, wr_hi, wr_lo):
    B, S, D = x.shape
    E = wr_hi.shape[0]
    tm = min(PROJ_ROWS, S)
    n_in = len(os_)
    row = lambda bb, i: (bb, i, 0)
    const = lambda bb, i: (0, 0)
    in_specs = ([pl.BlockSpec((1, tm, o.shape[2]), row) for o in os_]
                + [pl.BlockSpec(w.shape, const) for w in ws_]
                + [pl.BlockSpec((1, tm, D), row), pl.BlockSpec((1, D), const), pl.BlockSpec((1, D), const),
                   pl.BlockSpec((E, D), const), pl.BlockSpec((E, D), const)])
    return pl.pallas_call(
        functools.partial(_out_ln_router_kernel, n_in=n_in),
        grid=(B, S // tm),
        in_specs=in_specs,
        out_specs=[pl.BlockSpec((1, tm, D), row), pl.BlockSpec((1, tm, D), row),
                   pl.BlockSpec((1, E, tm), lambda bb, i: (bb, 0, i))],
        out_shape=[jax.ShapeDtypeStruct((B, S, D), F32), jax.ShapeDtypeStruct((B, S, D), BF16),
                   jax.ShapeDtypeStruct((B, E, S), F32)],
        compiler_params=_cparams(("parallel", "parallel")),
        name="out_ln_router",
    )(*os_, *ws_, x, g, b, wr_hi, wr_lo)


def _topk_kernel(aff_ref, tri_ref, rel_ref, cnt_ref, *, S, E, cap, T, CH):
    aff = aff_ref[0]
    bits = pltpu.bitcast(aff, jnp.int32)
    capf = jnp.float32(cap)

    def count(mask):
        return jnp.sum(jnp.where(mask, 1.0, 0.0), axis=1, keepdims=True)

    def thr_body(i, t):
        cand = t | jnp.left_shift(jnp.int32(1), 30 - i)
        return jnp.where(count(bits >= cand) >= capf, cand, t)

    thr = lax.fori_loop(0, 31, thr_body, jnp.zeros((E, 1), jnp.int32))
    gt = bits > thr
    ties = bits == thr
    need = capf - count(gt)
    idx = lax.broadcasted_iota(jnp.int32, (E, S), 1)
    nbits = max(1, (S - 1).bit_length())

    def cut_body(i, c):
        cand = c | jnp.left_shift(jnp.int32(1), nbits - 1 - i)
        return jnp.where(count(ties & (idx < cand)) < need, cand, c)

    cut = lax.fori_loop(0, nbits, cut_body, jnp.zeros((E, 1), jnp.int32))
    sel = gt | (ties & (idx <= cut))

    tri = tri_ref[...]
    nsc = S // T
    lane_sc = lax.broadcasted_iota(jnp.int32, (E, nsc), 1)
    cnt = jnp.zeros((E, nsc), F32)
    for sc in range(nsc):
        run = jnp.zeros((E, 1), F32)
        for ch in range(T // CH):
            lo = sc * T + ch * CH
            selc = jnp.where(sel[:, lo:lo + CH], 1.0, 0.0)
            incl = jnp.dot(selc.astype(BF16), tri, preferred_element_type=F32)
            rel = jnp.where(selc > 0.0, incl - 1.0 + run, -1.0)
            rel_ref[0, :, lo:lo + CH] = rel
            run = run + incl[:, CH - 1:CH]
        cnt = jnp.where(lane_sc == sc, run, cnt)
    cnt_ref[0] = cnt


def _topk(aff, T):
    B, E, S = aff.shape
    cap = EC_FACTOR * S // N_EXPERTS
    CH = min(CUMSUM_CHUNK, T)
    r = lax.broadcasted_iota(jnp.int32, (CH, CH), 0)
    c = lax.broadcasted_iota(jnp.int32, (CH, CH), 1)
    tri = jnp.where(r <= c, 1.0, 0.0).astype(BF16)
    nsc = S // T
    return pl.pallas_call(
        functools.partial(_topk_kernel, S=S, E=E, cap=cap, T=T, CH=CH),
        grid=(B,),
        in_specs=[pl.BlockSpec((1, E, S), lambda b: (b, 0, 0)),
                  pl.BlockSpec((CH, CH), lambda b: (0, 0))],
        out_specs=[pl.BlockSpec((1, E, S), lambda b: (b, 0, 0)),
                   pl.BlockSpec((1, E, nsc), lambda b: (b, 0, 0))],
        out_shape=[jax.ShapeDtypeStruct((B, E, S), F32), jax.ShapeDtypeStruct((B, E, nsc), F32)],
        compiler_params=_cparams(("parallel",)),
        name="topk_select",
    )(aff, tri)


def _moe_kernel(nblk_ref, xb_ref, relt_ref, afft_ref, wg_ref, wu_ref, wd_ref, x1_hbm, g_ref, b_ref,
                f_ref, x1_buf, x1_sem, *, T, NT, RB, E, nsc, TS):
    b, sg, e = pl.program_id(0), pl.program_id(1), pl.program_id(2)

    @pl.when(e == 0)
    def _():
        f_ref[...] = jnp.zeros(f_ref.shape, F32)

    def x1_copy():
        return pltpu.make_async_copy(x1_hbm.at[b, pl.ds(sg * NT * T, NT * T), :], x1_buf, x1_sem)

    @pl.when(e == E - 2)
    def _():
        x1_copy().start()

    for t in range(NT):
        tok = slice(t * T, (t + 1) * T)
        nb = nblk_ref[(b * nsc + sg * NT + t) * E + e]
        relt = relt_ref[0, 0, :, tok]
        afft = afft_ref[0, 0, :, tok]
        row_tiles = [slice(t * T + ts * TS, t * T + (ts + 1) * TS) for ts in range(T // TS)]

        def slot_block(j, carry, tok=tok, relt=relt, afft=afft, row_tiles=row_tiles):
            slot = (j * RB + lax.broadcasted_iota(jnp.int32, (RB, 1), 0)).astype(F32)
            hit = relt == slot
            gate = jnp.sum(jnp.where(hit, afft, 0.0), axis=1, keepdims=True)
            onehot = jnp.where(hit, 1.0, 0.0).astype(BF16)
            xg = jnp.dot(onehot, xb_ref[0, tok, :], preferred_element_type=F32).astype(BF16)
            hg = jnp.dot(xg, wg_ref[0], preferred_element_type=F32)
            hu = jnp.dot(xg, wu_ref[0], preferred_element_type=F32)
            h = (hg / (1.0 + jnp.exp(-hg)) * hu).astype(BF16)
            y = (jnp.dot(h, wd_ref[0], preferred_element_type=F32) * gate).astype(BF16)
            for ts, rows in enumerate(row_tiles):
                cols = slice(ts * TS, (ts + 1) * TS)
                f_ref[0, rows, :] += lax.dot_general(onehot[:, cols], y, TN_DIMS, preferred_element_type=F32)
            return carry

        lax.fori_loop(0, nb, slot_block, 0)

    @pl.when(e == E - 1)
    def _():
        x1_copy().wait()
        for r in range(NT * T // TS):
            rows = slice(r * TS, (r + 1) * TS)
            f_ref[0, rows, :] = _layer_norm(ALPHA * x1_buf[rows, :] + f_ref[0, rows, :], g_ref[...], b_ref[...])


def _moe(xb, relt, afft, nblk, wg, wu, wd, x1, g, b, T):
    B, S, D = xb.shape
    E, _, F = wg.shape
    nsc = S // T
    NT = min(MOE_TILES, nsc)
    assert nsc % NT == 0 and E >= 2
    RB = MOE_SLOTS
    TS = min(512, T)
    return pl.pallas_call(
        functools.partial(_moe_kernel, T=T, NT=NT, RB=RB, E=E, nsc=nsc, TS=TS),
        grid_spec=pltpu.PrefetchScalarGridSpec(
            num_scalar_prefetch=1,
            grid=(B, nsc // NT, E),
            in_specs=[pl.BlockSpec((1, NT * T, D), lambda b, s, e, n: (b, s, 0)),
                      pl.BlockSpec((1, 1, 1, NT * T), lambda b, s, e, n: (b, e, 0, s)),
                      pl.BlockSpec((1, 1, 1, NT * T), lambda b, s, e, n: (b, e, 0, s)),
                      pl.BlockSpec((1, D, F), lambda b, s, e, n: (e, 0, 0)),
                      pl.BlockSpec((1, D, F), lambda b, s, e, n: (e, 0, 0)),
                      pl.BlockSpec((1, F, D), lambda b, s, e, n: (e, 0, 0)),
                      pl.BlockSpec(memory_space=pl.ANY),
                      pl.BlockSpec((1, D), lambda b, s, e, n: (0, 0)),
                      pl.BlockSpec((1, D), lambda b, s, e, n: (0, 0))],
            out_specs=pl.BlockSpec((1, NT * T, D), lambda b, s, e, n: (b, s, 0)),
            scratch_shapes=[pltpu.VMEM((NT * T, D), F32), pltpu.SemaphoreType.DMA(())]),
        out_shape=jax.ShapeDtypeStruct((B, S, D), F32),
        compiler_params=_cparams(("parallel", "parallel", "arbitrary")),
        name="moe_ffn",
    )(nblk, xb, relt, afft, wg, wu, wd, x1, g, b)


def _proj_mla_kernel(x_ref, wdt_ref, gq_ref, gkv_ref, wqt_ref, wkvt_ref, c_ref, s_ref, qt_ref, k_ref, vt_ref):
    xb = x_ref[0].astype(BF16)
    dt = lax.dot_general(wdt_ref[...], xb, NT_DIMS, preferred_element_type=F32)
    cq = _rms_t(dt[0:Q_LORA], gq_ref[...], Q_LORA).astype(BF16)
    ckv = _rms_t(dt[Q_LORA:Q_LORA + KV_LORA], gkv_ref[...], KV_LORA).astype(BF16)
    r2 = QK_ROPE // 2
    c, s = c_ref[...], s_ref[...]
    kr = dt[Q_LORA + KV_LORA:Q_LORA + KV_LORA + QK_ROPE]
    kr1, kr2 = _rope_t(kr[0:r2], kr[r2:], c, s)
    qt = jnp.dot(wqt_ref[...], cq, preferred_element_type=F32)
    kvt = jnp.dot(wkvt_ref[...], ckv, preferred_element_type=F32)
    scale = (QK_NOPE + QK_ROPE) ** -0.5 * LOG2E
    dq, dkv = QK_NOPE + QK_ROPE, QK_NOPE + V_DIM
    for h in range(MLA_HEADS):
        q = qt[h * dq:(h + 1) * dq]
        q1, q2 = _rope_t(q[QK_NOPE:QK_NOPE + r2], q[QK_NOPE + r2:], c, s)
        _store_qt(qt_ref, h, jnp.concatenate([q[0:QK_NOPE], q1, q2], axis=0) * scale)
        kv = kvt[h * dkv:(h + 1) * dkv]
        _store_k(k_ref, h, jnp.concatenate([kv[0:QK_NOPE], kr1, kr2], axis=0))
        _store_vt(vt_ref, h, kv[QK_NOPE:])


def _proj_mla(x, wdt, gq, gkv, wqt, wkvt, tabs):
    B, S, D = x.shape
    tm = min(PROJ_ROWS, S)
    H = MLA_HEADS
    const = lambda b, i: (0, 0)
    full = lambda a: pl.BlockSpec(a.shape, const)
    tab = lambda t: pl.BlockSpec((t.shape[0], tm), lambda b, i: (0, i))
    gq_b = jnp.broadcast_to(gq.reshape(-1, 1), (Q_LORA, tm))
    gkv_b = jnp.broadcast_to(gkv.reshape(-1, 1), (KV_LORA, tm))
    return pl.pallas_call(
        _proj_mla_kernel,
        grid=(B, S // tm),
        in_specs=[pl.BlockSpec((1, tm, D), lambda b, i: (b, i, 0)),
                  full(wdt), full(gq_b), full(gkv_b), full(wqt), full(wkvt), tab(tabs[0]), tab(tabs[1])],
        out_specs=[pl.BlockSpec((1, H, LANES, tm), lambda b, i: (b, 0, 0, i)),
                   pl.BlockSpec((1, H, tm, LANES), lambda b, i: (b, 0, i, 0)),
                   pl.BlockSpec((1, H, LANES, tm), lambda b, i: (b, 0, 0, i))],
        out_shape=[jax.ShapeDtypeStruct((B, H, LANES, S), BF16),
                   jax.ShapeDtypeStruct((B, H, S, LANES), BF16),
                   jax.ShapeDtypeStruct((B, H, LANES, S), BF16)],
        compiler_params=_cparams(("parallel", "parallel")),
        name="proj_mla",
    )(x, wdt, gq_b, gkv_b, wqt, wkvt, *tabs)


def _angles(pos, dim):
    freqs = ROPE_THETA ** (-(jnp.arange(0, dim, 2, dtype=F32) / dim))
    return pos[:, None] * freqs[None, :]


def _tables(S):
    t = jnp.arange(S)
    ar = _angles((t // GRID_W).astype(F32), HEAD_DIM // 2)
    ac = _angles((t % GRID_W).astype(F32), HEAD_DIM // 2)
    axial = jnp.concatenate([ar, ac], axis=1).T
    seq = _angles(t.astype(F32), HEAD_DIM).T
    latent = _angles(t.astype(F32), QK_ROPE).T
    return tuple((jnp.cos(a), jnp.sin(a)) for a in (axial, seq, latent))


def _cast_kernel(*refs):
    n = len(refs) // 2
    for src, dst in zip(refs[:n], refs[n:]):
        dst[...] = src[...].astype(BF16)


def _experts_bf16(*ws):
    spec = lambda w: pl.BlockSpec((1,) + w.shape[1:], lambda e: (e, 0, 0))
    return pl.pallas_call(
        _cast_kernel,
        grid=(ws[0].shape[0],),
        in_specs=[spec(w) for w in ws],
        out_specs=[spec(w) for w in ws],
        out_shape=[jax.ShapeDtypeStruct(w.shape, BF16) for w in ws],
        compiler_params=_cparams(("parallel",)),
        name="experts_bf16",
    )(*ws)


def _split_bf16(w):
    hi = w.astype(BF16)
    return hi, (w - hi.astype(F32)).astype(BF16)


def _moe_layer(x1, x1b, aff, w_gate, w_up, w_down, g, b):
    B, S, D = x1.shape
    T = min(MOE_TOKENS, S)
    rel, cnt = _topk(aff, T)
    nblk = ((cnt.astype(jnp.int32) + MOE_SLOTS - 1) // MOE_SLOTS).transpose(0, 2, 1).reshape(-1)
    relt = rel.reshape(B, N_EXPERTS, 1, S)
    afft = aff.reshape(B, N_EXPERTS, 1, S)
    wg, wu, wd = _experts_bf16(w_gate, w_up, w_down)
    return _moe(x1b, relt, afft, nblk, wg, wu, wd, x1, g, b, T)


def kernel(x, ab_w_in, ab_q_norm, ab_k_norm, ab_sink, ab_w_out, mla_w_down, mla_q_norm, mla_kv_norm,
           mla_w_uq, mla_w_ukv, mla_w_out, ln_mix_g, ln_mix_b, moe_router, moe_w_gate, moe_w_up, moe_w_down,
           ln_ffn_g, ln_ffn_b):
    B, S, D = x.shape
    tabs_a, tabs_b, tabs_m = _tables(S)
    row = lambda v: v.reshape(1, -1)

    qta, ka, vta, qtb, kb, vtb = _proj_ab(x, ab_w_in[0].T.astype(BF16), ab_q_norm[0], ab_k_norm[0], tabs_a, tabs_b)
    oa = _flash(qta, ka, vta, HEAD_DIM)
    ob = _window(qtb, kb, vtb, ab_sink[0], HEAD_DIM)
    n_a = A_HEADS * HEAD_DIM
    w_oa = ab_w_out[0][:n_a].astype(BF16)
    w_ob = ab_w_out[0][n_a:].astype(BF16)
    wr_hi, wr_lo = _split_bf16(moe_router[0].T)
    x1, x1b, aff = _out_ln_router([oa, ob], [w_oa, w_ob], x, row(ln_mix_g[0]), row(ln_mix_b[0]), wr_hi, wr_lo)
    x = _moe_layer(x1, x1b, aff, moe_w_gate[0], moe_w_up[0], moe_w_down[0], row(ln_ffn_g[0]), row(ln_ffn_b[0]))

    qt, k, vt = _proj_mla(x, mla_w_down[0].T.astype(BF16), mla_q_norm[0], mla_kv_norm[0],
                          mla_w_uq[0].T.astype(BF16), mla_w_ukv[0].T.astype(BF16), tabs_m)
    oc = _flash(qt, k, vt, V_DIM)
    w_oc = mla_w_out[0].astype(BF16)
    wr_hi, wr_lo = _split_bf16(moe_router[1].T)
    x1, x1b, aff = _out_ln_router([oc], [w_oc], x, row(ln_mix_g[1]), row(ln_mix_b[1]), wr_hi, wr_lo)
    x = _moe_layer(x1, x1b, aff, moe_w_gate[1], moe_w_up[1], moe_w_down[1], row(ln_ffn_g[1]), row(ln_ffn_b[1]))
    return x
```

```python
import functools

import jax
import jax.numpy as jnp
from jax import lax
from jax.experimental import pallas as pl
from jax.experimental.pallas import tpu as pltpu

F32 = jnp.float32
BF16 = jnp.bfloat16

GRID_W = 64
ROPE_THETA = 10000.0
HEAD_DIM = 64
A_HEADS, A_KV = 8, 2
B_HEADS, B_KV = 8, 2
WINDOW = 128
MLA_HEADS = 16
Q_LORA, KV_LORA = 256, 128
QK_NOPE, QK_ROPE, V_DIM = 64, 32, 64
N_EXPERTS = 16
EC_FACTOR = 2
DEPTH = 2
ALPHA = (2.0 * DEPTH) ** 0.25
NEG_INF = -1e30
SHIFT_SLACK = 64.0
RMS_EPS = 1e-6
LOG2E = 1.4426950408889634
LN_EPS = 1e-5

LANES = 128
VMEM_LIMIT = 56 * 1024 * 1024

PROJ_ROWS = 512
ATTN_ROWS = 4096
ATTN_KEYS = 2048
ATTN_KEY_SUB = 512
ATTN_COLS = 256
AHEAD = 2
WIN_Q = 1024
MOE_TOKENS = 1024
MOE_TILES = 2
MOE_SLOTS = 144
CUMSUM_CHUNK = 256

NT_DIMS = (((1,), (1,)), ((), ()))
TN_DIMS = (((0,), (0,)), ((), ()))


def _cparams(sem):
    return pltpu.CompilerParams(dimension_semantics=sem, vmem_limit_bytes=VMEM_LIMIT)


def _rope_t(x1, x2, c, s):
    return x1 * c - x2 * s, x2 * c + x1 * s


def _rms_t(x, g, n):
    ms = jnp.sum(x * x, axis=0, keepdims=True) * (1.0 / n)
    return x * lax.rsqrt(ms + RMS_EPS) * g


def _layer_norm(y, g, b):
    mu = jnp.mean(y, axis=1, keepdims=True)
    d = y - mu
    var = jnp.mean(d * d, axis=1, keepdims=True)
    return d * lax.rsqrt(var + LN_EPS) * g + b


def _store_qt(ref, h, q):
    d = q.shape[0]
    ref[0, h, 0:d, :] = q.astype(BF16)
    ref[0, h, d:LANES, :] = jnp.zeros((LANES - d, q.shape[1]), BF16)


def _store_k(ref, h, kt):
    d, t = kt.shape
    ref[0, h] = jnp.concatenate([kt, jnp.zeros((LANES - d, t), F32)], axis=0).T.astype(BF16)


def _store_vt(ref, h, vt):
    d, t = vt.shape
    ref[0, h, 0:d, :] = vt.astype(BF16)
    row = lax.broadcasted_iota(jnp.int32, (LANES - d, t), 0)
    ref[0, h, d:LANES, :] = jnp.where(row == 0, 1.0, 0.0).astype(BF16)


def _proj_ab_kernel(x_ref, wt_ref, gq_ref, gk_ref, ca_ref, sa_ref, cb_ref, sb_ref,
                    qa_ref, ka_ref, va_ref, qb_ref, kb_ref, vb_ref):
    xb = x_ref[0].astype(BF16)
    pt = lax.dot_general(wt_ref[...], xb, NT_DIMS, preferred_element_type=F32)
    scale = HEAD_DIM ** -0.5 * LOG2E
    q4, h2 = HEAD_DIM // 4, HEAD_DIM // 2
    ca, sa, cb, sb = ca_ref[...], sa_ref[...], cb_ref[...], sb_ref[...]

    def rope_a(p):
        r1, r2 = _rope_t(p[0:q4], p[q4:2 * q4], ca[0:q4], sa[0:q4])
        c1, c2 = _rope_t(p[2 * q4:3 * q4], p[3 * q4:], ca[q4:], sa[q4:])
        return jnp.concatenate([r1, r2, c1, c2], axis=0)

    def rope_b(p):
        x1, x2 = _rope_t(p[0:h2], p[h2:], cb, sb)
        return jnp.concatenate([x1, x2], axis=0)

    heads = iter(pt[g * HEAD_DIM:(g + 1) * HEAD_DIM] for g in range(pt.shape[0] // HEAD_DIM))
    for h in range(A_HEADS):
        _store_qt(qa_ref, h, rope_a(_rms_t(next(heads), gq_ref[...], HEAD_DIM)) * scale)
    for h in range(A_KV):
        _store_k(ka_ref, h, rope_a(_rms_t(next(heads), gk_ref[...], HEAD_DIM)))
    for h in range(A_KV):
        _store_vt(va_ref, h, next(heads))
    for h in range(B_HEADS):
        _store_qt(qb_ref, h, rope_b(next(heads)) * scale)
    for h in range(B_KV):
        _store_k(kb_ref, h, rope_b(next(heads)))
    for h in range(B_KV):
        _store_vt(vb_ref, h, next(heads))


def _proj_ab(x, wt, gq, gk, tabs_a, tabs_b):
    B, S, D = x.shape
    tm = min(PROJ_ROWS, S)
    const = lambda b, i: (0, 0)
    tab = lambda t: pl.BlockSpec((t.shape[0], tm), lambda b, i: (0, i))
    hm = lambda h: pl.BlockSpec((1, h, tm, LANES), lambda b, i: (b, 0, i, 0))
    tr = lambda h: pl.BlockSpec((1, h, LANES, tm), lambda b, i: (b, 0, 0, i))
    sd = lambda h: jax.ShapeDtypeStruct((B, h, S, LANES), BF16)
    sdt = lambda h: jax.ShapeDtypeStruct((B, h, LANES, S), BF16)
    gq_b = jnp.broadcast_to(gq.reshape(-1, 1), (HEAD_DIM, tm))
    gk_b = jnp.broadcast_to(gk.reshape(-1, 1), (HEAD_DIM, tm))
    return pl.pallas_call(
        _proj_ab_kernel,
        grid=(B, S // tm),
        in_specs=[pl.BlockSpec((1, tm, D), lambda b, i: (b, i, 0)),
                  pl.BlockSpec(wt.shape, const),
                  pl.BlockSpec(gq_b.shape, const), pl.BlockSpec(gk_b.shape, const),
                  tab(tabs_a[0]), tab(tabs_a[1]), tab(tabs_b[0]), tab(tabs_b[1])],
        out_specs=[tr(A_HEADS), hm(A_KV), tr(A_KV), tr(B_HEADS), hm(B_KV), tr(B_KV)],
        out_shape=[sdt(A_HEADS), sd(A_KV), sdt(A_KV), sdt(B_HEADS), sd(B_KV), sdt(B_KV)],
        compiler_params=_cparams(("parallel", "parallel")),
        name="proj_ab",
    )(x, wt, gq_b, gk_b, *tabs_a, *tabs_b)


def _flash_kernel(qt_ref, k_ref, vt_ref, o_ref, m_sc, acc_sc, pv_sc, *, NKV, G, tq, tk, ks, nk, cw, vd):
    heads = NKV * G
    M = heads * tq
    n_sub = M // cw

    def q_tile(c):
        h, j = divmod(c * cw, tq)
        return qt_ref[0, h, :, j:j + cw]

    def kv_of(c):
        return (c * cw // tq) // G

    def cols(c):
        return slice(c * cw, (c + 1) * cw)

    for c in range(n_sub):
        s0 = jnp.dot(k_ref[0, kv_of(c), 0:LANES, :], q_tile(c), preferred_element_type=F32)
        m_sc[:, cols(c)] = jnp.max(s0, axis=0, keepdims=True)
    acc_sc[...] = jnp.zeros((LANES, M), F32)
    pv_sc[...] = jnp.zeros((LANES, M), F32)

    def body(kb, carry):
        off = pl.multiple_of(kb * tk, tk)
        acc_sc[...] += pv_sc[...]

        def keys(c, s):
            return k_ref[0, kv_of(c), pl.ds(pl.multiple_of(off + s * ks, ks), ks), :]

        def values_t(c, s=None):
            if s is None:
                return vt_ref[0, kv_of(c), :, pl.ds(off, tk)]
            return vt_ref[0, kv_of(c), :, pl.ds(pl.multiple_of(off + s * ks, ks), ks)]

        tiles = [(c, s) for c in range(n_sub) for s in range(tk // ks)]

        def scores(t):
            c, s = tiles[t]
            return jnp.dot(keys(c, s), q_tile(c), preferred_element_type=F32)

        excess = None
        ahead = {t: scores(t) for t in range(min(AHEAD, len(tiles)))}
        pts = []
        for t, (c, s) in enumerate(tiles):
            st = ahead.pop(t)
            if t + AHEAD < len(tiles):
                ahead[t + AHEAD] = scores(t + AHEAD)
            m = m_sc[:, cols(c)]
            over = jnp.max(st, axis=0, keepdims=True) - m
            excess = over if excess is None else jnp.maximum(excess, over)
            pts.append(jnp.exp2(st - m).astype(BF16))
            if s == tk // ks - 1:
                pt = pts[0] if len(pts) == 1 else jnp.concatenate(pts, axis=0)
                pv_sc[:, cols(c)] = jnp.dot(values_t(c), pt, preferred_element_type=F32)
                pts = []
        renew = jnp.max(excess) > SHIFT_SLACK

        @pl.when(renew)
        def _():
            pv_sc[...] = jnp.zeros((LANES, M), F32)
            for c, s in tiles:
                st = jnp.dot(keys(c, s), q_tile(c), preferred_element_type=F32)
                m_prev = m_sc[:, cols(c)]
                m_new = jnp.maximum(m_prev, jnp.max(st, axis=0, keepdims=True))
                alpha = jnp.exp2(m_prev - m_new)
                pt = jnp.exp2(st - m_new).astype(BF16)
                acc_sc[:, cols(c)] = (acc_sc[:, cols(c)] * alpha
                                      + jnp.dot(values_t(c, s), pt, preferred_element_type=F32))
                m_sc[:, cols(c)] = m_new

        return carry

    lax.fori_loop(0, nk, body, 0)
    acc = acc_sc[0:2 * vd, :] + pv_sc[0:2 * vd, :]
    o = acc[0:vd, :] / acc[vd:vd + 1, :]
    for p in range(heads // 2):
        pair = jnp.concatenate([o[:, 2 * p * tq:(2 * p + 1) * tq], o[:, (2 * p + 1) * tq:(2 * p + 2) * tq]], axis=0)
        o_ref[0, :, p * LANES:(p + 1) * LANES] = pair.T.astype(BF16)


def _flash(qt, k, vt, vd):
    B, H, _, S = qt.shape
    HK = k.shape[1]
    G = H // HK
    assert 2 * vd == LANES
    NKV = max(1, 2 // G)
    heads = NKV * G
    tq = min(ATTN_ROWS // heads, S)
    tk = min(ATTN_KEYS, S)
    M = heads * tq
    cw = min(ATTN_COLS, tq)
    ks = min(ATTN_KEY_SUB, tk)
    return pl.pallas_call(
        functools.partial(_flash_kernel, NKV=NKV, G=G, tq=tq, tk=tk, ks=ks, nk=S // tk, cw=cw, vd=vd),
        grid=(B, HK // NKV, S // tq),
        in_specs=[pl.BlockSpec((1, heads, LANES, tq), lambda b, h, i: (b, h, 0, i)),
                  pl.BlockSpec((1, NKV, S, LANES), lambda b, h, i: (b, h, 0, 0)),
                  pl.BlockSpec((1, NKV, LANES, S), lambda b, h, i: (b, h, 0, 0))],
        out_specs=pl.BlockSpec((1, tq, heads * vd), lambda b, h, i: (b, i, h)),
        out_shape=jax.ShapeDtypeStruct((B, S, H * vd), BF16),
        scratch_shapes=[pltpu.VMEM((1, M), F32), pltpu.VMEM((LANES, M), F32), pltpu.VMEM((LANES, M), F32)],
        compiler_params=_cparams(("parallel", "parallel", "parallel")),
        name="flash_attn",
    )(qt, k, vt)


def _window_kernel(sink_ref, qt_ref, k_ref, vt_ref, o_ref, *, G, tq, span, S, vd):
    kvh = pl.program_id(1)
    start = pl.program_id(2) * tq
    for j in range(tq // LANES):
        q0 = start + j * LANES
        kstart = pl.multiple_of(jnp.clip(q0 - WINDOW, 0, S - span), LANES)
        k = k_ref[0, 0, pl.ds(kstart, span), :]
        vt = vt_ref[0, 0, :, pl.ds(kstart, span)]
        kpos = kstart + lax.broadcasted_iota(jnp.int32, (span, 1), 0)
        qpos = q0 + lax.broadcasted_iota(jnp.int32, (1, LANES), 1)
        valid = jnp.abs(qpos - kpos) <= WINDOW
        outs = []
        for g in range(G):
            st = jnp.dot(k, qt_ref[0, g, :, j * LANES:(j + 1) * LANES], preferred_element_type=F32)
            st = jnp.where(valid, st, NEG_INF)
            sink = sink_ref[kvh * G + g] * LOG2E
            m = jnp.maximum(jnp.max(st, axis=0, keepdims=True), sink)
            pt = jnp.exp2(st - m).astype(BF16)
            acc = jnp.dot(vt, pt, preferred_element_type=F32)
            outs.append(acc[0:vd, :] / (acc[vd:vd + 1, :] + jnp.exp2(sink - m)))
        for p in range(G // 2):
            pair = jnp.concatenate([outs[2 * p], outs[2 * p + 1]], axis=0)
            o_ref[0, j * LANES:(j + 1) * LANES, p * LANES:(p + 1) * LANES] = pair.T.astype(BF16)


def _window(qt, k, vt, sink, vd):
    B, H, _, S = qt.shape
    HK = k.shape[1]
    G = H // HK
    tq = min(WIN_Q, S)
    span = min(LANES + 2 * WINDOW, S)
    return pl.pallas_call(
        functools.partial(_window_kernel, G=G, tq=tq, span=span, S=S, vd=vd),
        grid_spec=pltpu.PrefetchScalarGridSpec(
            num_scalar_prefetch=1,
            grid=(B, HK, S // tq),
            in_specs=[pl.BlockSpec((1, G, LANES, tq), lambda b, h, i, sk: (b, h, 0, i)),
                      pl.BlockSpec((1, 1, S, LANES), lambda b, h, i, sk: (b, h, 0, 0)),
                      pl.BlockSpec((1, 1, LANES, S), lambda b, h, i, sk: (b, h, 0, 0))],
            out_specs=pl.BlockSpec((1, tq, G * vd), lambda b, h, i, sk: (b, i, h))),
        out_shape=jax.ShapeDtypeStruct((B, S, H * vd), BF16),
        compiler_params=_cparams(("parallel", "parallel", "parallel")),
        name="window_attn",
    )(sink, qt, k, vt)


def _out_ln_router_kernel(*refs, n_in):
    o_refs = refs[:n_in]
    w_refs = refs[n_in:2 * n_in]
    x_ref, g_ref, b_ref, wrh_ref, wrl_ref, x1_ref, x1b_ref, aff_ref = refs[2 * n_in:]
    h = jnp.dot(o_refs[0][0], w_refs[0][...], preferred_element_type=F32)
    for i in range(1, n_in):
        h = h + jnp.dot(o_refs[i][0], w_refs[i][...], preferred_element_type=F32)
    x1 = _layer_norm(ALPHA * x_ref[0] + h, g_ref[...], b_ref[...])
    x1_ref[0] = x1
    x_hi = x1.astype(BF16)
    x1b_ref[0] = x_hi
    x_lo = (x1 - x_hi.astype(F32)).astype(BF16)
    logits = (lax.dot_general(wrh_ref[...], x_hi, NT_DIMS, preferred_element_type=F32)
              + lax.dot_general(wrh_ref[...], x_lo, NT_DIMS, preferred_element_type=F32)
              + lax.dot_general(wrl_ref[...], x_hi, NT_DIMS, preferred_element_type=F32))
    z = jnp.exp(logits - jnp.max(logits, axis=0, keepdims=True))
    aff_ref[0] = z / jnp.sum(z, axis=0, keepdims=True)


def _out_ln_router(os_, ws_, x, g, b, wr_hi, wr_lo):
    B, S, D = x.shape
    E = wr_hi.shape[0]
    tm = min(PROJ_ROWS, S)
    n_in = len(os_)
    row = lambda bb, i: (bb, i, 0)
    const = lambda bb, i: (0, 0)
    in_specs = ([pl.BlockSpec((1, tm, o.shape[2]), row) for o in os_]
                + [pl.BlockSpec(w.shape, const) for w in ws_]
                + [pl.BlockSpec((1, tm, D), row), pl.BlockSpec((1, D), const), pl.BlockSpec((1, D), const),
                   pl.BlockSpec((E, D), const), pl.BlockSpec((E, D), const)])
    return pl.pallas_call(
        functools.partial(_out_ln_router_kernel, n_in=n_in),
        grid=(B, S // tm),
        in_specs=in_specs,
        out_specs=[pl.BlockSpec((1, tm, D), row), pl.BlockSpec((1, tm, D), row),
                   pl.BlockSpec((1, E, tm), lambda bb, i: (bb, 0, i))],
        out_shape=[jax.ShapeDtypeStruct((B, S, D), F32), jax.ShapeDtypeStruct((B, S, D), BF16),
                   jax.ShapeDtypeStruct((B, E, S), F32)],
        compiler_params=_cparams(("parallel", "parallel")),
        name="out_ln_router",
    )(*os_, *ws_, x, g, b, wr_hi, wr_lo)


def _topk_kernel(aff_ref, tri_ref, rel_ref, cnt_ref, *, S, E, cap, T, CH):
    aff = aff_ref[0]
    bits = pltpu.bitcast(aff, jnp.int32)
    capf = jnp.float32(cap)

    def count(mask):
        return jnp.sum(jnp.where(mask, 1.0, 0.0), axis=1, keepdims=True)

    def thr_body(i, t):
        cand = t | jnp.left_shift(jnp.int32(1), 30 - i)
        return jnp.where(count(bits >= cand) >= capf, cand, t)

    thr = lax.fori_loop(0, 31, thr_body, jnp.zeros((E, 1), jnp.int32))
    gt = bits > thr
    ties = bits == thr
    need = capf - count(gt)
    idx = lax.broadcasted_iota(jnp.int32, (E, S), 1)
    nbits = max(1, (S - 1).bit_length())

    def cut_body(i, c):
        cand = c | jnp.left_shift(jnp.int32(1), nbits - 1 - i)
        return jnp.where(count(ties & (idx < cand)) < need, cand, c)

    cut = lax.fori_loop(0, nbits, cut_body, jnp.zeros((E, 1), jnp.int32))
    sel = gt | (ties & (idx <= cut))

    tri = tri_ref[...]
    nsc = S // T
    lane_sc = lax.broadcasted_iota(jnp.int32, (E, nsc), 1)
    cnt = jnp.zeros((E, nsc), F32)
    for sc in range(nsc):
        run = jnp.zeros((E, 1), F32)
        for ch in range(T // CH):
            lo = sc * T + ch * CH
            selc = jnp.where(sel[:, lo:lo + CH], 1.0, 0.0)
            incl = jnp.dot(selc.astype(BF16), tri, preferred_element_type=F32)
            rel = jnp.where(selc > 0.0, incl - 1.0 + run, -1.0)
            rel_ref[0, :, lo:lo + CH] = rel
            run = run + incl[:, CH - 1:CH]
        cnt = jnp.where(lane_sc == sc, run, cnt)
    cnt_ref[0] = cnt


def _topk(aff, T):
    B, E, S = aff.shape
    cap = EC_FACTOR * S // N_EXPERTS
    CH = min(CUMSUM_CHUNK, T)
    r = lax.broadcasted_iota(jnp.int32, (CH, CH), 0)
    c = lax.broadcasted_iota(jnp.int32, (CH, CH), 1)
    tri = jnp.where(r <= c, 1.0, 0.0).astype(BF16)
    nsc = S // T
    return pl.pallas_call(
        functools.partial(_topk_kernel, S=S, E=E, cap=cap, T=T, CH=CH),
        grid=(B,),
        in_specs=[pl.BlockSpec((1, E, S), lambda b: (b, 0, 0)),
                  pl.BlockSpec((CH, CH), lambda b: (0, 0))],
        out_specs=[pl.BlockSpec((1, E, S), lambda b: (b, 0, 0)),
                   pl.BlockSpec((1, E, nsc), lambda b: (b, 0, 0))],
        out_shape=[jax.ShapeDtypeStruct((B, E, S), F32), jax.ShapeDtypeStruct((B, E, nsc), F32)],
        compiler_params=_cparams(("parallel",)),
        name="topk_select",
    )(aff, tri)


def _moe_kernel(nblk_ref, xb_ref, relt_ref, afft_ref, wg_ref, wu_ref, wd_ref, x1_hbm, g_ref, b_ref,
                f_ref, x1_buf, x1_sem, *, T, NT, RB, E, nsc, TS):
    b, sg, e = pl.program_id(0), pl.program_id(1), pl.program_id(2)

    @pl.when(e == 0)
    def _():
        f_ref[...] = jnp.zeros(f_ref.shape, F32)

    def x1_copy():
        return pltpu.make_async_copy(x1_hbm.at[b, pl.ds(sg * NT * T, NT * T), :], x1_buf, x1_sem)

    @pl.when(e == E - 2)
    def _():
        x1_copy().start()

    for t in range(NT):
        tok = slice(t * T, (t + 1) * T)
        nb = nblk_ref[(b * nsc + sg * NT + t) * E + e]
        relt = relt_ref[0, 0, :, tok]
        afft = afft_ref[0, 0, :, tok]
        row_tiles = [slice(t * T + ts * TS, t * T + (ts + 1) * TS) for ts in range(T // TS)]

        def slot_block(j, carry, tok=tok, relt=relt, afft=afft, row_tiles=row_tiles):
            slot = (j * RB + lax.broadcasted_iota(jnp.int32, (RB, 1), 0)).astype(F32)
            hit = relt == slot
            gate = jnp.sum(jnp.where(hit, afft, 0.0), axis=1, keepdims=True)
            onehot = jnp.where(hit, 1.0, 0.0).astype(BF16)
            xg = jnp.dot(onehot, xb_ref[0, tok, :], preferred_element_type=F32).astype(BF16)
            hg = jnp.dot(xg, wg_ref[0], preferred_element_type=F32)
            hu = jnp.dot(xg, wu_ref[0], preferred_element_type=F32)
            h = (hg / (1.0 + jnp.exp(-hg)) * hu).astype(BF16)
            y = (jnp.dot(h, wd_ref[0], preferred_element_type=F32) * gate).astype(BF16)
            for ts, rows in enumerate(row_tiles):
                cols = slice(ts * TS, (ts + 1) * TS)
                f_ref[0, rows, :] += lax.dot_general(onehot[:, cols], y, TN_DIMS, preferred_element_type=F32)
            return carry

        lax.fori_loop(0, nb, slot_block, 0)

    @pl.when(e == E - 1)
    def _():
        x1_copy().wait()
        for r in range(NT * T // TS):
            rows = slice(r * TS, (r + 1) * TS)
            f_ref[0, rows, :] = _layer_norm(ALPHA * x1_buf[rows, :] + f_ref[0, rows, :], g_ref[...], b_ref[...])


def _moe(xb, relt, afft, nblk, wg, wu, wd, x1, g, b, T):
    B, S, D = xb.shape
    E, _, F = wg.shape
    nsc = S // T
    NT = min(MOE_TILES, nsc)
    assert nsc % NT == 0 and E >= 2
    RB = MOE_SLOTS
    TS = min(512, T)
    return pl.pallas_call(
        functools.partial(_moe_kernel, T=T, NT=NT, RB=RB, E=E, nsc=nsc, TS=TS),
        grid_spec=pltpu.PrefetchScalarGridSpec(
            num_scalar_prefetch=1,
            grid=(B, nsc // NT, E),
            in_specs=[pl.BlockSpec((1, NT * T, D), lambda b, s, e, n: (b, s, 0)),
                      pl.BlockSpec((1, 1, 1, NT * T), lambda b, s, e, n: (b, e, 0, s)),
                      pl.BlockSpec((1, 1, 1, NT * T), lambda b, s, e, n: (b, e, 0, s)),
                      pl.BlockSpec((1, D, F), lambda b, s, e, n: (e, 0, 0)),
                      pl.BlockSpec((1, D, F), lambda b, s, e, n: (e, 0, 0)),
                      pl.BlockSpec((1, F, D), lambda b, s, e, n: (e, 0, 0)),
                      pl.BlockSpec(memory_space=pl.ANY),
                      pl.BlockSpec((1, D), lambda b, s, e, n: (0, 0)),
                      pl.BlockSpec((1, D), lambda b, s, e, n: (0, 0))],
            out_specs=pl.BlockSpec((1, NT * T, D), lambda b, s, e, n: (b, s, 0)),
            scratch_shapes=[pltpu.VMEM((NT * T, D), F32), pltpu.SemaphoreType.DMA(())]),
        out_shape=jax.ShapeDtypeStruct((B, S, D), F32),
        compiler_params=_cparams(("parallel", "parallel", "arbitrary")),
        name="moe_ffn",
    )(nblk, xb, relt, afft, wg, wu, wd, x1, g, b)


def _proj_mla_kernel(x_ref, wdt_ref, gq_ref, gkv_ref, wqt_ref, wkvt_ref, c_ref, s_ref, qt_ref, k_ref, vt_ref):
    xb = x_ref[0].astype(BF16)
    dt = lax.dot_general(wdt_ref[...], xb, NT_DIMS, preferred_element_type=F32)
    cq = _rms_t(dt[0:Q_LORA], gq_ref[...], Q_LORA).astype(BF16)
    ckv = _rms_t(dt[Q_LORA:Q_LORA + KV_LORA], gkv_ref[...], KV_LORA).astype(BF16)
    r2 = QK_ROPE // 2
    c, s = c_ref[...], s_ref[...]
    kr = dt[Q_LORA + KV_LORA:Q_LORA + KV_LORA + QK_ROPE]
    kr1, kr2 = _rope_t(kr[0:r2], kr[r2:], c, s)
    qt = jnp.dot(wqt_ref[...], cq, preferred_element_type=F32)
    kvt = jnp.dot(wkvt_ref[...], ckv, preferred_element_type=F32)
    scale = (QK_NOPE + QK_ROPE) ** -0.5 * LOG2E
    dq, dkv = QK_NOPE + QK_ROPE, QK_NOPE + V_DIM
    for h in range(MLA_HEADS):
        q = qt[h * dq:(h + 1) * dq]
        q1, q2 = _rope_t(q[QK_NOPE:QK_NOPE + r2], q[QK_NOPE + r2:], c, s)
        _store_qt(qt_ref, h, jnp.concatenate([q[0:QK_NOPE], q1, q2], axis=0) * scale)
        kv = kvt[h * dkv:(h + 1) * dkv]
        _store_k(k_ref, h, jnp.concatenate([kv[0:QK_NOPE], kr1, kr2], axis=0))
        _store_vt(vt_ref, h, kv[QK_NOPE:])


def _proj_mla(x, wdt, gq, gkv, wqt, wkvt, tabs):
    B, S, D = x.shape
    tm = min(PROJ_ROWS, S)
    H = MLA_HEADS
    const = lambda b, i: (0, 0)
    full = lambda a: pl.BlockSpec(a.shape, const)
    tab = lambda t: pl.BlockSpec((t.shape[0], tm), lambda b, i: (0, i))
    gq_b = jnp.broadcast_to(gq.reshape(-1, 1), (Q_LORA, tm))
    gkv_b = jnp.broadcast_to(gkv.reshape(-1, 1), (KV_LORA, tm))
    return pl.pallas_call(
        _proj_mla_kernel,
        grid=(B, S // tm),
        in_specs=[pl.BlockSpec((1, tm, D), lambda b, i: (b, i, 0)),
                  full(wdt), full(gq_b), full(gkv_b), full(wqt), full(wkvt), tab(tabs[0]), tab(tabs[1])],
        out_specs=[pl.BlockSpec((1, H, LANES, tm), lambda b, i: (b, 0, 0, i)),
                   pl.BlockSpec((1, H, tm, LANES), lambda b, i: (b, 0, i, 0)),
                   pl.BlockSpec((1, H, LANES, tm), lambda b, i: (b, 0, 0, i))],
        out_shape=[jax.ShapeDtypeStruct((B, H, LANES, S), BF16),
                   jax.ShapeDtypeStruct((B, H, S, LANES), BF16),
                   jax.ShapeDtypeStruct((B, H, LANES, S), BF16)],
        compiler_params=_cparams(("parallel", "parallel")),
        name="proj_mla",
    )(x, wdt, gq_b, gkv_b, wqt, wkvt, *tabs)


def _angles(pos, dim):
    freqs = ROPE_THETA ** (-(jnp.arange(0, dim, 2, dtype=F32) / dim))
    return pos[:, None] * freqs[None, :]


def _tables(S):
    t = jnp.arange(S)
    ar = _angles((t // GRID_W).astype(F32), HEAD_DIM // 2)
    ac = _angles((t % GRID_W).astype(F32), HEAD_DIM // 2)
    axial = jnp.concatenate([ar, ac], axis=1).T
    seq = _angles(t.astype(F32), HEAD_DIM).T
    latent = _angles(t.astype(F32), QK_ROPE).T
    return tuple((jnp.cos(a), jnp.sin(a)) for a in (axial, seq, latent))


def _cast_kernel(*refs):
    n = len(refs) // 2
    for src, dst in zip(refs[:n], refs[n:]):
        dst[...] = src[...].astype(BF16)


def _experts_bf16(layer, *ws):
    return pl.pallas_call(
        _cast_kernel,
        grid=(ws[0].shape[1],),
        in_specs=[pl.BlockSpec((None, 1) + w.shape[2:], lambda e: (layer, e, 0, 0)) for w in ws],
        out_specs=[pl.BlockSpec((1,) + w.shape[2:], lambda e: (e, 0, 0)) for w in ws],
        out_shape=[jax.ShapeDtypeStruct(w.shape[1:], BF16) for w in ws],
        compiler_params=_cparams(("parallel",)),
        name="experts_bf16",
    )(*ws)


def _split_bf16(w):
    hi = w.astype(BF16)
    return hi, (w - hi.astype(F32)).astype(BF16)


def _moe_layer(layer, x1, x1b, aff, w_gate, w_up, w_down, g, b):
    B, S, D = x1.shape
    T = min(MOE_TOKENS, S)
    rel, cnt = _topk(aff, T)
    nblk = ((cnt.astype(jnp.int32) + MOE_SLOTS - 1) // MOE_SLOTS).transpose(0, 2, 1).reshape(-1)
    relt = rel.reshape(B, N_EXPERTS, 1, S)
    afft = aff.reshape(B, N_EXPERTS, 1, S)
    wg, wu, wd = _experts_bf16(layer, w_gate, w_up, w_down)
    return _moe(x1b, relt, afft, nblk, wg, wu, wd, x1, g, b, T)


def kernel(x, ab_w_in, ab_q_norm, ab_k_norm, ab_sink, ab_w_out, mla_w_down, mla_q_norm, mla_kv_norm,
           mla_w_uq, mla_w_ukv, mla_w_out, ln_mix_g, ln_mix_b, moe_router, moe_w_gate, moe_w_up, moe_w_down,
           ln_ffn_g, ln_ffn_b):
    B, S, D = x.shape
    tabs_a, tabs_b, tabs_m = _tables(S)
    row = lambda v: v.reshape(1, -1)

    qta, ka, vta, qtb, kb, vtb = _proj_ab(x, ab_w_in[0].T.astype(BF16), ab_q_norm[0], ab_k_norm[0], tabs_a, tabs_b)
    oa = _flash(qta, ka, vta, HEAD_DIM)
    ob = _window(qtb, kb, vtb, ab_sink[0], HEAD_DIM)
    n_a = A_HEADS * HEAD_DIM
    w_oa = ab_w_out[0][:n_a].astype(BF16)
    w_ob = ab_w_out[0][n_a:].astype(BF16)
    wr_hi, wr_lo = _split_bf16(moe_router[0].T)
    x1, x1b, aff = _out_ln_router([oa, ob], [w_oa, w_ob], x, row(ln_mix_g[0]), row(ln_mix_b[0]), wr_hi, wr_lo)
    x = _moe_layer(0, x1, x1b, aff, moe_w_gate, moe_w_up, moe_w_down, row(ln_ffn_g[0]), row(ln_ffn_b[0]))

    qt, k, vt = _proj_mla(x, mla_w_down[0].T.astype(BF16), mla_q_norm[0], mla_kv_norm[0],
                          mla_w_uq[0].T.astype(BF16), mla_w_ukv[0].T.astype(BF16), tabs_m)
    oc = _flash(qt, k, vt, V_DIM)
    w_oc = mla_w_out[0].astype(BF16)
    wr_hi, wr_lo = _split_bf16(moe_router[1].T)
    x1, x1b, aff = _out_ln_router([oc], [w_oc], x, row(ln_mix_g[1]), row(ln_mix_b[1]), wr_hi, wr_lo)
    x = _moe_layer(1, x1, x1b, aff, moe_w_gate, moe_w_up, moe_w_down, row(ln_ffn_g[1]), row(ln_ffn_b[1]))
    return x
```

```python
import functools

import jax
import jax.numpy as jnp
from jax import lax
from jax.experimental import pallas as pl
from jax.experimental.pallas import tpu as pltpu

F32 = jnp.float32
BF16 = jnp.bfloat16

GRID_W = 64
ROPE_THETA = 10000.0
HEAD_DIM = 64
A_HEADS, A_KV = 8, 2
B_HEADS, B_KV = 8, 2
WINDOW = 128
MLA_HEADS = 16
Q_LORA, KV_LORA = 256, 128
QK_NOPE, QK_ROPE, V_DIM = 64, 32, 64
N_EXPERTS = 16
EC_FACTOR = 2
DEPTH = 2
ALPHA = (2.0 * DEPTH) ** 0.25
NEG_INF = -1e30
SHIFT_SLACK = 64.0
RMS_EPS = 1e-6
LOG2E = 1.4426950408889634
LN_EPS = 1e-5

LANES = 128
VMEM_LIMIT = 56 * 1024 * 1024

PROJ_ROWS = 512
ATTN_ROWS = 4096
ATTN_KEYS = 2048
ATTN_KEY_SUB = 512
ATTN_COLS = 256
AHEAD = 2
WIN_Q = 1024
MOE_TOKENS = 1024
MOE_TILES = 2
MOE_SLOTS = 144
CUMSUM_CHUNK = 256

NT_DIMS = (((1,), (1,)), ((), ()))
TN_DIMS = (((0,), (0,)), ((), ()))


def _cparams(sem):
    return pltpu.CompilerParams(dimension_semantics=sem, vmem_limit_bytes=VMEM_LIMIT)


def _rope_t(x1, x2, c, s):
    return x1 * c - x2 * s, x2 * c + x1 * s


def _rms_t(x, g, n):
    ms = jnp.sum(x * x, axis=0, keepdims=True) * (1.0 / n)
    return x * lax.rsqrt(ms + RMS_EPS) * g


def _layer_norm(y, g, b):
    mu = jnp.mean(y, axis=1, keepdims=True)
    d = y - mu
    var = jnp.mean(d * d, axis=1, keepdims=True)
    return d * lax.rsqrt(var + LN_EPS) * g + b


def _store_qt(ref, h, q):
    d = q.shape[0]
    ref[0, h, 0:d, :] = q.astype(BF16)
    ref[0, h, d:LANES, :] = jnp.zeros((LANES - d, q.shape[1]), BF16)


def _store_k(ref, h, kt):
    d, t = kt.shape
    ref[0, h] = jnp.concatenate([kt, jnp.zeros((LANES - d, t), F32)], axis=0).T.astype(BF16)


def _store_vt(ref, h, vt):
    d, t = vt.shape
    ref[0, h, 0:d, :] = vt.astype(BF16)
    row = lax.broadcasted_iota(jnp.int32, (LANES - d, t), 0)
    ref[0, h, d:LANES, :] = jnp.where(row == 0, 1.0, 0.0).astype(BF16)


def _proj_ab_kernel(x_ref, wt_ref, gq_ref, gk_ref, ca_ref, sa_ref, cb_ref, sb_ref,
                    qa_ref, ka_ref, va_ref, qb_ref, kb_ref, vb_ref):
    xb = x_ref[0].astype(BF16)
    pt = lax.dot_general(wt_ref[...], xb, NT_DIMS, preferred_element_type=F32)
    scale = HEAD_DIM ** -0.5 * LOG2E
    q4, h2 = HEAD_DIM // 4, HEAD_DIM // 2
    ca, sa, cb, sb = ca_ref[...], sa_ref[...], cb_ref[...], sb_ref[...]

    def rope_a(p):
        r1, r2 = _rope_t(p[0:q4], p[q4:2 * q4], ca[0:q4], sa[0:q4])
        c1, c2 = _rope_t(p[2 * q4:3 * q4], p[3 * q4:], ca[q4:], sa[q4:])
        return jnp.concatenate([r1, r2, c1, c2], axis=0)

    def rope_b(p):
        x1, x2 = _rope_t(p[0:h2], p[h2:], cb, sb)
        return jnp.concatenate([x1, x2], axis=0)

    heads = iter(pt[g * HEAD_DIM:(g + 1) * HEAD_DIM] for g in range(pt.shape[0] // HEAD_DIM))
    for h in range(A_HEADS):
        _store_qt(qa_ref, h, rope_a(_rms_t(next(heads), gq_ref[...], HEAD_DIM)) * scale)
    for h in range(A_KV):
        _store_k(ka_ref, h, rope_a(_rms_t(next(heads), gk_ref[...], HEAD_DIM)))
    for h in range(A_KV):
        _store_vt(va_ref, h, next(heads))
    for h in range(B_HEADS):
        _store_qt(qb_ref, h, rope_b(next(heads)) * scale)
    for h in range(B_KV):
        _store_k(kb_ref, h, rope_b(next(heads)))
    for h in range(B_KV):
        _store_vt(vb_ref, h, next(heads))


def _proj_ab(x, wt, gq, gk, tabs_a, tabs_b):
    B, S, D = x.shape
    tm = min(PROJ_ROWS, S)
    const = lambda b, i: (0, 0)
    tab = lambda t: pl.BlockSpec((t.shape[0], tm), lambda b, i: (0, i))
    hm = lambda h: pl.BlockSpec((1, h, tm, LANES), lambda b, i: (b, 0, i, 0))
    tr = lambda h: pl.BlockSpec((1, h, LANES, tm), lambda b, i: (b, 0, 0, i))
    sd = lambda h: jax.ShapeDtypeStruct((B, h, S, LANES), BF16)
    sdt = lambda h: jax.ShapeDtypeStruct((B, h, LANES, S), BF16)
    gq_b = jnp.broadcast_to(gq.reshape(-1, 1), (HEAD_DIM, tm))
    gk_b = jnp.broadcast_to(gk.reshape(-1, 1), (HEAD_DIM, tm))
    return pl.pallas_call(
        _proj_ab_kernel,
        grid=(B, S // tm),
        in_specs=[pl.BlockSpec((1, tm, D), lambda b, i: (b, i, 0)),
                  pl.BlockSpec(wt.shape, const),
                  pl.BlockSpec(gq_b.shape, const), pl.BlockSpec(gk_b.shape, const),
                  tab(tabs_a[0]), tab(tabs_a[1]), tab(tabs_b[0]), tab(tabs_b[1])],
        out_specs=[tr(A_HEADS), hm(A_KV), tr(A_KV), tr(B_HEADS), hm(B_KV), tr(B_KV)],
        out_shape=[sdt(A_HEADS), sd(A_KV), sdt(A_KV), sdt(B_HEADS), sd(B_KV), sdt(B_KV)],
        compiler_params=_cparams(("parallel", "parallel")),
        name="proj_ab",
    )(x, wt, gq_b, gk_b, *tabs_a, *tabs_b)


def _flash_kernel(qt_ref, k_ref, vt_ref, o_ref, m_sc, acc_sc, pv_sc, *, NKV, G, tq, tk, ks, nk, cw, vd):
    heads = NKV * G
    M = heads * tq
    n_sub = M // cw

    def q_tile(c):
        h, j = divmod(c * cw, tq)
        return qt_ref[0, h, :, j:j + cw]

    def kv_of(c):
        return (c * cw // tq) // G

    def cols(c):
        return slice(c * cw, (c + 1) * cw)

    for c in range(n_sub):
        s0 = jnp.dot(k_ref[0, kv_of(c), 0:LANES, :], q_tile(c), preferred_element_type=F32)
        m_sc[:, cols(c)] = jnp.max(s0, axis=0, keepdims=True)
    acc_sc[...] = jnp.zeros((LANES, M), F32)
    pv_sc[...] = jnp.zeros((LANES, M), F32)

    def body(kb, carry):
        off = pl.multiple_of(kb * tk, tk)
        acc_sc[...] += pv_sc[...]

        def keys(c, s):
            return k_ref[0, kv_of(c), pl.ds(pl.multiple_of(off + s * ks, ks), ks), :]

        def values_t(c, s=None):
            if s is None:
                return vt_ref[0, kv_of(c), :, pl.ds(off, tk)]
            return vt_ref[0, kv_of(c), :, pl.ds(pl.multiple_of(off + s * ks, ks), ks)]

        tiles = [(c, s) for c in range(n_sub) for s in range(tk // ks)]

        def scores(t):
            c, s = tiles[t]
            return jnp.dot(keys(c, s), q_tile(c), preferred_element_type=F32)

        excess = None
        ahead = {t: scores(t) for t in range(min(AHEAD, len(tiles)))}
        pts = []
        for t, (c, s) in enumerate(tiles):
            st = ahead.pop(t)
            if t + AHEAD < len(tiles):
                ahead[t + AHEAD] = scores(t + AHEAD)
            m = m_sc[:, cols(c)]
            over = jnp.max(st, axis=0, keepdims=True) - m
            excess = over if excess is None else jnp.maximum(excess, over)
            pts.append(jnp.exp2(st - m).astype(BF16))
            if s == tk // ks - 1:
                pt = pts[0] if len(pts) == 1 else jnp.concatenate(pts, axis=0)
                pv_sc[:, cols(c)] = jnp.dot(values_t(c), pt, preferred_element_type=F32)
                pts = []
        renew = jnp.max(excess) > SHIFT_SLACK

        @pl.when(renew)
        def _():
            pv_sc[...] = jnp.zeros((LANES, M), F32)
            for c, s in tiles:
                st = jnp.dot(keys(c, s), q_tile(c), preferred_element_type=F32)
                m_prev = m_sc[:, cols(c)]
                m_new = jnp.maximum(m_prev, jnp.max(st, axis=0, keepdims=True))
                alpha = jnp.exp2(m_prev - m_new)
                pt = jnp.exp2(st - m_new).astype(BF16)
                acc_sc[:, cols(c)] = (acc_sc[:, cols(c)] * alpha
                                      + jnp.dot(values_t(c, s), pt, preferred_element_type=F32))
                m_sc[:, cols(c)] = m_new

        return carry

    lax.fori_loop(0, nk, body, 0)
    acc = acc_sc[0:2 * vd, :] + pv_sc[0:2 * vd, :]
    o = acc[0:vd, :] / acc[vd:vd + 1, :]
    for p in range(heads // 2):
        pair = jnp.concatenate([o[:, 2 * p * tq:(2 * p + 1) * tq], o[:, (2 * p + 1) * tq:(2 * p + 2) * tq]], axis=0)
        o_ref[0, :, p * LANES:(p + 1) * LANES] = pair.T.astype(BF16)


def _flash(qt, k, vt, vd):
    B, H, _, S = qt.shape
    HK = k.shape[1]
    G = H // HK
    assert 2 * vd == LANES
    NKV = max(1, 2 // G)
    heads = NKV * G
    tq = min(ATTN_ROWS // heads, S)
    tk = min(ATTN_KEYS, S)
    M = heads * tq
    cw = min(ATTN_COLS, tq)
    ks = min(ATTN_KEY_SUB, tk)
    return pl.pallas_call(
        functools.partial(_flash_kernel, NKV=NKV, G=G, tq=tq, tk=tk, ks=ks, nk=S // tk, cw=cw, vd=vd),
        grid=(B, HK // NKV, S // tq),
        in_specs=[pl.BlockSpec((1, heads, LANES, tq), lambda b, h, i: (b, h, 0, i)),
                  pl.BlockSpec((1, NKV, S, LANES), lambda b, h, i: (b, h, 0, 0)),
                  pl.BlockSpec((1, NKV, LANES, S), lambda b, h, i: (b, h, 0, 0))],
        out_specs=pl.BlockSpec((1, tq, heads * vd), lambda b, h, i: (b, i, h)),
        out_shape=jax.ShapeDtypeStruct((B, S, H * vd), BF16),
        scratch_shapes=[pltpu.VMEM((1, M), F32), pltpu.VMEM((LANES, M), F32), pltpu.VMEM((LANES, M), F32)],
        compiler_params=_cparams(("parallel", "parallel", "parallel")),
        name="flash_attn",
    )(qt, k, vt)


def _window_kernel(sink_ref, qt_ref, k_ref, vt_ref, o_ref, *, G, tq, span, S, vd):
    kvh = pl.program_id(1)
    start = pl.program_id(2) * tq
    for j in range(tq // LANES):
        q0 = start + j * LANES
        kstart = pl.multiple_of(jnp.clip(q0 - WINDOW, 0, S - span), LANES)
        k = k_ref[0, 0, pl.ds(kstart, span), :]
        vt = vt_ref[0, 0, :, pl.ds(kstart, span)]
        kpos = kstart + lax.broadcasted_iota(jnp.int32, (span, 1), 0)
        qpos = q0 + lax.broadcasted_iota(jnp.int32, (1, LANES), 1)
        valid = jnp.abs(qpos - kpos) <= WINDOW
        outs = []
        for g in range(G):
            st = jnp.dot(k, qt_ref[0, g, :, j * LANES:(j + 1) * LANES], preferred_element_type=F32)
            st = jnp.where(valid, st, NEG_INF)
            sink = sink_ref[kvh * G + g] * LOG2E
            m = jnp.maximum(jnp.max(st, axis=0, keepdims=True), sink)
            pt = jnp.exp2(st - m).astype(BF16)
            acc = jnp.dot(vt, pt, preferred_element_type=F32)
            outs.append(acc[0:vd, :] / (acc[vd:vd + 1, :] + jnp.exp2(sink - m)))
        for p in range(G // 2):
            pair = jnp.concatenate([outs[2 * p], outs[2 * p + 1]], axis=0)
            o_ref[0, j * LANES:(j + 1) * LANES, p * LANES:(p + 1) * LANES] = pair.T.astype(BF16)


def _window(qt, k, vt, sink, vd):
    B, H, _, S = qt.shape
    HK = k.shape[1]
    G = H // HK
    tq = min(WIN_Q, S)
    span = min(LANES + 2 * WINDOW, S)
    return pl.pallas_call(
        functools.partial(_window_kernel, G=G, tq=tq, span=span, S=S, vd=vd),
        grid_spec=pltpu.PrefetchScalarGridSpec(
            num_scalar_prefetch=1,
            grid=(B, HK, S // tq),
            in_specs=[pl.BlockSpec((1, G, LANES, tq), lambda b, h, i, sk: (b, h, 0, i)),
                      pl.BlockSpec((1, 1, S, LANES), lambda b, h, i, sk: (b, h, 0, 0)),
                      pl.BlockSpec((1, 1, LANES, S), lambda b, h, i, sk: (b, h, 0, 0))],
            out_specs=pl.BlockSpec((1, tq, G * vd), lambda b, h, i, sk: (b, i, h))),
        out_shape=jax.ShapeDtypeStruct((B, S, H * vd), BF16),
        compiler_params=_cparams(("parallel", "parallel", "parallel")),
        name="window_attn",
    )(sink, qt, k, vt)


def _out_ln_router_kernel(*refs, n_in):
    o_refs = refs[:n_in]
    w_refs = refs[n_in:2 * n_in]
    x_ref, g_ref, b_ref, wrh_ref, wrl_ref, x1_ref, x1b_ref, aff_ref = refs[2 * n_in:]
    h = jnp.dot(o_refs[0][0], w_refs[0][...], preferred_element_type=F32)
    for i in range(1, n_in):
        h = h + jnp.dot(o_refs[i][0], w_refs[i][...], preferred_element_type=F32)
    x1 = _layer_norm(ALPHA * x_ref[0] + h, g_ref[...], b_ref[...])
    x1_ref[0] = x1
    x_hi = x1.astype(BF16)
    x1b_ref[0] = x_hi
    x_lo = (x1 - x_hi.astype(F32)).astype(BF16)
    logits = (lax.dot_general(wrh_ref[...], x_hi, NT_DIMS, preferred_element_type=F32)
              + lax.dot_general(wrh_ref[...], x_lo, NT_DIMS, preferred_element_type=F32)
              + lax.dot_general(wrl_ref[...], x_hi, NT_DIMS, preferred_element_type=F32))
    z = jnp.exp(logits - jnp.max(logits, axis=0, keepdims=True))
    aff_ref[0] = z / jnp.sum(z, axis=0, keepdims=True)


def _out_ln_router(os_, ws_, x, g, b, wr_hi, wr_lo):
    B, S, D = x.shape
    E = wr_hi.shape[0]
    tm = min(PROJ_ROWS, S)
    n_in = len(os_)
    row = lambda bb, i: (bb, i, 0)
    const = lambda bb, i: (0, 0)
    in_specs = ([pl.BlockSpec((1, tm, o.shape[2]), row) for o in os_]
                + [pl.BlockSpec(w.shape, const) for w in ws_]
                + [pl.BlockSpec((1, tm, D), row), pl.BlockSpec((1, D), const), pl.BlockSpec((1, D), const),
                   pl.BlockSpec((E, D), const), pl.BlockSpec((E, D), const)])
    return pl.pallas_call(
        functools.partial(_out_ln_router_kernel, n_in=n_in),
        grid=(B, S // tm),
        in_specs=in_specs,
        out_specs=[pl.BlockSpec((1, tm, D), row), pl.BlockSpec((1, tm, D), row),
                   pl.BlockSpec((1, E, tm), lambda bb, i: (bb, 0, i))],
        out_shape=[jax.ShapeDtypeStruct((B, S, D), F32), jax.ShapeDtypeStruct((B, S, D), BF16),
                   jax.ShapeDtypeStruct((B, E, S), F32)],
        compiler_params=_cparams(("parallel", "parallel")),
        name="out_ln_router",
    )(*os_, *ws_, x, g, b, wr_hi, wr_lo)


def _topk_kernel(aff_ref, tri_ref, rel_ref, cnt_ref, *, S, E, cap, T, CH):
    aff = aff_ref[0]
    bits = pltpu.bitcast(aff, jnp.int32)
    capf = jnp.float32(cap)

    def count(mask):
        return jnp.sum(jnp.where(mask, 1.0, 0.0), axis=1, keepdims=True)

    def thr_body(i, t):
        cand = t | jnp.left_shift(jnp.int32(1), 30 - i)
        return jnp.where(count(bits >= cand) >= capf, cand, t)

    thr = lax.fori_loop(0, 31, thr_body, jnp.zeros((E, 1), jnp.int32))
    gt = bits > thr
    ties = bits == thr
    need = capf - count(gt)
    idx = lax.broadcasted_iota(jnp.int32, (E, S), 1)
    nbits = max(1, (S - 1).bit_length())

    def cut_body(i, c):
        cand = c | jnp.left_shift(jnp.int32(1), nbits - 1 - i)
        return jnp.where(count(ties & (idx < cand)) < need, cand, c)

    cut = lax.fori_loop(0, nbits, cut_body, jnp.zeros((E, 1), jnp.int32))
    sel = gt | (ties & (idx <= cut))

    tri = tri_ref[...]
    nsc = S // T
    lane_sc = lax.broadcasted_iota(jnp.int32, (E, nsc), 1)
    cnt = jnp.zeros((E, nsc), F32)
    for sc in range(nsc):
        run = jnp.zeros((E, 1), F32)
        for ch in range(T // CH):
            lo = sc * T + ch * CH
            selc = jnp.where(sel[:, lo:lo + CH], 1.0, 0.0)
            incl = jnp.dot(selc.astype(BF16), tri, preferred_element_type=F32)
            rel = jnp.where(selc > 0.0, incl - 1.0 + run, -1.0)
            rel_ref[0, :, lo:lo + CH] = rel
            run = run + incl[:, CH - 1:CH]
        cnt = jnp.where(lane_sc == sc, run, cnt)
    cnt_ref[0] = cnt


def _topk(aff, T):
    B, E, S = aff.shape
    cap = EC_FACTOR * S // N_EXPERTS
    CH = min(CUMSUM_CHUNK, T)
    r = lax.broadcasted_iota(jnp.int32, (CH, CH), 0)
    c = lax.broadcasted_iota(jnp.int32, (CH, CH), 1)
    tri = jnp.where(r <= c, 1.0, 0.0).astype(BF16)
    nsc = S // T
    return pl.pallas_call(
        functools.partial(_topk_kernel, S=S, E=E, cap=cap, T=T, CH=CH),
        grid=(B,),
        in_specs=[pl.BlockSpec((1, E, S), lambda b: (b, 0, 0)),
                  pl.BlockSpec((CH, CH), lambda b: (0, 0))],
        out_specs=[pl.BlockSpec((1, E, S), lambda b: (b, 0, 0)),
                   pl.BlockSpec((1, E, nsc), lambda b: (b, 0, 0))],
        out_shape=[jax.ShapeDtypeStruct((B, E, S), F32), jax.ShapeDtypeStruct((B, E, nsc), F32)],
        compiler_params=_cparams(("parallel",)),
        name="topk_select",
    )(aff, tri)


def _moe_kernel(nblk_ref, xb_ref, relt_ref, afft_ref, rel_all_ref, wg_ref, wu_ref, wd_ref, x1_hbm, g_ref, b_ref,
                f_ref, y_all, x1_buf, x1_sem, *, T, NT, RB, E, nsc, TS):
    b, sg, e = pl.program_id(0), pl.program_id(1), pl.program_id(2)

    @pl.when(e == 0)
    def _():
        f_ref[...] = jnp.zeros(f_ref.shape, F32)

    def x1_copy(t):
        return pltpu.make_async_copy(x1_hbm.at[b, pl.ds((sg * NT + t) * T, T), :], x1_buf, x1_sem)

    @pl.when(e == E - 2)
    def _():
        x1_copy(0).start()

    for t in range(NT):
        tok = slice(t * T, (t + 1) * T)
        nb = nblk_ref[(b * nsc + sg * NT + t) * E + e]
        relt = relt_ref[0, 0, :, tok]
        afft = afft_ref[0, 0, :, tok]
        row_tiles = [slice(t * T + ts * TS, t * T + (ts + 1) * TS) for ts in range(T // TS)]
        first_rows = pl.ds(pl.multiple_of((t * E + e) * RB, RB), RB)

        def expert_rows(j, tok=tok, relt=relt, afft=afft):
            slot = (j * RB + lax.broadcasted_iota(jnp.int32, (RB, 1), 0)).astype(F32)
            hit = relt == slot
            gate = jnp.sum(jnp.where(hit, afft, 0.0), axis=1, keepdims=True)
            onehot = jnp.where(hit, 1.0, 0.0).astype(BF16)
            xg = jnp.dot(onehot, xb_ref[0, tok, :], preferred_element_type=F32).astype(BF16)
            hg = jnp.dot(xg, wg_ref[0], preferred_element_type=F32)
            hu = jnp.dot(xg, wu_ref[0], preferred_element_type=F32)
            h = (hg / (1.0 + jnp.exp(-hg)) * hu).astype(BF16)
            return onehot, (jnp.dot(h, wd_ref[0], preferred_element_type=F32) * gate).astype(BF16)

        @pl.when(nb == 0)
        def _(first_rows=first_rows):
            y_all[first_rows, :] = jnp.zeros((RB, y_all.shape[1]), BF16)

        @pl.when(nb > 0)
        def _(first_rows=first_rows, expert_rows=expert_rows):
            y_all[first_rows, :] = expert_rows(0)[1]

        def later_block(j, carry, expert_rows=expert_rows, row_tiles=row_tiles):
            onehot, y = expert_rows(j)
            for ts, rows in enumerate(row_tiles):
                cols = slice(ts * TS, (ts + 1) * TS)
                f_ref[0, rows, :] += lax.dot_general(onehot[:, cols], y, TN_DIMS, preferred_element_type=F32)
            return carry

        lax.fori_loop(1, nb, later_block, 0)

    @pl.when(e == E - 1)
    def _():
        slot = lax.broadcasted_iota(jnp.int32, (RB, 1), 0).astype(F32)
        for t in range(NT):
            ys = y_all[t * E * RB:(t + 1) * E * RB, :]
            row_tiles = [slice(t * T + ts * TS, t * T + (ts + 1) * TS) for ts in range(T // TS)]
            for rows in row_tiles:
                onehot_all = jnp.concatenate(
                    [jnp.where(rel_all_ref[0, ee, :, rows] == slot, 1.0, 0.0).astype(BF16) for ee in range(E)], axis=0)
                f_ref[0, rows, :] += lax.dot_general(onehot_all, ys, TN_DIMS, preferred_element_type=F32)
            x1_copy(t).wait()
            for ts, rows in enumerate(row_tiles):
                x1 = x1_buf[ts * TS:(ts + 1) * TS, :]
                f_ref[0, rows, :] = _layer_norm(ALPHA * x1 + f_ref[0, rows, :], g_ref[...], b_ref[...])
            if t + 1 < NT:
                x1_copy(t + 1).start()


def _moe(xb, relt, afft, nblk, wg, wu, wd, x1, g, b, T):
    B, S, D = xb.shape
    E, _, F = wg.shape
    nsc = S // T
    NT = min(MOE_TILES, nsc)
    assert nsc % NT == 0 and E >= 2
    RB = MOE_SLOTS
    TS = min(512, T)
    return pl.pallas_call(
        functools.partial(_moe_kernel, T=T, NT=NT, RB=RB, E=E, nsc=nsc, TS=TS),
        grid_spec=pltpu.PrefetchScalarGridSpec(
            num_scalar_prefetch=1,
            grid=(B, nsc // NT, E),
            in_specs=[pl.BlockSpec((1, NT * T, D), lambda b, s, e, n: (b, s, 0)),
                      pl.BlockSpec((1, 1, 1, NT * T), lambda b, s, e, n: (b, e, 0, s)),
                      pl.BlockSpec((1, 1, 1, NT * T), lambda b, s, e, n: (b, e, 0, s)),
                      pl.BlockSpec((1, E, 1, NT * T), lambda b, s, e, n: (b, 0, 0, s)),
                      pl.BlockSpec((1, D, F), lambda b, s, e, n: (e, 0, 0)),
                      pl.BlockSpec((1, D, F), lambda b, s, e, n: (e, 0, 0)),
                      pl.BlockSpec((1, F, D), lambda b, s, e, n: (e, 0, 0)),
                      pl.BlockSpec(memory_space=pl.ANY),
                      pl.BlockSpec((1, D), lambda b, s, e, n: (0, 0)),
                      pl.BlockSpec((1, D), lambda b, s, e, n: (0, 0))],
            out_specs=pl.BlockSpec((1, NT * T, D), lambda b, s, e, n: (b, s, 0)),
            scratch_shapes=[pltpu.VMEM((NT * E * RB, D), BF16), pltpu.VMEM((T, D), F32), pltpu.SemaphoreType.DMA(())]),
        out_shape=jax.ShapeDtypeStruct((B, S, D), F32),
        compiler_params=_cparams(("parallel", "parallel", "arbitrary")),
        name="moe_ffn",
    )(nblk, xb, relt, afft, relt, wg, wu, wd, x1, g, b)


def _proj_mla_kernel(x_ref, wdt_ref, gq_ref, gkv_ref, wqt_ref, wkvt_ref, c_ref, s_ref, qt_ref, k_ref, vt_ref):
    xb = x_ref[0].astype(BF16)
    dt = lax.dot_general(wdt_ref[...], xb, NT_DIMS, preferred_element_type=F32)
    cq = _rms_t(dt[0:Q_LORA], gq_ref[...], Q_LORA).astype(BF16)
    ckv = _rms_t(dt[Q_LORA:Q_LORA + KV_LORA], gkv_ref[...], KV_LORA).astype(BF16)
    r2 = QK_ROPE // 2
    c, s = c_ref[...], s_ref[...]
    kr = dt[Q_LORA + KV_LORA:Q_LORA + KV_LORA + QK_ROPE]
    kr1, kr2 = _rope_t(kr[0:r2], kr[r2:], c, s)
    qt = jnp.dot(wqt_ref[...], cq, preferred_element_type=F32)
    kvt = jnp.dot(wkvt_ref[...], ckv, preferred_element_type=F32)
    scale = (QK_NOPE + QK_ROPE) ** -0.5 * LOG2E
    dq, dkv = QK_NOPE + QK_ROPE, QK_NOPE + V_DIM
    for h in range(MLA_HEADS):
        q = qt[h * dq:(h + 1) * dq]
        q1, q2 = _rope_t(q[QK_NOPE:QK_NOPE + r2], q[QK_NOPE + r2:], c, s)
        _store_qt(qt_ref, h, jnp.concatenate([q[0:QK_NOPE], q1, q2], axis=0) * scale)
        kv = kvt[h * dkv:(h + 1) * dkv]
        _store_k(k_ref, h, jnp.concatenate([kv[0:QK_NOPE], kr1, kr2], axis=0))
        _store_vt(vt_ref, h, kv[QK_NOPE:])


def _proj_mla(x, wdt, gq, gkv, wqt, wkvt, tabs):
    B, S, D = x.shape
    tm = min(PROJ_ROWS, S)
    H = MLA_HEADS
    const = lambda b, i: (0, 0)
    full = lambda a: pl.BlockSpec(a.shape, const)
    tab = lambda t: pl.BlockSpec((t.shape[0], tm), lambda b, i: (0, i))
    gq_b = jnp.broadcast_to(gq.reshape(-1, 1), (Q_LORA, tm))
    gkv_b = jnp.broadcast_to(gkv.reshape(-1, 1), (KV_LORA, tm))
    return pl.pallas_call(
        _proj_mla_kernel,
        grid=(B, S // tm),
        in_specs=[pl.BlockSpec((1, tm, D), lambda b, i: (b, i, 0)),
                  full(wdt), full(gq_b), full(gkv_b), full(wqt), full(wkvt), tab(tabs[0]), tab(tabs[1])],
        out_specs=[pl.BlockSpec((1, H, LANES, tm), lambda b, i: (b, 0, 0, i)),
                   pl.BlockSpec((1, H, tm, LANES), lambda b, i: (b, 0, i, 0)),
                   pl.BlockSpec((1, H, LANES, tm), lambda b, i: (b, 0, 0, i))],
        out_shape=[jax.ShapeDtypeStruct((B, H, LANES, S), BF16),
                   jax.ShapeDtypeStruct((B, H, S, LANES), BF16),
                   jax.ShapeDtypeStruct((B, H, LANES, S), BF16)],
        compiler_params=_cparams(("parallel", "parallel")),
        name="proj_mla",
    )(x, wdt, gq_b, gkv_b, wqt, wkvt, *tabs)


def _angles(pos, dim):
    freqs = ROPE_THETA ** (-(jnp.arange(0, dim, 2, dtype=F32) / dim))
    return pos[:, None] * freqs[None, :]


def _tables(S):
    t = jnp.arange(S)
    ar = _angles((t // GRID_W).astype(F32), HEAD_DIM // 2)
    ac = _angles((t % GRID_W).astype(F32), HEAD_DIM // 2)
    axial = jnp.concatenate([ar, ac], axis=1).T
    seq = _angles(t.astype(F32), HEAD_DIM).T
    latent = _angles(t.astype(F32), QK_ROPE).T
    return tuple((jnp.cos(a), jnp.sin(a)) for a in (axial, seq, latent))


def _cast_kernel(*refs):
    n = len(refs) // 2
    for src, dst in zip(refs[:n], refs[n:]):
        dst[...] = src[...].astype(BF16)


def _experts_bf16(layer, *ws):
    return pl.pallas_call(
        _cast_kernel,
        grid=(ws[0].shape[1],),
        in_specs=[pl.BlockSpec((None, 1) + w.shape[2:], lambda e: (layer, e, 0, 0)) for w in ws],
        out_specs=[pl.BlockSpec((1,) + w.shape[2:], lambda e: (e, 0, 0)) for w in ws],
        out_shape=[jax.ShapeDtypeStruct(w.shape[1:], BF16) for w in ws],
        compiler_params=_cparams(("parallel",)),
        name="experts_bf16",
    )(*ws)


def _split_bf16(w):
    hi = w.astype(BF16)
    return hi, (w - hi.astype(F32)).astype(BF16)


def _moe_layer(layer, x1, x1b, aff, w_gate, w_up, w_down, g, b):
    B, S, D = x1.shape
    T = min(MOE_TOKENS, S)
    rel, cnt = _topk(aff, T)
    nblk = ((cnt.astype(jnp.int32) + MOE_SLOTS - 1) // MOE_SLOTS).transpose(0, 2, 1).reshape(-1)
    relt = rel.reshape(B, N_EXPERTS, 1, S)
    afft = aff.reshape(B, N_EXPERTS, 1, S)
    wg, wu, wd = _experts_bf16(layer, w_gate, w_up, w_down)
    return _moe(x1b, relt, afft, nblk, wg, wu, wd, x1, g, b, T)


def kernel(x, ab_w_in, ab_q_norm, ab_k_norm, ab_sink, ab_w_out, mla_w_down, mla_q_norm, mla_kv_norm,
           mla_w_uq, mla_w_ukv, mla_w_out, ln_mix_g, ln_mix_b, moe_router, moe_w_gate, moe_w_up, moe_w_down,
           ln_ffn_g, ln_ffn_b):
    B, S, D = x.shape
    tabs_a, tabs_b, tabs_m = _tables(S)
    row = lambda v: v.reshape(1, -1)

    qta, ka, vta, qtb, kb, vtb = _proj_ab(x, ab_w_in[0].T.astype(BF16), ab_q_norm[0], ab_k_norm[0], tabs_a, tabs_b)
    oa = _flash(qta, ka, vta, HEAD_DIM)
    ob = _window(qtb, kb, vtb, ab_sink[0], HEAD_DIM)
    n_a = A_HEADS * HEAD_DIM
    w_oa = ab_w_out[0][:n_a].astype(BF16)
    w_ob = ab_w_out[0][n_a:].astype(BF16)
    wr_hi, wr_lo = _split_bf16(moe_router[0].T)
    x1, x1b, aff = _out_ln_router([oa, ob], [w_oa, w_ob], x, row(ln_mix_g[0]), row(ln_mix_b[0]), wr_hi, wr_lo)
    x = _moe_layer(0, x1, x1b, aff, moe_w_gate, moe_w_up, moe_w_down, row(ln_ffn_g[0]), row(ln_ffn_b[0]))

    qt, k, vt = _proj_mla(x, mla_w_down[0].T.astype(BF16), mla_q_norm[0], mla_kv_norm[0],
                          mla_w_uq[0].T.astype(BF16), mla_w_ukv[0].T.astype(BF16), tabs_m)
    oc = _flash(qt, k, vt, V_DIM)
    w_oc = mla_w_out[0].astype(BF16)
    wr_hi, wr_lo = _split_bf16(moe_router[1].T)
    x1, x1b, aff = _out_ln_router([oc], [w_oc], x, row(ln_mix_g[1]), row(ln_mix_b[1]), wr_hi, wr_lo)
    x = _moe_layer(1, x1, x1b, aff, moe_w_gate, moe_w_up, moe_w_down, row(ln_ffn_g[1]), row(ln_ffn_b[1]))
    return x
```

```python
import functools

import jax
import jax.numpy as jnp
from jax import lax
from jax.experimental import pallas as pl
from jax.experimental.pallas import tpu as pltpu

F32 = jnp.float32
BF16 = jnp.bfloat16

GRID_W = 64
ROPE_THETA = 10000.0
HEAD_DIM = 64
A_HEADS, A_KV = 8, 2
B_HEADS, B_KV = 8, 2
WINDOW = 128
MLA_HEADS = 16
Q_LORA, KV_LORA = 256, 128
QK_NOPE, QK_ROPE, V_DIM = 64, 32, 64
N_EXPERTS = 16
EC_FACTOR = 2
DEPTH = 2
ALPHA = (2.0 * DEPTH) ** 0.25
NEG_INF = -1e30
SHIFT_SLACK = 64.0
RMS_EPS = 1e-6
LOG2E = 1.4426950408889634
LN_EPS = 1e-5

LANES = 128
VMEM_LIMIT = 56 * 1024 * 1024

PROJ_ROWS = 512
ATTN_ROWS = 4096
ATTN_KEYS = 2048
ATTN_KEY_SUB = 512
ATTN_COLS = 256
AHEAD = 4
WIN_Q = 1024
MOE_TOKENS = 1024
MOE_TILES = 2
MOE_SLOTS = 144
CUMSUM_CHUNK = 256

NT_DIMS = (((1,), (1,)), ((), ()))
TN_DIMS = (((0,), (0,)), ((), ()))


def _cparams(sem):
    return pltpu.CompilerParams(dimension_semantics=sem, vmem_limit_bytes=VMEM_LIMIT)


def _rope_t(x1, x2, c, s):
    return x1 * c - x2 * s, x2 * c + x1 * s


def _rms_t(x, g, n):
    ms = jnp.sum(x * x, axis=0, keepdims=True) * (1.0 / n)
    return x * lax.rsqrt(ms + RMS_EPS) * g


def _layer_norm(y, g, b):
    mu = jnp.mean(y, axis=1, keepdims=True)
    d = y - mu
    var = jnp.mean(d * d, axis=1, keepdims=True)
    return d * lax.rsqrt(var + LN_EPS) * g + b


def _store_qt(ref, h, q):
    d = q.shape[0]
    ref[0, h, 0:d, :] = q.astype(BF16)
    ref[0, h, d:LANES, :] = jnp.zeros((LANES - d, q.shape[1]), BF16)


def _store_k(ref, h, kt):
    d, t = kt.shape
    ref[0, h] = jnp.concatenate([kt, jnp.zeros((LANES - d, t), F32)], axis=0).T.astype(BF16)


def _store_vt(ref, h, vt):
    d, t = vt.shape
    ref[0, h, 0:d, :] = vt.astype(BF16)
    row = lax.broadcasted_iota(jnp.int32, (LANES - d, t), 0)
    ref[0, h, d:LANES, :] = jnp.where(row == 0, 1.0, 0.0).astype(BF16)


def _proj_ab_kernel(x_ref, wt_ref, gq_ref, gk_ref, ca_ref, sa_ref, cb_ref, sb_ref,
                    qa_ref, ka_ref, va_ref, qb_ref, kb_ref, vb_ref):
    xb = x_ref[0].astype(BF16)
    pt = lax.dot_general(wt_ref[...], xb, NT_DIMS, preferred_element_type=F32)
    scale = HEAD_DIM ** -0.5 * LOG2E
    q4, h2 = HEAD_DIM // 4, HEAD_DIM // 2
    ca, sa, cb, sb = ca_ref[...], sa_ref[...], cb_ref[...], sb_ref[...]

    def rope_a(p):
        r1, r2 = _rope_t(p[0:q4], p[q4:2 * q4], ca[0:q4], sa[0:q4])
        c1, c2 = _rope_t(p[2 * q4:3 * q4], p[3 * q4:], ca[q4:], sa[q4:])
        return jnp.concatenate([r1, r2, c1, c2], axis=0)

    def rope_b(p):
        x1, x2 = _rope_t(p[0:h2], p[h2:], cb, sb)
        return jnp.concatenate([x1, x2], axis=0)

    heads = iter(pt[g * HEAD_DIM:(g + 1) * HEAD_DIM] for g in range(pt.shape[0] // HEAD_DIM))
    for h in range(A_HEADS):
        _store_qt(qa_ref, h, rope_a(_rms_t(next(heads), gq_ref[...], HEAD_DIM)) * scale)
    for h in range(A_KV):
        _store_k(ka_ref, h, rope_a(_rms_t(next(heads), gk_ref[...], HEAD_DIM)))
    for h in range(A_KV):
        _store_vt(va_ref, h, next(heads))
    for h in range(B_HEADS):
        _store_qt(qb_ref, h, rope_b(next(heads)) * scale)
    for h in range(B_KV):
        _store_k(kb_ref, h, rope_b(next(heads)))
    for h in range(B_KV):
        _store_vt(vb_ref, h, next(heads))


def _proj_ab(x, wt, gq, gk, tabs_a, tabs_b):
    B, S, D = x.shape
    tm = min(PROJ_ROWS, S)
    const = lambda b, i: (0, 0)
    tab = lambda t: pl.BlockSpec((t.shape[0], tm), lambda b, i: (0, i))
    hm = lambda h: pl.BlockSpec((1, h, tm, LANES), lambda b, i: (b, 0, i, 0))
    tr = lambda h: pl.BlockSpec((1, h, LANES, tm), lambda b, i: (b, 0, 0, i))
    sd = lambda h: jax.ShapeDtypeStruct((B, h, S, LANES), BF16)
    sdt = lambda h: jax.ShapeDtypeStruct((B, h, LANES, S), BF16)
    gq_b = jnp.broadcast_to(gq.reshape(-1, 1), (HEAD_DIM, tm))
    gk_b = jnp.broadcast_to(gk.reshape(-1, 1), (HEAD_DIM, tm))
    return pl.pallas_call(
        _proj_ab_kernel,
        grid=(B, S // tm),
        in_specs=[pl.BlockSpec((1, tm, D), lambda b, i: (b, i, 0)),
                  pl.BlockSpec(wt.shape, const),
                  pl.BlockSpec(gq_b.shape, const), pl.BlockSpec(gk_b.shape, const),
                  tab(tabs_a[0]), tab(tabs_a[1]), tab(tabs_b[0]), tab(tabs_b[1])],
        out_specs=[tr(A_HEADS), hm(A_KV), tr(A_KV), tr(B_HEADS), hm(B_KV), tr(B_KV)],
        out_shape=[sdt(A_HEADS), sd(A_KV), sdt(A_KV), sdt(B_HEADS), sd(B_KV), sdt(B_KV)],
        compiler_params=_cparams(("parallel", "parallel")),
        name="proj_ab",
    )(x, wt, gq_b, gk_b, *tabs_a, *tabs_b)


def _flash_kernel(qt_ref, k_ref, vt_ref, o_ref, m_sc, acc_sc, pv_sc, *, NKV, G, tq, tk, ks, nk, cw, vd):
    heads = NKV * G
    M = heads * tq
    n_sub = M // cw

    def q_tile(c):
        h, j = divmod(c * cw, tq)
        return qt_ref[0, h, :, j:j + cw]

    def kv_of(c):
        return (c * cw // tq) // G

    def cols(c):
        return slice(c * cw, (c + 1) * cw)

    for c in range(n_sub):
        s0 = jnp.dot(k_ref[0, kv_of(c), 0:LANES, :], q_tile(c), preferred_element_type=F32)
        m_sc[:, cols(c)] = jnp.max(s0, axis=0, keepdims=True)
    acc_sc[...] = jnp.zeros((LANES, M), F32)
    pv_sc[...] = jnp.zeros((LANES, M), F32)

    def body(kb, carry):
        off = pl.multiple_of(kb * tk, tk)
        acc_sc[...] += pv_sc[...]

        def keys(c, s):
            return k_ref[0, kv_of(c), pl.ds(pl.multiple_of(off + s * ks, ks), ks), :]

        def values_t(c, s=None):
            if s is None:
                return vt_ref[0, kv_of(c), :, pl.ds(off, tk)]
            return vt_ref[0, kv_of(c), :, pl.ds(pl.multiple_of(off + s * ks, ks), ks)]

        tiles = [(c, s) for c in range(n_sub) for s in range(tk // ks)]

        def scores(t):
            c, s = tiles[t]
            return jnp.dot(keys(c, s), q_tile(c), preferred_element_type=F32)

        excess = None
        ahead = {t: scores(t) for t in range(min(AHEAD, len(tiles)))}
        pts = []
        for t, (c, s) in enumerate(tiles):
            st = ahead.pop(t)
            if t + AHEAD < len(tiles):
                ahead[t + AHEAD] = scores(t + AHEAD)
            m = m_sc[:, cols(c)]
            over = jnp.max(st, axis=0, keepdims=True) - m
            excess = over if excess is None else jnp.maximum(excess, over)
            pts.append(jnp.exp2(st - m).astype(BF16))
            if s == tk // ks - 1:
                pt = pts[0] if len(pts) == 1 else jnp.concatenate(pts, axis=0)
                pv_sc[:, cols(c)] = jnp.dot(values_t(c), pt, preferred_element_type=F32)
                pts = []
        renew = jnp.max(excess) > SHIFT_SLACK

        @pl.when(renew)
        def _():
            pv_sc[...] = jnp.zeros((LANES, M), F32)
            for c, s in tiles:
                st = jnp.dot(keys(c, s), q_tile(c), preferred_element_type=F32)
                m_prev = m_sc[:, cols(c)]
                m_new = jnp.maximum(m_prev, jnp.max(st, axis=0, keepdims=True))
                alpha = jnp.exp2(m_prev - m_new)
                pt = jnp.exp2(st - m_new).astype(BF16)
                acc_sc[:, cols(c)] = (acc_sc[:, cols(c)] * alpha
                                      + jnp.dot(values_t(c, s), pt, preferred_element_type=F32))
                m_sc[:, cols(c)] = m_new

        return carry

    lax.fori_loop(0, nk, body, 0)
    acc = acc_sc[0:2 * vd, :] + pv_sc[0:2 * vd, :]
    o = acc[0:vd, :] / acc[vd:vd + 1, :]
    for p in range(heads // 2):
        pair = jnp.concatenate([o[:, 2 * p * tq:(2 * p + 1) * tq], o[:, (2 * p + 1) * tq:(2 * p + 2) * tq]], axis=0)
        o_ref[0, :, p * LANES:(p + 1) * LANES] = pair.T.astype(BF16)


def _flash(qt, k, vt, vd):
    B, H, _, S = qt.shape
    HK = k.shape[1]
    G = H // HK
    assert 2 * vd == LANES
    NKV = max(1, 2 // G)
    heads = NKV * G
    tq = min(ATTN_ROWS // heads, S)
    tk = min(ATTN_KEYS, S)
    M = heads * tq
    cw = min(ATTN_COLS, tq)
    ks = min(ATTN_KEY_SUB, tk)
    return pl.pallas_call(
        functools.partial(_flash_kernel, NKV=NKV, G=G, tq=tq, tk=tk, ks=ks, nk=S // tk, cw=cw, vd=vd),
        grid=(B, HK // NKV, S // tq),
        in_specs=[pl.BlockSpec((1, heads, LANES, tq), lambda b, h, i: (b, h, 0, i)),
                  pl.BlockSpec((1, NKV, S, LANES), lambda b, h, i: (b, h, 0, 0)),
                  pl.BlockSpec((1, NKV, LANES, S), lambda b, h, i: (b, h, 0, 0))],
        out_specs=pl.BlockSpec((1, tq, heads * vd), lambda b, h, i: (b, i, h)),
        out_shape=jax.ShapeDtypeStruct((B, S, H * vd), BF16),
        scratch_shapes=[pltpu.VMEM((1, M), F32), pltpu.VMEM((LANES, M), F32), pltpu.VMEM((LANES, M), F32)],
        compiler_params=_cparams(("parallel", "parallel", "parallel")),
        name="flash_attn",
    )(qt, k, vt)


def _window_kernel(sink_ref, qt_ref, k_ref, vt_ref, o_ref, *, G, tq, span, S, vd):
    kvh = pl.program_id(1)
    start = pl.program_id(2) * tq
    for j in range(tq // LANES):
        q0 = start + j * LANES
        kstart = pl.multiple_of(jnp.clip(q0 - WINDOW, 0, S - span), LANES)
        k = k_ref[0, 0, pl.ds(kstart, span), :]
        vt = vt_ref[0, 0, :, pl.ds(kstart, span)]
        kpos = kstart + lax.broadcasted_iota(jnp.int32, (span, 1), 0)
        qpos = q0 + lax.broadcasted_iota(jnp.int32, (1, LANES), 1)
        valid = jnp.abs(qpos - kpos) <= WINDOW
        outs = []
        for g in range(G):
            st = jnp.dot(k, qt_ref[0, g, :, j * LANES:(j + 1) * LANES], preferred_element_type=F32)
            st = jnp.where(valid, st, NEG_INF)
            sink = sink_ref[kvh * G + g] * LOG2E
            m = jnp.maximum(jnp.max(st, axis=0, keepdims=True), sink)
            pt = jnp.exp2(st - m).astype(BF16)
            acc = jnp.dot(vt, pt, preferred_element_type=F32)
            outs.append(acc[0:vd, :] / (acc[vd:vd + 1, :] + jnp.exp2(sink - m)))
        for p in range(G // 2):
            pair = jnp.concatenate([outs[2 * p], outs[2 * p + 1]], axis=0)
            o_ref[0, j * LANES:(j + 1) * LANES, p * LANES:(p + 1) * LANES] = pair.T.astype(BF16)


def _window(qt, k, vt, sink, vd):
    B, H, _, S = qt.shape
    HK = k.shape[1]
    G = H // HK
    tq = min(WIN_Q, S)
    span = min(LANES + 2 * WINDOW, S)
    return pl.pallas_call(
        functools.partial(_window_kernel, G=G, tq=tq, span=span, S=S, vd=vd),
        grid_spec=pltpu.PrefetchScalarGridSpec(
            num_scalar_prefetch=1,
            grid=(B, HK, S // tq),
            in_specs=[pl.BlockSpec((1, G, LANES, tq), lambda b, h, i, sk: (b, h, 0, i)),
                      pl.BlockSpec((1, 1, S, LANES), lambda b, h, i, sk: (b, h, 0, 0)),
                      pl.BlockSpec((1, 1, LANES, S), lambda b, h, i, sk: (b, h, 0, 0))],
            out_specs=pl.BlockSpec((1, tq, G * vd), lambda b, h, i, sk: (b, i, h))),
        out_shape=jax.ShapeDtypeStruct((B, S, H * vd), BF16),
        compiler_params=_cparams(("parallel", "parallel", "parallel")),
        name="window_attn",
    )(sink, qt, k, vt)


def _out_ln_router_kernel(*refs, n_in):
    o_refs = refs[:n_in]
    w_refs = refs[n_in:2 * n_in]
    x_ref, g_ref, b_ref, wrh_ref, wrl_ref, x1_ref, x1b_ref, aff_ref = refs[2 * n_in:]
    h = jnp.dot(o_refs[0][0], w_refs[0][...], preferred_element_type=F32)
    for i in range(1, n_in):
        h = h + jnp.dot(o_refs[i][0], w_refs[i][...], preferred_element_type=F32)
    x1 = _layer_norm(ALPHA * x_ref[0] + h, g_ref[...], b_ref[...])
    x1_ref[0] = x1
    x_hi = x1.astype(BF16)
    x1b_ref[0] = x_hi
    x_lo = (x1 - x_hi.astype(F32)).astype(BF16)
    logits = (lax.dot_general(wrh_ref[...], x_hi, NT_DIMS, preferred_element_type=F32)
              + lax.dot_general(wrh_ref[...], x_lo, NT_DIMS, preferred_element_type=F32)
              + lax.dot_general(wrl_ref[...], x_hi, NT_DIMS, preferred_element_type=F32))
    z = jnp.exp(logits - jnp.max(logits, axis=0, keepdims=True))
    aff_ref[0] = z / jnp.sum(z, axis=0, keepdims=True)


def _out_ln_router(os_, ws_, x, g, b, wr_hi, wr_lo):
    B, S, D = x.shape
    E = wr_hi.shape[0]
    tm = min(PROJ_ROWS, S)
    n_in = len(os_)
    row = lambda bb, i: (bb, i, 0)
    const = lambda bb, i: (0, 0)
    in_specs = ([pl.BlockSpec((1, tm, o.shape[2]), row) for o in os_]
                + [pl.BlockSpec(w.shape, const) for w in ws_]
                + [pl.BlockSpec((1, tm, D), row), pl.BlockSpec((1, D), const), pl.BlockSpec((1, D), const),
                   pl.BlockSpec((E, D), const), pl.BlockSpec((E, D), const)])
    return pl.pallas_call(
        functools.partial(_out_ln_router_kernel, n_in=n_in),
        grid=(B, S // tm),
        in_specs=in_specs,
        out_specs=[pl.BlockSpec((1, tm, D), row), pl.BlockSpec((1, tm, D), row),
                   pl.BlockSpec((1, E, tm), lambda bb, i: (bb, 0, i))],
        out_shape=[jax.ShapeDtypeStruct((B, S, D), F32), jax.ShapeDtypeStruct((B, S, D), BF16),
                   jax.ShapeDtypeStruct((B, E, S), F32)],
        compiler_params=_cparams(("parallel", "parallel")),
        name="out_ln_router",
    )(*os_, *ws_, x, g, b, wr_hi, wr_lo)


def _topk_kernel(aff_ref, tri_ref, rel_ref, cnt_ref, *, S, E, cap, T, CH):
    aff = aff_ref[0]
    bits = pltpu.bitcast(aff, jnp.int32)
    capf = jnp.float32(cap)

    def count(mask):
        return jnp.sum(jnp.where(mask, 1.0, 0.0), axis=1, keepdims=True)

    def thr_body(i, t):
        cand = t | jnp.left_shift(jnp.int32(1), 30 - i)
        return jnp.where(count(bits >= cand) >= capf, cand, t)

    thr = lax.fori_loop(0, 31, thr_body, jnp.zeros((E, 1), jnp.int32))
    gt = bits > thr
    ties = bits == thr
    need = capf - count(gt)
    idx = lax.broadcasted_iota(jnp.int32, (E, S), 1)
    nbits = max(1, (S - 1).bit_length())

    def cut_body(i, c):
        cand = c | jnp.left_shift(jnp.int32(1), nbits - 1 - i)
        return jnp.where(count(ties & (idx < cand)) < need, cand, c)

    cut = lax.fori_loop(0, nbits, cut_body, jnp.zeros((E, 1), jnp.int32))
    sel = gt | (ties & (idx <= cut))

    tri = tri_ref[...]
    nsc = S // T
    lane_sc = lax.broadcasted_iota(jnp.int32, (E, nsc), 1)
    cnt = jnp.zeros((E, nsc), F32)
    for sc in range(nsc):
        run = jnp.zeros((E, 1), F32)
        for ch in range(T // CH):
            lo = sc * T + ch * CH
            selc = jnp.where(sel[:, lo:lo + CH], 1.0, 0.0)
            incl = jnp.dot(selc.astype(BF16), tri, preferred_element_type=F32)
            rel = jnp.where(selc > 0.0, incl - 1.0 + run, -1.0)
            rel_ref[0, :, lo:lo + CH] = rel
            run = run + incl[:, CH - 1:CH]
        cnt = jnp.where(lane_sc == sc, run, cnt)
    cnt_ref[0] = cnt


def _topk(aff, T):
    B, E, S = aff.shape
    cap = EC_FACTOR * S // N_EXPERTS
    CH = min(CUMSUM_CHUNK, T)
    r = lax.broadcasted_iota(jnp.int32, (CH, CH), 0)
    c = lax.broadcasted_iota(jnp.int32, (CH, CH), 1)
    tri = jnp.where(r <= c, 1.0, 0.0).astype(BF16)
    nsc = S // T
    return pl.pallas_call(
        functools.partial(_topk_kernel, S=S, E=E, cap=cap, T=T, CH=CH),
        grid=(B,),
        in_specs=[pl.BlockSpec((1, E, S), lambda b: (b, 0, 0)),
                  pl.BlockSpec((CH, CH), lambda b: (0, 0))],
        out_specs=[pl.BlockSpec((1, E, S), lambda b: (b, 0, 0)),
                   pl.BlockSpec((1, E, nsc), lambda b: (b, 0, 0))],
        out_shape=[jax.ShapeDtypeStruct((B, E, S), F32), jax.ShapeDtypeStruct((B, E, nsc), F32)],
        compiler_params=_cparams(("parallel",)),
        name="topk_select",
    )(aff, tri)


def _moe_kernel(nblk_ref, xb_ref, relt_ref, afft_ref, rel_all_ref, wg_ref, wu_ref, wd_ref, x1_hbm, g_ref, b_ref,
                f_ref, y_all, x1_buf, x1_sem, *, T, NT, RB, E, nsc, TS):
    b, sg, e = pl.program_id(0), pl.program_id(1), pl.program_id(2)

    @pl.when(e == 0)
    def _():
        f_ref[...] = jnp.zeros(f_ref.shape, F32)

    def x1_copy(t):
        return pltpu.make_async_copy(x1_hbm.at[b, pl.ds((sg * NT + t) * T, T), :], x1_buf, x1_sem)

    @pl.when(e == E - 2)
    def _():
        x1_copy(0).start()

    for t in range(NT):
        tok = slice(t * T, (t + 1) * T)
        nb = nblk_ref[(b * nsc + sg * NT + t) * E + e]
        relt = relt_ref[0, 0, :, tok]
        afft = afft_ref[0, 0, :, tok]
        row_tiles = [slice(t * T + ts * TS, t * T + (ts + 1) * TS) for ts in range(T // TS)]
        first_rows = pl.ds(pl.multiple_of((t * E + e) * RB, RB), RB)

        def expert_rows(j, tok=tok, relt=relt, afft=afft):
            slot = (j * RB + lax.broadcasted_iota(jnp.int32, (RB, 1), 0)).astype(F32)
            hit = relt == slot
            gate = jnp.sum(jnp.where(hit, afft, 0.0), axis=1, keepdims=True)
            onehot = jnp.where(hit, 1.0, 0.0).astype(BF16)
            xg = jnp.dot(onehot, xb_ref[0, tok, :], preferred_element_type=F32).astype(BF16)
            hg = jnp.dot(xg, wg_ref[0], preferred_element_type=F32)
            hu = jnp.dot(xg, wu_ref[0], preferred_element_type=F32)
            h = (hg / (1.0 + jnp.exp(-hg)) * hu).astype(BF16)
            return onehot, (jnp.dot(h, wd_ref[0], preferred_element_type=F32) * gate).astype(BF16)

        @pl.when(nb == 0)
        def _(first_rows=first_rows):
            y_all[first_rows, :] = jnp.zeros((RB, y_all.shape[1]), BF16)

        @pl.when(nb > 0)
        def _(first_rows=first_rows, expert_rows=expert_rows):
            y_all[first_rows, :] = expert_rows(0)[1]

        def later_block(j, carry, expert_rows=expert_rows, row_tiles=row_tiles):
            onehot, y = expert_rows(j)
            for ts, rows in enumerate(row_tiles):
                cols = slice(ts * TS, (ts + 1) * TS)
                f_ref[0, rows, :] += lax.dot_general(onehot[:, cols], y, TN_DIMS, preferred_element_type=F32)
            return carry

        lax.fori_loop(1, nb, later_block, 0)

    @pl.when(e == E - 1)
    def _():
        slot = lax.broadcasted_iota(jnp.int32, (RB, 1), 0).astype(F32)
        for t in range(NT):
            ys = y_all[t * E * RB:(t + 1) * E * RB, :]
            row_tiles = [slice(t * T + ts * TS, t * T + (ts + 1) * TS) for ts in range(T // TS)]
            for rows in row_tiles:
                onehot_all = jnp.concatenate(
                    [jnp.where(rel_all_ref[0, ee, :, rows] == slot, 1.0, 0.0).astype(BF16) for ee in range(E)], axis=0)
                f_ref[0, rows, :] += lax.dot_general(onehot_all, ys, TN_DIMS, preferred_element_type=F32)
            x1_copy(t).wait()
            for ts, rows in enumerate(row_tiles):
                x1 = x1_buf[ts * TS:(ts + 1) * TS, :]
                f_ref[0, rows, :] = _layer_norm(ALPHA * x1 + f_ref[0, rows, :], g_ref[...], b_ref[...])
            if t + 1 < NT:
                x1_copy(t + 1).start()


def _moe(xb, relt, afft, nblk, wg, wu, wd, x1, g, b, T):
    B, S, D = xb.shape
    E, _, F = wg.shape
    nsc = S // T
    NT = min(MOE_TILES, nsc)
    assert nsc % NT == 0 and E >= 2
    RB = MOE_SLOTS
    TS = min(512, T)
    return pl.pallas_call(
        functools.partial(_moe_kernel, T=T, NT=NT, RB=RB, E=E, nsc=nsc, TS=TS),
        grid_spec=pltpu.PrefetchScalarGridSpec(
            num_scalar_prefetch=1,
            grid=(B, nsc // NT, E),
            in_specs=[pl.BlockSpec((1, NT * T, D), lambda b, s, e, n: (b, s, 0)),
                      pl.BlockSpec((1, 1, 1, NT * T), lambda b, s, e, n: (b, e, 0, s)),
                      pl.BlockSpec((1, 1, 1, NT * T), lambda b, s, e, n: (b, e, 0, s)),
                      pl.BlockSpec((1, E, 1, NT * T), lambda b, s, e, n: (b, 0, 0, s)),
                      pl.BlockSpec((1, D, F), lambda b, s, e, n: (e, 0, 0)),
                      pl.BlockSpec((1, D, F), lambda b, s, e, n: (e, 0, 0)),
                      pl.BlockSpec((1, F, D), lambda b, s, e, n: (e, 0, 0)),
                      pl.BlockSpec(memory_space=pl.ANY),
                      pl.BlockSpec((1, D), lambda b, s, e, n: (0, 0)),
                      pl.BlockSpec((1, D), lambda b, s, e, n: (0, 0))],
            out_specs=pl.BlockSpec((1, NT * T, D), lambda b, s, e, n: (b, s, 0)),
            scratch_shapes=[pltpu.VMEM((NT * E * RB, D), BF16), pltpu.VMEM((T, D), F32), pltpu.SemaphoreType.DMA(())]),
        out_shape=jax.ShapeDtypeStruct((B, S, D), F32),
        compiler_params=_cparams(("parallel", "parallel", "arbitrary")),
        name="moe_ffn",
    )(nblk, xb, relt, afft, relt, wg, wu, wd, x1, g, b)


def _proj_mla_kernel(x_ref, wdt_ref, gq_ref, gkv_ref, wqt_ref, wkvt_ref, c_ref, s_ref, qt_ref, k_ref, vt_ref):
    xb = x_ref[0].astype(BF16)
    dt = lax.dot_general(wdt_ref[...], xb, NT_DIMS, preferred_element_type=F32)
    cq = _rms_t(dt[0:Q_LORA], gq_ref[...], Q_LORA).astype(BF16)
    ckv = _rms_t(dt[Q_LORA:Q_LORA + KV_LORA], gkv_ref[...], KV_LORA).astype(BF16)
    r2 = QK_ROPE // 2
    c, s = c_ref[...], s_ref[...]
    kr = dt[Q_LORA + KV_LORA:Q_LORA + KV_LORA + QK_ROPE]
    kr1, kr2 = _rope_t(kr[0:r2], kr[r2:], c, s)
    qt = jnp.dot(wqt_ref[...], cq, preferred_element_type=F32)
    kvt = jnp.dot(wkvt_ref[...], ckv, preferred_element_type=F32)
    scale = (QK_NOPE + QK_ROPE) ** -0.5 * LOG2E
    dq, dkv = QK_NOPE + QK_ROPE, QK_NOPE + V_DIM
    for h in range(MLA_HEADS):
        q = qt[h * dq:(h + 1) * dq]
        q1, q2 = _rope_t(q[QK_NOPE:QK_NOPE + r2], q[QK_NOPE + r2:], c, s)
        _store_qt(qt_ref, h, jnp.concatenate([q[0:QK_NOPE], q1, q2], axis=0) * scale)
        kv = kvt[h * dkv:(h + 1) * dkv]
        _store_k(k_ref, h, jnp.concatenate([kv[0:QK_NOPE], kr1, kr2], axis=0))
        _store_vt(vt_ref, h, kv[QK_NOPE:])


def _proj_mla(x, wdt, gq, gkv, wqt, wkvt, tabs):
    B, S, D = x.shape
    tm = min(PROJ_ROWS, S)
    H = MLA_HEADS
    const = lambda b, i: (0, 0)
    full = lambda a: pl.BlockSpec(a.shape, const)
    tab = lambda t: pl.BlockSpec((t.shape[0], tm), lambda b, i: (0, i))
    gq_b = jnp.broadcast_to(gq.reshape(-1, 1), (Q_LORA, tm))
    gkv_b = jnp.broadcast_to(gkv.reshape(-1, 1), (KV_LORA, tm))
    return pl.pallas_call(
        _proj_mla_kernel,
        grid=(B, S // tm),
        in_specs=[pl.BlockSpec((1, tm, D), lambda b, i: (b, i, 0)),
                  full(wdt), full(gq_b), full(gkv_b), full(wqt), full(wkvt), tab(tabs[0]), tab(tabs[1])],
        out_specs=[pl.BlockSpec((1, H, LANES, tm), lambda b, i: (b, 0, 0, i)),
                   pl.BlockSpec((1, H, tm, LANES), lambda b, i: (b, 0, i, 0)),
                   pl.BlockSpec((1, H, LANES, tm), lambda b, i: (b, 0, 0, i))],
        out_shape=[jax.ShapeDtypeStruct((B, H, LANES, S), BF16),
                   jax.ShapeDtypeStruct((B, H, S, LANES), BF16),
                   jax.ShapeDtypeStruct((B, H, LANES, S), BF16)],
        compiler_params=_cparams(("parallel", "parallel")),
        name="proj_mla",
    )(x, wdt, gq_b, gkv_b, wqt, wkvt, *tabs)


def _angles(pos, dim):
    freqs = ROPE_THETA ** (-(jnp.arange(0, dim, 2, dtype=F32) / dim))
    return pos[:, None] * freqs[None, :]


def _tables(S):
    t = jnp.arange(S)
    ar = _angles((t // GRID_W).astype(F32), HEAD_DIM // 2)
    ac = _angles((t % GRID_W).astype(F32), HEAD_DIM // 2)
    axial = jnp.concatenate([ar, ac], axis=1).T
    seq = _angles(t.astype(F32), HEAD_DIM).T
    latent = _angles(t.astype(F32), QK_ROPE).T
    return tuple((jnp.cos(a), jnp.sin(a)) for a in (axial, seq, latent))


def _cast_kernel(*refs):
    n = len(refs) // 2
    for src, dst in zip(refs[:n], refs[n:]):
        dst[...] = src[...].astype(BF16)


def _experts_bf16(layer, *ws):
    return pl.pallas_call(
        _cast_kernel,
        grid=(ws[0].shape[1],),
        in_specs=[pl.BlockSpec((None, 1) + w.shape[2:], lambda e: (layer, e, 0, 0)) for w in ws],
        out_specs=[pl.BlockSpec((1,) + w.shape[2:], lambda e: (e, 0, 0)) for w in ws],
        out_shape=[jax.ShapeDtypeStruct(w.shape[1:], BF16) for w in ws],
        compiler_params=_cparams(("parallel",)),
        name="experts_bf16",
    )(*ws)


def _split_bf16(w):
    hi = w.astype(BF16)
    return hi, (w - hi.astype(F32)).astype(BF16)


def _moe_layer(layer, x1, x1b, aff, w_gate, w_up, w_down, g, b):
    B, S, D = x1.shape
    T = min(MOE_TOKENS, S)
    rel, cnt = _topk(aff, T)
    nblk = ((cnt.astype(jnp.int32) + MOE_SLOTS - 1) // MOE_SLOTS).transpose(0, 2, 1).reshape(-1)
    relt = rel.reshape(B, N_EXPERTS, 1, S)
    afft = aff.reshape(B, N_EXPERTS, 1, S)
    wg, wu, wd = _experts_bf16(layer, w_gate, w_up, w_down)
    return _moe(x1b, relt, afft, nblk, wg, wu, wd, x1, g, b, T)


def kernel(x, ab_w_in, ab_q_norm, ab_k_norm, ab_sink, ab_w_out, mla_w_down, mla_q_norm, mla_kv_norm,
           mla_w_uq, mla_w_ukv, mla_w_out, ln_mix_g, ln_mix_b, moe_router, moe_w_gate, moe_w_up, moe_w_down,
           ln_ffn_g, ln_ffn_b):
    B, S, D = x.shape
    tabs_a, tabs_b, tabs_m = _tables(S)
    row = lambda v: v.reshape(1, -1)

    qta, ka, vta, qtb, kb, vtb = _proj_ab(x, ab_w_in[0].T.astype(BF16), ab_q_norm[0], ab_k_norm[0], tabs_a, tabs_b)
    oa = _flash(qta, ka, vta, HEAD_DIM)
    ob = _window(qtb, kb, vtb, ab_sink[0], HEAD_DIM)
    n_a = A_HEADS * HEAD_DIM
    w_oa = ab_w_out[0][:n_a].astype(BF16)
    w_ob = ab_w_out[0][n_a:].astype(BF16)
    wr_hi, wr_lo = _split_bf16(moe_router[0].T)
    x1, x1b, aff = _out_ln_router([oa, ob], [w_oa, w_ob], x, row(ln_mix_g[0]), row(ln_mix_b[0]), wr_hi, wr_lo)
    x = _moe_layer(0, x1, x1b, aff, moe_w_gate, moe_w_up, moe_w_down, row(ln_ffn_g[0]), row(ln_ffn_b[0]))

    qt, k, vt = _proj_mla(x, mla_w_down[0].T.astype(BF16), mla_q_norm[0], mla_kv_norm[0],
                          mla_w_uq[0].T.astype(BF16), mla_w_ukv[0].T.astype(BF16), tabs_m)
    oc = _flash(qt, k, vt, V_DIM)
    w_oc = mla_w_out[0].astype(BF16)
    wr_hi, wr_lo = _split_bf16(moe_router[1].T)
    x1, x1b, aff = _out_ln_router([oc], [w_oc], x, row(ln_mix_g[1]), row(ln_mix_b[1]), wr_hi, wr_lo)
    x = _moe_layer(1, x1, x1b, aff, moe_w_gate, moe_w_up, moe_w_down, row(ln_ffn_g[1]), row(ln_ffn_b[1]))
    return x
```

```python
import functools

import jax
import jax.numpy as jnp
from jax import lax
from jax.experimental import pallas as pl
from jax.experimental.pallas import tpu as pltpu

F32 = jnp.float32
BF16 = jnp.bfloat16

GRID_W = 64
ROPE_THETA = 10000.0
HEAD_DIM = 64
A_HEADS, A_KV = 8, 2
B_HEADS, B_KV = 8, 2
WINDOW = 128
MLA_HEADS = 16
Q_LORA, KV_LORA = 256, 128
QK_NOPE, QK_ROPE, V_DIM = 64, 32, 64
N_EXPERTS = 16
EC_FACTOR = 2
DEPTH = 2
ALPHA = (2.0 * DEPTH) ** 0.25
NEG_INF = -1e30
SHIFT_SLACK = 64.0
RMS_EPS = 1e-6
LOG2E = 1.4426950408889634
LN_EPS = 1e-5

LANES = 128
VMEM_LIMIT = 56 * 1024 * 1024

PROJ_ROWS = 512
ATTN_ROWS = 4096
ATTN_KEYS = 2048
ATTN_KEY_SUB = 512
ATTN_COLS = 256
AHEAD = 4
WIN_Q = 1024
MOE_TOKENS = 1024
MOE_TILES = 2
MOE_SLOTS = 144
CUMSUM_CHUNK = 256

NT_DIMS = (((1,), (1,)), ((), ()))
TN_DIMS = (((0,), (0,)), ((), ()))


def _cparams(sem):
    return pltpu.CompilerParams(dimension_semantics=sem, vmem_limit_bytes=VMEM_LIMIT)


def _rope_t(x1, x2, c, s):
    return x1 * c - x2 * s, x2 * c + x1 * s


def _rms_t(x, g, n):
    ms = jnp.sum(x * x, axis=0, keepdims=True) * (1.0 / n)
    return x * lax.rsqrt(ms + RMS_EPS) * g


def _layer_norm(y, g, b):
    mu = jnp.mean(y, axis=1, keepdims=True)
    d = y - mu
    var = jnp.mean(d * d, axis=1, keepdims=True)
    return d * lax.rsqrt(var + LN_EPS) * g + b


def _store_qt(ref, h, q):
    d = q.shape[0]
    ref[0, h, 0:d, :] = q.astype(BF16)
    ref[0, h, d:LANES, :] = jnp.zeros((LANES - d, q.shape[1]), BF16)


def _store_k(ref, h, kt):
    d, t = kt.shape
    ref[0, h] = jnp.concatenate([kt, jnp.zeros((LANES - d, t), F32)], axis=0).T.astype(BF16)


def _store_vt(ref, h, vt):
    d, t = vt.shape
    ref[0, h, 0:d, :] = vt.astype(BF16)
    row = lax.broadcasted_iota(jnp.int32, (LANES - d, t), 0)
    ref[0, h, d:LANES, :] = jnp.where(row == 0, 1.0, 0.0).astype(BF16)


def _proj_ab_kernel(x_ref, wt_ref, gq_ref, gk_ref, ca_ref, sa_ref, cb_ref, sb_ref,
                    qa_ref, ka_ref, va_ref, qb_ref, kb_ref, vb_ref):
    xb = x_ref[0].astype(BF16)
    pt = lax.dot_general(wt_ref[...], xb, NT_DIMS, preferred_element_type=F32)
    scale = HEAD_DIM ** -0.5 * LOG2E
    q4, h2 = HEAD_DIM // 4, HEAD_DIM // 2
    ca, sa, cb, sb = ca_ref[...], sa_ref[...], cb_ref[...], sb_ref[...]

    def rope_a(p):
        r1, r2 = _rope_t(p[0:q4], p[q4:2 * q4], ca[0:q4], sa[0:q4])
        c1, c2 = _rope_t(p[2 * q4:3 * q4], p[3 * q4:], ca[q4:], sa[q4:])
        return jnp.concatenate([r1, r2, c1, c2], axis=0)

    def rope_b(p):
        x1, x2 = _rope_t(p[0:h2], p[h2:], cb, sb)
        return jnp.concatenate([x1, x2], axis=0)

    heads = iter(pt[g * HEAD_DIM:(g + 1) * HEAD_DIM] for g in range(pt.shape[0] // HEAD_DIM))
    for h in range(A_HEADS):
        _store_qt(qa_ref, h, rope_a(_rms_t(next(heads), gq_ref[...], HEAD_DIM)) * scale)
    for h in range(A_KV):
        _store_k(ka_ref, h, rope_a(_rms_t(next(heads), gk_ref[...], HEAD_DIM)))
    for h in range(A_KV):
        _store_vt(va_ref, h, next(heads))
    for h in range(B_HEADS):
        _store_qt(qb_ref, h, rope_b(next(heads)) * scale)
    for h in range(B_KV):
        _store_k(kb_ref, h, rope_b(next(heads)))
    for h in range(B_KV):
        _store_vt(vb_ref, h, next(heads))


def _proj_ab(x, wt, gq, gk, tabs_a, tabs_b):
    B, S, D = x.shape
    tm = min(PROJ_ROWS, S)
    const = lambda b, i: (0, 0)
    tab = lambda t: pl.BlockSpec((t.shape[0], tm), lambda b, i: (0, i))
    hm = lambda h: pl.BlockSpec((1, h, tm, LANES), lambda b, i: (b, 0, i, 0))
    tr = lambda h: pl.BlockSpec((1, h, LANES, tm), lambda b, i: (b, 0, 0, i))
    sd = lambda h: jax.ShapeDtypeStruct((B, h, S, LANES), BF16)
    sdt = lambda h: jax.ShapeDtypeStruct((B, h, LANES, S), BF16)
    gq_b = jnp.broadcast_to(gq.reshape(-1, 1), (HEAD_DIM, tm))
    gk_b = jnp.broadcast_to(gk.reshape(-1, 1), (HEAD_DIM, tm))
    return pl.pallas_call(
        _proj_ab_kernel,
        grid=(B, S // tm),
        in_specs=[pl.BlockSpec((1, tm, D), lambda b, i: (b, i, 0)),
                  pl.BlockSpec(wt.shape, const),
                  pl.BlockSpec(gq_b.shape, const), pl.BlockSpec(gk_b.shape, const),
                  tab(tabs_a[0]), tab(tabs_a[1]), tab(tabs_b[0]), tab(tabs_b[1])],
        out_specs=[tr(A_HEADS), hm(A_KV), tr(A_KV), tr(B_HEADS), hm(B_KV), tr(B_KV)],
        out_shape=[sdt(A_HEADS), sd(A_KV), sdt(A_KV), sdt(B_HEADS), sd(B_KV), sdt(B_KV)],
        compiler_params=_cparams(("parallel", "parallel")),
        name="proj_ab",
    )(x, wt, gq_b, gk_b, *tabs_a, *tabs_b)


def _flash_kernel(qt_ref, k_ref, vt_ref, o_ref, m_sc, acc_sc, *, NKV, G, tq, tk, ks, nk, cw, vd):
    heads = NKV * G
    M = heads * tq
    n_sub = M // cw

    def q_tile(c):
        h, j = divmod(c * cw, tq)
        return qt_ref[0, h, :, j:j + cw]

    def kv_of(c):
        return (c * cw // tq) // G

    def cols(c):
        return slice(c * cw, (c + 1) * cw)

    for c in range(n_sub):
        s0 = jnp.dot(k_ref[0, kv_of(c), 0:LANES, :], q_tile(c), preferred_element_type=F32)
        m_sc[:, cols(c)] = jnp.max(s0, axis=0, keepdims=True)
    acc_sc[...] = jnp.zeros((LANES, M), F32)

    def block(kb):
        off = pl.multiple_of(kb * tk, tk)

        def keys(c, s):
            return k_ref[0, kv_of(c), pl.ds(pl.multiple_of(off + s * ks, ks), ks), :]

        def values_t(c, s=None):
            if s is None:
                return vt_ref[0, kv_of(c), :, pl.ds(off, tk)]
            return vt_ref[0, kv_of(c), :, pl.ds(pl.multiple_of(off + s * ks, ks), ks)]

        tiles = [(c, s) for c in range(n_sub) for s in range(tk // ks)]
        return keys, values_t, tiles

    def fixed_shift_body(kb, excess):
        keys, values_t, tiles = block(kb)

        def scores(t):
            c, s = tiles[t]
            return jnp.dot(keys(c, s), q_tile(c), preferred_element_type=F32)

        ahead = {t: scores(t) for t in range(min(AHEAD, len(tiles)))}
        pts = []
        for t, (c, s) in enumerate(tiles):
            st = ahead.pop(t)
            if t + AHEAD < len(tiles):
                ahead[t + AHEAD] = scores(t + AHEAD)
            m = m_sc[:, cols(c)]
            excess = jnp.maximum(excess, jnp.max(st, axis=0, keepdims=True) - m)
            pts.append(jnp.exp2(st - m).astype(BF16))
            if s == tk // ks - 1:
                pt = pts[0] if len(pts) == 1 else jnp.concatenate(pts, axis=0)
                acc_sc[:, cols(c)] += jnp.dot(values_t(c), pt, preferred_element_type=F32)
                pts = []
        return excess

    excess = lax.fori_loop(0, nk, fixed_shift_body, jnp.full((1, cw), -jnp.inf, F32))

    @pl.when(jnp.max(excess) > SHIFT_SLACK)
    def _():
        acc_sc[...] = jnp.zeros((LANES, M), F32)

        def renewing_body(kb, carry):
            keys, values_t, tiles = block(kb)
            for c, s in tiles:
                st = jnp.dot(keys(c, s), q_tile(c), preferred_element_type=F32)
                m_prev = m_sc[:, cols(c)]
                m_new = jnp.maximum(m_prev, jnp.max(st, axis=0, keepdims=True))
                alpha = jnp.exp2(m_prev - m_new)
                pt = jnp.exp2(st - m_new).astype(BF16)
                acc_sc[:, cols(c)] = (acc_sc[:, cols(c)] * alpha
                                      + jnp.dot(values_t(c, s), pt, preferred_element_type=F32))
                m_sc[:, cols(c)] = m_new
            return carry

        lax.fori_loop(0, nk, renewing_body, 0)

    acc = acc_sc[0:2 * vd, :]
    o = acc[0:vd, :] / acc[vd:vd + 1, :]
    for p in range(heads // 2):
        pair = jnp.concatenate([o[:, 2 * p * tq:(2 * p + 1) * tq], o[:, (2 * p + 1) * tq:(2 * p + 2) * tq]], axis=0)
        o_ref[0, :, p * LANES:(p + 1) * LANES] = pair.T.astype(BF16)


def _flash(qt, k, vt, vd):
    B, H, _, S = qt.shape
    HK = k.shape[1]
    G = H // HK
    assert 2 * vd == LANES
    NKV = max(1, 2 // G)
    heads = NKV * G
    tq = min(ATTN_ROWS // heads, S)
    tk = min(ATTN_KEYS, S)
    M = heads * tq
    cw = min(ATTN_COLS, tq)
    ks = min(ATTN_KEY_SUB, tk)
    return pl.pallas_call(
        functools.partial(_flash_kernel, NKV=NKV, G=G, tq=tq, tk=tk, ks=ks, nk=S // tk, cw=cw, vd=vd),
        grid=(B, HK // NKV, S // tq),
        in_specs=[pl.BlockSpec((1, heads, LANES, tq), lambda b, h, i: (b, h, 0, i)),
                  pl.BlockSpec((1, NKV, S, LANES), lambda b, h, i: (b, h, 0, 0)),
                  pl.BlockSpec((1, NKV, LANES, S), lambda b, h, i: (b, h, 0, 0))],
        out_specs=pl.BlockSpec((1, tq, heads * vd), lambda b, h, i: (b, i, h)),
        out_shape=jax.ShapeDtypeStruct((B, S, H * vd), BF16),
        scratch_shapes=[pltpu.VMEM((1, M), F32), pltpu.VMEM((LANES, M), F32)],
        compiler_params=_cparams(("parallel", "parallel", "parallel")),
        name="flash_attn",
    )(qt, k, vt)


def _window_kernel(sink_ref, qt_ref, k_ref, vt_ref, o_ref, *, G, tq, span, S, vd):
    kvh = pl.program_id(1)
    start = pl.program_id(2) * tq
    for j in range(tq // LANES):
        q0 = start + j * LANES
        kstart = pl.multiple_of(jnp.clip(q0 - WINDOW, 0, S - span), LANES)
        k = k_ref[0, 0, pl.ds(kstart, span), :]
        vt = vt_ref[0, 0, :, pl.ds(kstart, span)]
        kpos = kstart + lax.broadcasted_iota(jnp.int32, (span, 1), 0)
        qpos = q0 + lax.broadcasted_iota(jnp.int32, (1, LANES), 1)
        valid = jnp.abs(qpos - kpos) <= WINDOW
        outs = []
        for g in range(G):
            st = jnp.dot(k, qt_ref[0, g, :, j * LANES:(j + 1) * LANES], preferred_element_type=F32)
            st = jnp.where(valid, st, NEG_INF)
            sink = sink_ref[kvh * G + g] * LOG2E
            m = jnp.maximum(jnp.max(st, axis=0, keepdims=True), sink)
            pt = jnp.exp2(st - m).astype(BF16)
            acc = jnp.dot(vt, pt, preferred_element_type=F32)
            outs.append(acc[0:vd, :] / (acc[vd:vd + 1, :] + jnp.exp2(sink - m)))
        for p in range(G // 2):
            pair = jnp.concatenate([outs[2 * p], outs[2 * p + 1]], axis=0)
            o_ref[0, j * LANES:(j + 1) * LANES, p * LANES:(p + 1) * LANES] = pair.T.astype(BF16)


def _window(qt, k, vt, sink, vd):
    B, H, _, S = qt.shape
    HK = k.shape[1]
    G = H // HK
    tq = min(WIN_Q, S)
    span = min(LANES + 2 * WINDOW, S)
    return pl.pallas_call(
        functools.partial(_window_kernel, G=G, tq=tq, span=span, S=S, vd=vd),
        grid_spec=pltpu.PrefetchScalarGridSpec(
            num_scalar_prefetch=1,
            grid=(B, HK, S // tq),
            in_specs=[pl.BlockSpec((1, G, LANES, tq), lambda b, h, i, sk: (b, h, 0, i)),
                      pl.BlockSpec((1, 1, S, LANES), lambda b, h, i, sk: (b, h, 0, 0)),
                      pl.BlockSpec((1, 1, LANES, S), lambda b, h, i, sk: (b, h, 0, 0))],
            out_specs=pl.BlockSpec((1, tq, G * vd), lambda b, h, i, sk: (b, i, h))),
        out_shape=jax.ShapeDtypeStruct((B, S, H * vd), BF16),
        compiler_params=_cparams(("parallel", "parallel", "parallel")),
        name="window_attn",
    )(sink, qt, k, vt)


def _out_ln_router_kernel(*refs, n_in):
    o_refs = refs[:n_in]
    w_refs = refs[n_in:2 * n_in]
    x_ref, g_ref, b_ref, wrh_ref, wrl_ref, x1_ref, x1b_ref, aff_ref = refs[2 * n_in:]
    h = jnp.dot(o_refs[0][0], w_refs[0][...], preferred_element_type=F32)
    for i in range(1, n_in):
        h = h + jnp.dot(o_refs[i][0], w_refs[i][...], preferred_element_type=F32)
    x1 = _layer_norm(ALPHA * x_ref[0] + h, g_ref[...], b_ref[...])
    x1_ref[0] = x1
    x_hi = x1.astype(BF16)
    x1b_ref[0] = x_hi
    x_lo = (x1 - x_hi.astype(F32)).astype(BF16)
    logits = (lax.dot_general(wrh_ref[...], x_hi, NT_DIMS, preferred_element_type=F32)
              + lax.dot_general(wrh_ref[...], x_lo, NT_DIMS, preferred_element_type=F32)
              + lax.dot_general(wrl_ref[...], x_hi, NT_DIMS, preferred_element_type=F32))
    z = jnp.exp(logits - jnp.max(logits, axis=0, keepdims=True))
    aff_ref[0] = z / jnp.sum(z, axis=0, keepdims=True)


def _out_ln_router(os_, ws_, x, g, b, wr_hi, wr_lo):
    B, S, D = x.shape
    E = wr_hi.shape[0]
    tm = min(PROJ_ROWS, S)
    n_in = len(os_)
    row = lambda bb, i: (bb, i, 0)
    const = lambda bb, i: (0, 0)
    in_specs = ([pl.BlockSpec((1, tm, o.shape[2]), row) for o in os_]
                + [pl.BlockSpec(w.shape, const) for w in ws_]
                + [pl.BlockSpec((1, tm, D), row), pl.BlockSpec((1, D), const), pl.BlockSpec((1, D), const),
                   pl.BlockSpec((E, D), const), pl.BlockSpec((E, D), const)])
    return pl.pallas_call(
        functools.partial(_out_ln_router_kernel, n_in=n_in),
        grid=(B, S // tm),
        in_specs=in_specs,
        out_specs=[pl.BlockSpec((1, tm, D), row), pl.BlockSpec((1, tm, D), row),
                   pl.BlockSpec((1, E, tm), lambda bb, i: (bb, 0, i))],
        out_shape=[jax.ShapeDtypeStruct((B, S, D), F32), jax.ShapeDtypeStruct((B, S, D), BF16),
                   jax.ShapeDtypeStruct((B, E, S), F32)],
        compiler_params=_cparams(("parallel", "parallel")),
        name="out_ln_router",
    )(*os_, *ws_, x, g, b, wr_hi, wr_lo)


def _topk_kernel(aff_ref, tri_ref, rel_ref, cnt_ref, *, S, E, cap, T, CH):
    aff = aff_ref[0]
    bits = pltpu.bitcast(aff, jnp.int32)
    capf = jnp.float32(cap)

    def count(mask):
        return jnp.sum(jnp.where(mask, 1.0, 0.0), axis=1, keepdims=True)

    def thr_body(i, t):
        cand = t | jnp.left_shift(jnp.int32(1), 30 - i)
        return jnp.where(count(bits >= cand) >= capf, cand, t)

    thr = lax.fori_loop(0, 31, thr_body, jnp.zeros((E, 1), jnp.int32))
    gt = bits > thr
    ties = bits == thr
    need = capf - count(gt)
    idx = lax.broadcasted_iota(jnp.int32, (E, S), 1)
    nbits = max(1, (S - 1).bit_length())

    def cut_body(i, c):
        cand = c | jnp.left_shift(jnp.int32(1), nbits - 1 - i)
        return jnp.where(count(ties & (idx < cand)) < need, cand, c)

    cut = lax.fori_loop(0, nbits, cut_body, jnp.zeros((E, 1), jnp.int32))
    sel = gt | (ties & (idx <= cut))

    tri = tri_ref[...]
    nsc = S // T
    lane_sc = lax.broadcasted_iota(jnp.int32, (E, nsc), 1)
    cnt = jnp.zeros((E, nsc), F32)
    for sc in range(nsc):
        run = jnp.zeros((E, 1), F32)
        for ch in range(T // CH):
            lo = sc * T + ch * CH
            selc = jnp.where(sel[:, lo:lo + CH], 1.0, 0.0)
            incl = jnp.dot(selc.astype(BF16), tri, preferred_element_type=F32)
            rel = jnp.where(selc > 0.0, incl - 1.0 + run, -1.0)
            rel_ref[0, :, lo:lo + CH] = rel
            run = run + incl[:, CH - 1:CH]
        cnt = jnp.where(lane_sc == sc, run, cnt)
    cnt_ref[0] = cnt


def _topk(aff, T):
    B, E, S = aff.shape
    cap = EC_FACTOR * S // N_EXPERTS
    CH = min(CUMSUM_CHUNK, T)
    r = lax.broadcasted_iota(jnp.int32, (CH, CH), 0)
    c = lax.broadcasted_iota(jnp.int32, (CH, CH), 1)
    tri = jnp.where(r <= c, 1.0, 0.0).astype(BF16)
    nsc = S // T
    return pl.pallas_call(
        functools.partial(_topk_kernel, S=S, E=E, cap=cap, T=T, CH=CH),
        grid=(B,),
        in_specs=[pl.BlockSpec((1, E, S), lambda b: (b, 0, 0)),
                  pl.BlockSpec((CH, CH), lambda b: (0, 0))],
        out_specs=[pl.BlockSpec((1, E, S), lambda b: (b, 0, 0)),
                   pl.BlockSpec((1, E, nsc), lambda b: (b, 0, 0))],
        out_shape=[jax.ShapeDtypeStruct((B, E, S), F32), jax.ShapeDtypeStruct((B, E, nsc), F32)],
        compiler_params=_cparams(("parallel",)),
        name="topk_select",
    )(aff, tri)


def _moe_kernel(nblk_ref, xb_ref, relt_ref, afft_ref, rel_all_ref, wg_ref, wu_ref, wd_ref, x1_hbm, g_ref, b_ref,
                f_ref, y_all, x1_buf, x1_sem, *, T, NT, RB, E, nsc, TS):
    b, sg, e = pl.program_id(0), pl.program_id(1), pl.program_id(2)

    @pl.when(e == 0)
    def _():
        f_ref[...] = jnp.zeros(f_ref.shape, F32)

    def x1_copy(t):
        return pltpu.make_async_copy(x1_hbm.at[b, pl.ds((sg * NT + t) * T, T), :], x1_buf, x1_sem)

    @pl.when(e == E - 2)
    def _():
        x1_copy(0).start()

    for t in range(NT):
        tok = slice(t * T, (t + 1) * T)
        nb = nblk_ref[(b * nsc + sg * NT + t) * E + e]
        relt = relt_ref[0, 0, :, tok]
        afft = afft_ref[0, 0, :, tok]
        row_tiles = [slice(t * T + ts * TS, t * T + (ts + 1) * TS) for ts in range(T // TS)]
        first_rows = pl.ds(pl.multiple_of((t * E + e) * RB, RB), RB)

        def expert_rows(j, tok=tok, relt=relt, afft=afft):
            slot = (j * RB + lax.broadcasted_iota(jnp.int32, (RB, 1), 0)).astype(F32)
            hit = relt == slot
            gate = jnp.sum(jnp.where(hit, afft, 0.0), axis=1, keepdims=True)
            onehot = jnp.where(hit, 1.0, 0.0).astype(BF16)
            xg = jnp.dot(onehot, xb_ref[0, tok, :], preferred_element_type=F32).astype(BF16)
            hg = jnp.dot(xg, wg_ref[0], preferred_element_type=F32)
            hu = jnp.dot(xg, wu_ref[0], preferred_element_type=F32)
            h = (hg / (1.0 + jnp.exp(-hg)) * hu).astype(BF16)
            return onehot, (jnp.dot(h, wd_ref[0], preferred_element_type=F32) * gate).astype(BF16)

        @pl.when(nb == 0)
        def _(first_rows=first_rows):
            y_all[first_rows, :] = jnp.zeros((RB, y_all.shape[1]), BF16)

        @pl.when(nb > 0)
        def _(first_rows=first_rows, expert_rows=expert_rows):
            y_all[first_rows, :] = expert_rows(0)[1]

        def later_block(j, carry, expert_rows=expert_rows, row_tiles=row_tiles):
            onehot, y = expert_rows(j)
            for ts, rows in enumerate(row_tiles):
                cols = slice(ts * TS, (ts + 1) * TS)
                f_ref[0, rows, :] += lax.dot_general(onehot[:, cols], y, TN_DIMS, preferred_element_type=F32)
            return carry

        lax.fori_loop(1, nb, later_block, 0)

    @pl.when(e == E - 1)
    def _():
        slot = lax.broadcasted_iota(jnp.int32, (RB, 1), 0).astype(F32)
        for t in range(NT):
            ys = y_all[t * E * RB:(t + 1) * E * RB, :]
            row_tiles = [slice(t * T + ts * TS, t * T + (ts + 1) * TS) for ts in range(T // TS)]
            for rows in row_tiles:
                onehot_all = jnp.concatenate(
                    [jnp.where(rel_all_ref[0, ee, :, rows] == slot, 1.0, 0.0).astype(BF16) for ee in range(E)], axis=0)
                f_ref[0, rows, :] += lax.dot_general(onehot_all, ys, TN_DIMS, preferred_element_type=F32)
            x1_copy(t).wait()
            for ts, rows in enumerate(row_tiles):
                x1 = x1_buf[ts * TS:(ts + 1) * TS, :]
                f_ref[0, rows, :] = _layer_norm(ALPHA * x1 + f_ref[0, rows, :], g_ref[...], b_ref[...])
            if t + 1 < NT:
                x1_copy(t + 1).start()


def _moe(xb, relt, afft, nblk, wg, wu, wd, x1, g, b, T):
    B, S, D = xb.shape
    E, _, F = wg.shape
    nsc = S // T
    NT = min(MOE_TILES, nsc)
    assert nsc % NT == 0 and E >= 2
    RB = MOE_SLOTS
    TS = min(512, T)
    return pl.pallas_call(
        functools.partial(_moe_kernel, T=T, NT=NT, RB=RB, E=E, nsc=nsc, TS=TS),
        grid_spec=pltpu.PrefetchScalarGridSpec(
            num_scalar_prefetch=1,
            grid=(B, nsc // NT, E),
            in_specs=[pl.BlockSpec((1, NT * T, D), lambda b, s, e, n: (b, s, 0)),
                      pl.BlockSpec((1, 1, 1, NT * T), lambda b, s, e, n: (b, e, 0, s)),
                      pl.BlockSpec((1, 1, 1, NT * T), lambda b, s, e, n: (b, e, 0, s)),
                      pl.BlockSpec((1, E, 1, NT * T), lambda b, s, e, n: (b, 0, 0, s)),
                      pl.BlockSpec((1, D, F), lambda b, s, e, n: (e, 0, 0)),
                      pl.BlockSpec((1, D, F), lambda b, s, e, n: (e, 0, 0)),
                      pl.BlockSpec((1, F, D), lambda b, s, e, n: (e, 0, 0)),
                      pl.BlockSpec(memory_space=pl.ANY),
                      pl.BlockSpec((1, D), lambda b, s, e, n: (0, 0)),
                      pl.BlockSpec((1, D), lambda b, s, e, n: (0, 0))],
            out_specs=pl.BlockSpec((1, NT * T, D), lambda b, s, e, n: (b, s, 0)),
            scratch_shapes=[pltpu.VMEM((NT * E * RB, D), BF16), pltpu.VMEM((T, D), F32), pltpu.SemaphoreType.DMA(())]),
        out_shape=jax.ShapeDtypeStruct((B, S, D), F32),
        compiler_params=_cparams(("parallel", "parallel", "arbitrary")),
        name="moe_ffn",
    )(nblk, xb, relt, afft, relt, wg, wu, wd, x1, g, b)


def _proj_mla_kernel(x_ref, wdt_ref, gq_ref, gkv_ref, wqt_ref, wkvt_ref, c_ref, s_ref, qt_ref, k_ref, vt_ref):
    xb = x_ref[0].astype(BF16)
    dt = lax.dot_general(wdt_ref[...], xb, NT_DIMS, preferred_element_type=F32)
    cq = _rms_t(dt[0:Q_LORA], gq_ref[...], Q_LORA).astype(BF16)
    ckv = _rms_t(dt[Q_LORA:Q_LORA + KV_LORA], gkv_ref[...], KV_LORA).astype(BF16)
    r2 = QK_ROPE // 2
    c, s = c_ref[...], s_ref[...]
    kr = dt[Q_LORA + KV_LORA:Q_LORA + KV_LORA + QK_ROPE]
    kr1, kr2 = _rope_t(kr[0:r2], kr[r2:], c, s)
    qt = jnp.dot(wqt_ref[...], cq, preferred_element_type=F32)
    kvt = jnp.dot(wkvt_ref[...], ckv, preferred_element_type=F32)
    scale = (QK_NOPE + QK_ROPE) ** -0.5 * LOG2E
    dq, dkv = QK_NOPE + QK_ROPE, QK_NOPE + V_DIM
    for h in range(MLA_HEADS):
        q = qt[h * dq:(h + 1) * dq]
        q1, q2 = _rope_t(q[QK_NOPE:QK_NOPE + r2], q[QK_NOPE + r2:], c, s)
        _store_qt(qt_ref, h, jnp.concatenate([q[0:QK_NOPE], q1, q2], axis=0) * scale)
        kv = kvt[h * dkv:(h + 1) * dkv]
        _store_k(k_ref, h, jnp.concatenate([kv[0:QK_NOPE], kr1, kr2], axis=0))
        _store_vt(vt_ref, h, kv[QK_NOPE:])


def _proj_mla(x, wdt, gq, gkv, wqt, wkvt, tabs):
    B, S, D = x.shape
    tm = min(PROJ_ROWS, S)
    H = MLA_HEADS
    const = lambda b, i: (0, 0)
    full = lambda a: pl.BlockSpec(a.shape, const)
    tab = lambda t: pl.BlockSpec((t.shape[0], tm), lambda b, i: (0, i))
    gq_b = jnp.broadcast_to(gq.reshape(-1, 1), (Q_LORA, tm))
    gkv_b = jnp.broadcast_to(gkv.reshape(-1, 1), (KV_LORA, tm))
    return pl.pallas_call(
        _proj_mla_kernel,
        grid=(B, S // tm),
        in_specs=[pl.BlockSpec((1, tm, D), lambda b, i: (b, i, 0)),
                  full(wdt), full(gq_b), full(gkv_b), full(wqt), full(wkvt), tab(tabs[0]), tab(tabs[1])],
        out_specs=[pl.BlockSpec((1, H, LANES, tm), lambda b, i: (b, 0, 0, i)),
                   pl.BlockSpec((1, H, tm, LANES), lambda b, i: (b, 0, i, 0)),
                   pl.BlockSpec((1, H, LANES, tm), lambda b, i: (b, 0, 0, i))],
        out_shape=[jax.ShapeDtypeStruct((B, H, LANES, S), BF16),
                   jax.ShapeDtypeStruct((B, H, S, LANES), BF16),
                   jax.ShapeDtypeStruct((B, H, LANES, S), BF16)],
        compiler_params=_cparams(("parallel", "parallel")),
        name="proj_mla",
    )(x, wdt, gq_b, gkv_b, wqt, wkvt, *tabs)


def _angles(pos, dim):
    freqs = ROPE_THETA ** (-(jnp.arange(0, dim, 2, dtype=F32) / dim))
    return pos[:, None] * freqs[None, :]


def _tables(S):
    t = jnp.arange(S)
    ar = _angles((t // GRID_W).astype(F32), HEAD_DIM // 2)
    ac = _angles((t % GRID_W).astype(F32), HEAD_DIM // 2)
    axial = jnp.concatenate([ar, ac], axis=1).T
    seq = _angles(t.astype(F32), HEAD_DIM).T
    latent = _angles(t.astype(F32), QK_ROPE).T
    return tuple((jnp.cos(a), jnp.sin(a)) for a in (axial, seq, latent))


def _cast_kernel(*refs):
    n = len(refs) // 2
    for src, dst in zip(refs[:n], refs[n:]):
        dst[...] = src[...].astype(BF16)


def _experts_bf16(layer, *ws):
    return pl.pallas_call(
        _cast_kernel,
        grid=(ws[0].shape[1],),
        in_specs=[pl.BlockSpec((None, 1) + w.shape[2:], lambda e: (layer, e, 0, 0)) for w in ws],
        out_specs=[pl.BlockSpec((1,) + w.shape[2:], lambda e: (e, 0, 0)) for w in ws],
        out_shape=[jax.ShapeDtypeStruct(w.shape[1:], BF16) for w in ws],
        compiler_params=_cparams(("parallel",)),
        name="experts_bf16",
    )(*ws)


def _split_bf16(w):
    hi = w.astype(BF16)
    return hi, (w - hi.astype(F32)).astype(BF16)


def _moe_layer(layer, x1, x1b, aff, w_gate, w_up, w_down, g, b):
    B, S, D = x1.shape
    T = min(MOE_TOKENS, S)
    rel, cnt = _topk(aff, T)
    nblk = ((cnt.astype(jnp.int32) + MOE_SLOTS - 1) // MOE_SLOTS).transpose(0, 2, 1).reshape(-1)
    relt = rel.reshape(B, N_EXPERTS, 1, S)
    afft = aff.reshape(B, N_EXPERTS, 1, S)
    wg, wu, wd = _experts_bf16(layer, w_gate, w_up, w_down)
    return _moe(x1b, relt, afft, nblk, wg, wu, wd, x1, g, b, T)


def kernel(x, ab_w_in, ab_q_norm, ab_k_norm, ab_sink, ab_w_out, mla_w_down, mla_q_norm, mla_kv_norm,
           mla_w_uq, mla_w_ukv, mla_w_out, ln_mix_g, ln_mix_b, moe_router, moe_w_gate, moe_w_up, moe_w_down,
           ln_ffn_g, ln_ffn_b):
    B, S, D = x.shape
    tabs_a, tabs_b, tabs_m = _tables(S)
    row = lambda v: v.reshape(1, -1)

    qta, ka, vta, qtb, kb, vtb = _proj_ab(x, ab_w_in[0].T.astype(BF16), ab_q_norm[0], ab_k_norm[0], tabs_a, tabs_b)
    oa = _flash(qta, ka, vta, HEAD_DIM)
    ob = _window(qtb, kb, vtb, ab_sink[0], HEAD_DIM)
    n_a = A_HEADS * HEAD_DIM
    w_oa = ab_w_out[0][:n_a].astype(BF16)
    w_ob = ab_w_out[0][n_a:].astype(BF16)
    wr_hi, wr_lo = _split_bf16(moe_router[0].T)
    x1, x1b, aff = _out_ln_router([oa, ob], [w_oa, w_ob], x, row(ln_mix_g[0]), row(ln_mix_b[0]), wr_hi, wr_lo)
    x = _moe_layer(0, x1, x1b, aff, moe_w_gate, moe_w_up, moe_w_down, row(ln_ffn_g[0]), row(ln_ffn_b[0]))

    qt, k, vt = _proj_mla(x, mla_w_down[0].T.astype(BF16), mla_q_norm[0], mla_kv_norm[0],
                          mla_w_uq[0].T.astype(BF16), mla_w_ukv[0].T.astype(BF16), tabs_m)
    oc = _flash(qt, k, vt, V_DIM)
    w_oc = mla_w_out[0].astype(BF16)
    wr_hi, wr_lo = _split_bf16(moe_router[1].T)
    x1, x1b, aff = _out_ln_router([oc], [w_oc], x, row(ln_mix_g[1]), row(ln_mix_b[1]), wr_hi, wr_lo)
    x = _moe_layer(1, x1, x1b, aff, moe_w_gate, moe_w_up, moe_w_down, row(ln_ffn_g[1]), row(ln_ffn_b[1]))
    return x
```
